```python
import math
import jax
import jax.numpy as jnp
from jax import lax
import numpy as np

D_MODEL = 1024
BATCH = 4
SEQ = 4096
DEPTH = 4

NORM_EPS = 1e-6

HG_WIDTH = D_MODEL
HG_EXPAND = 128
HG_HEADS = HG_WIDTH // HG_EXPAND
HG_DK = HG_EXPAND
HG_DV = HG_WIDTH // HG_HEADS
HG_CHUNK = 64

RW_WIDTH = D_MODEL
RW_HEAD = 64
RW_HEADS = RW_WIDTH // RW_HEAD
RW_DECAY_LORA = max(32, int(round(1.8 * D_MODEL ** 0.5 / 32)) * 32)
RW_AAA_LORA = max(32, int(round(1.8 * D_MODEL ** 0.5 / 32)) * 32)
RW_MV_LORA = max(32, int(round(1.3 * D_MODEL ** 0.5 / 32)) * 32)
RW_GATE_LORA = max(32, int(round(0.6 * D_MODEL ** 0.8 / 32)) * 32)
RW_GN_EPS = 64e-5

S5_WIDTH = D_MODEL
S5_GROUP = 16
S5_GROUPS = S5_WIDTH // S5_GROUP
S5_STATE = 64
S5_DT_MIN = 1e-3
S5_DT_MAX = 1e-1

FFN_HIDDEN = ((8 * D_MODEL + 3 * 256 - 1) // (3 * 256)) * 256

N_BRANCH = 3
HG_IN = 4 * HG_WIDTH
RW_IN = 3 * RW_WIDTH + RW_DECAY_LORA + RW_AAA_LORA + RW_GATE_LORA
OFF_HG = 0
OFF_RW = OFF_HG + HG_IN
OFF_S5 = OFF_RW + RW_IN
OFF_GATE = OFF_S5 + S5_WIDTH
IN_WIDTH = OFF_GATE + N_BRANCH * D_MODEL
MIX_WIDTH = HG_WIDTH + RW_WIDTH + S5_WIDTH

kernel_name = "hgrn2_rwkv7_s5_gated_hybrid"


def rmsnorm(x, g):
    xf = x.astype(jnp.float32)
    y = xf * lax.rsqrt(jnp.mean(xf * xf, axis=-1, keepdims=True) + NORM_EPS)
    return (y * g.astype(jnp.float32)).astype(x.dtype)


def token_shift(z):
    return jnp.pad(z, ((0, 0), (1, 0), (0, 0)))[:, :-1]


def hgrn2_mixer(q, f_logit, i, og, lb, onorm):
    B, S, _ = q.shape
    f32 = jnp.float32
    q = jax.nn.silu(q.astype(f32))
    lb = jnp.maximum(lb.astype(f32), 0.0)
    log_f = jnp.logaddexp(jnp.log(lb), jnp.log1p(-lb) + jax.nn.log_sigmoid(f_logit.astype(f32)))
    k = -jnp.expm1(log_f)
    v = i.astype(f32)
    nc = S // HG_CHUNK

    def to_chunks(z, d):
        return z.reshape(B, nc, HG_CHUNK, HG_HEADS, d).transpose(1, 0, 3, 2, 4)

    qc, kc, gc = to_chunks(q, HG_DK), to_chunks(k, HG_DK), to_chunks(log_f, HG_DK)
    vc = to_chunks(v, HG_DV)
    causal = jnp.tril(jnp.ones((HG_CHUNK, HG_CHUNK), dtype=bool))

    def step(state, inp):
        qb, kb, vb, gb = inp
        b = jnp.cumsum(gb, axis=2)
        diff = b[:, :, :, None, :] - b[:, :, None, :, :]
        decay = jnp.exp(jnp.where(causal[:, :, None], diff, -jnp.inf))
        scores = jnp.einsum('bhtk,bhsk,bhtsk->bhts', qb, kb, decay)
        o = jnp.einsum('bhts,bhsv->bhtv', scores, vb) + jnp.einsum('bhtk,bhkv->bhtv', qb * jnp.exp(b), state)
        b_last = b[:, :, -1:, :]
        state = jnp.exp(b_last[:, :, 0, :])[..., None] * state + jnp.einsum('bhsk,bhsv->bhkv', kb * jnp.exp(b_last - b), vb)
        return state, o

    s0 = jnp.zeros((B, HG_HEADS, HG_DK, HG_DV), f32)
    _, oc = lax.scan(step, s0, (qc, kc, vc, gc))
    o = oc.transpose(1, 0, 3, 2, 4).reshape(B, S, HG_HEADS, HG_DV)
    o = o * lax.rsqrt(jnp.mean(o * o, axis=-1, keepdims=True) + NORM_EPS)
    return o.reshape(B, S, HG_WIDTH) * onorm * jax.nn.silu(og.astype(f32))


def rwkv7_mixer(feats, v_first, vres, mu, w0, w2, a0, a2, g2, k_k, k_a, r_k, ln_w, ln_b):
    B, S, _ = feats.shape
    f32 = jnp.float32
    W = RW_WIDTH
    z = feats.astype(f32)
    z = z + mu * (token_shift(z) - z)
    r, k, v = z[..., :W], z[..., W:2 * W], z[..., 2 * W:3 * W]
    o1 = 3 * W
    o2 = o1 + RW_DECAY_LORA
    o3 = o2 + RW_AAA_LORA
    wd, ad, gd = z[..., o1:o2], z[..., o2:o3], z[..., o3:o3 + RW_GATE_LORA]
    w_log = -jax.nn.softplus(-(w0 + jnp.tanh(wd) @ w2)) - 0.5
    decay = jnp.exp(-jnp.exp(w_log))
    a = jax.nn.sigmoid(a0 + ad @ a2)
    g = jax.nn.sigmoid(gd) @ g2
    if vres is None:
        v_first = v
    else:
        v0, v1, v2 = vres
        v = v + (v_first - v) * jax.nn.sigmoid(v0 + (v @ v1) @ v2)

    def heads(t):
        return t.reshape(B, S, RW_HEADS, RW_HEAD)

    kk = heads(k * k_k)
    kk = kk * lax.rsqrt(jnp.maximum(jnp.sum(kk * kk, axis=-1, keepdims=True), 1e-24))
    k = k * (1.0 + (a - 1.0) * k_a)
    rh, wh, kh, vh, ah = heads(r), heads(decay), heads(k), heads(v), heads(a)
    a_vec = -kk
    b_vec = kk * ah

    def step(state, inp):
        r_t, w_t, k_t, v_t, a_t, b_t = inp
        sa = jnp.einsum('bhvk,bhk->bhv', state, a_t)
        state = state * w_t[:, :, None, :] + sa[..., None] * b_t[:, :, None, :] + v_t[..., None] * k_t[:, :, None, :]
        return state, jnp.einsum('bhvk,bhk->bhv', state, r_t)

    xs = tuple(jnp.moveaxis(t, 1, 0) for t in (rh, wh, kh, vh, a_vec, b_vec))
    s0 = jnp.zeros((B, RW_HEADS, RW_HEAD, RW_HEAD), f32)
    _, y = lax.scan(step, s0, xs)
    y = jnp.moveaxis(y, 0, 1)
    mean = jnp.mean(y, axis=-1, keepdims=True)
    var = jnp.mean(jnp.square(y - mean), axis=-1, keepdims=True)
    y = ((y - mean) * lax.rsqrt(var + RW_GN_EPS)).reshape(B, S, W) * ln_w + ln_b
    bonus = jnp.sum(rh * kh * r_k, axis=-1, keepdims=True) * vh
    out = (y + bonus.reshape(B, S, W)) * g
    return out, v_first


def s5_mixer(u, lam_re, lam_im, log_step, b_re, b_im, c_re, c_im, d, glu_w, glu_b):
    B, S, _ = u.shape
    f32 = jnp.float32
    u = u.astype(f32).reshape(B, S, S5_GROUPS, S5_GROUP)
    lam_re = lam_re.astype(f32)
    lam_im = lam_im.astype(f32)
    dt = jnp.exp(log_step.astype(f32))[:, None]
    mag = jnp.exp(lam_re * dt)
    ab_re = mag * jnp.cos(lam_im * dt)
    ab_im = mag * jnp.sin(lam_im * dt)
    den = lam_re * lam_re + lam_im * lam_im
    nr, ni = ab_re - 1.0, ab_im
    coef_re = (nr * lam_re + ni * lam_im) / den
    coef_im = (ni * lam_re - nr * lam_im) / den
    bb_re = coef_re[..., None] * b_re - coef_im[..., None] * b_im
    bb_im = coef_re[..., None] * b_im + coef_im[..., None] * b_re
    bu_re = jnp.einsum('bsgh,gph->bsgp', u, bb_re)
    bu_im = jnp.einsum('bsgh,gph->bsgp', u, bb_im)
    a_re = jnp.broadcast_to(ab_re, (1, S, S5_GROUPS, S5_STATE))
    a_im = jnp.broadcast_to(ab_im, (1, S, S5_GROUPS, S5_STATE))

    def combine(left, right):
        ar_l, ai_l, br_l, bi_l = left
        ar_r, ai_r, br_r, bi_r = right
        return (ar_r * ar_l - ai_r * ai_l,
                ar_r * ai_l + ai_r * ar_l,
                ar_r * br_l - ai_r * bi_l + br_r,
                ar_r * bi_l + ai_r * br_l + bi_r)

    _, _, xs_re, xs_im = lax.associative_scan(combine, (a_re, a_im, bu_re, bu_im), axis=1)
    y = (jnp.einsum('gjp,bsgp->bsgj', c_re, xs_re) - jnp.einsum('gjp,bsgp->bsgj', c_im, xs_im)
         + d.reshape(S5_GROUPS, S5_GROUP) * u)
    zg = jax.nn.gelu(y)
    out = zg * jax.nn.sigmoid(jnp.einsum('bsgj,gjk->bsgk', zg, glu_w) + glu_b.reshape(S5_GROUPS, S5_GROUP))
    return out.reshape(B, S, S5_WIDTH)


def setup_inputs(seed: int = 0) -> dict:
    key = jax.random.key(seed)
    ks = iter(jax.random.split(key, 48))
    f32 = jnp.float32
    L = DEPTH

    def nrm(shape, scale):
        return scale * jax.random.normal(next(ks), shape, f32)

    return {
        "x": nrm((BATCH, SEQ, D_MODEL), 1.0),
        "mix_norm": 1.0 + nrm((L, D_MODEL), 0.02),
        "w_in": nrm((L, D_MODEL, IN_WIDTH), D_MODEL ** -0.5),
        "hg_lb_logits": nrm((L, HG_WIDTH), 0.1),
        "hg_onorm": 1.0 + nrm((L, HG_WIDTH), 0.02),
        "rw_shift_mu": jax.random.uniform(next(ks), (L, RW_IN), f32),
        "rw_w0": jnp.linspace(-6.0, -1.0, RW_WIDTH, dtype=f32)[None, :] + nrm((L, RW_WIDTH), 0.1),
        "rw_w2": nrm((L, RW_DECAY_LORA, RW_WIDTH), 0.5 * RW_DECAY_LORA ** -0.5),
        "rw_a0": nrm((L, RW_WIDTH), 0.1),
        "rw_a2": nrm((L, RW_AAA_LORA, RW_WIDTH), 0.5 * RW_AAA_LORA ** -0.5),
        "rw_g2": nrm((L, RW_GATE_LORA, RW_WIDTH), RW_GATE_LORA ** -0.5),
        "rw_v0": 0.5 + nrm((L - 1, RW_WIDTH), 0.1),
        "rw_v1": nrm((L - 1, RW_WIDTH, RW_MV_LORA), RW_WIDTH ** -0.5),
        "rw_v2": nrm((L - 1, RW_MV_LORA, RW_WIDTH), 0.5 * RW_MV_LORA ** -0.5),
        "rw_k_k": 0.85 + nrm((L, RW_WIDTH), 0.02),
        "rw_k_a": 1.0 + nrm((L, RW_WIDTH), 0.02),
        "rw_r_k": nrm((L, RW_HEADS, RW_HEAD), 0.1),
        "rw_ln_w": 1.0 + nrm((L, RW_WIDTH), 0.02),
        "rw_ln_b": nrm((L, RW_WIDTH), 0.02),
        "s5_lambda_re": -0.5 + nrm((L, S5_GROUPS, S5_STATE), 0.01),
        "s5_lambda_im": jnp.pi * jnp.arange(S5_STATE, dtype=f32)[None, None, :] + nrm((L, S5_GROUPS, S5_STATE), 0.01),
        "s5_log_step": jax.random.uniform(next(ks), (L, S5_GROUPS), f32, minval=math.log(S5_DT_MIN), maxval=math.log(S5_DT_MAX)),
        "s5_b_re": nrm((L, S5_GROUPS, S5_STATE, S5_GROUP), (2 * S5_GROUP) ** -0.5),
        "s5_b_im": nrm((L, S5_GROUPS, S5_STATE, S5_GROUP), (2 * S5_GROUP) ** -0.5),
        "s5_c_re": nrm((L, S5_GROUPS, S5_GROUP, S5_STATE), (2 * S5_STATE) ** -0.5),
        "s5_c_im": nrm((L, S5_GROUPS, S5_GROUP, S5_STATE), (2 * S5_STATE) ** -0.5),
        "s5_d": nrm((L, S5_WIDTH), 1.0),
        "s5_glu_w": nrm((L, S5_GROUPS, S5_GROUP, S5_GROUP), S5_GROUP ** -0.5),
        "s5_glu_b": nrm((L, S5_WIDTH), 0.02),
        "w_branch": nrm((L, MIX_WIDTH, D_MODEL), D_MODEL ** -0.5),
        "w_out": nrm((L, D_MODEL, D_MODEL), D_MODEL ** -0.5),
        "ffn_norm": 1.0 + nrm((L, D_MODEL), 0.02),
        "ffn_w_gate": nrm((L, D_MODEL, FFN_HIDDEN), D_MODEL ** -0.5),
        "ffn_w_up": nrm((L, D_MODEL, FFN_HIDDEN), D_MODEL ** -0.5),
        "ffn_w_down": nrm((L, FFN_HIDDEN, D_MODEL), FFN_HIDDEN ** -0.5),
        "final_norm": 1.0 + nrm((D_MODEL,), 0.02),
    }


def reference(x, mix_norm, w_in, hg_lb_logits, hg_onorm, rw_shift_mu, rw_w0, rw_w2, rw_a0, rw_a2,
              rw_g2, rw_v0, rw_v1, rw_v2, rw_k_k, rw_k_a, rw_r_k, rw_ln_w, rw_ln_b,
              s5_lambda_re, s5_lambda_im, s5_log_step, s5_b_re, s5_b_im, s5_c_re, s5_c_im, s5_d,
              s5_glu_w, s5_glu_b, w_branch, w_out, ffn_norm, ffn_w_gate, ffn_w_up, ffn_w_down,
              final_norm):
    B, S, D = x.shape
    lb_sm = jax.nn.softmax(hg_lb_logits.astype(jnp.float32), axis=0)
    lb_all = jnp.cumsum(lb_sm, axis=0) - lb_sm[0:1]
    h = x
    v_first = None
    for l in range(DEPTH):
        xn = rmsnorm(h, mix_norm[l])
        proj = xn @ w_in[l]
        hg = proj[..., OFF_HG:OFF_RW]
        o_hg = hgrn2_mixer(hg[..., :HG_WIDTH], hg[..., HG_WIDTH:2 * HG_WIDTH],
                           hg[..., 2 * HG_WIDTH:3 * HG_WIDTH], hg[..., 3 * HG_WIDTH:],
                           lb_all[l], hg_onorm[l])
        vres = None if l == 0 else (rw_v0[l - 1], rw_v1[l - 1], rw_v2[l - 1])
        o_rw, v_first = rwkv7_mixer(proj[..., OFF_RW:OFF_S5], v_first, vres, rw_shift_mu[l],
                                    rw_w0[l], rw_w2[l], rw_a0[l], rw_a2[l], rw_g2[l],
                                    rw_k_k[l], rw_k_a[l], rw_r_k[l], rw_ln_w[l], rw_ln_b[l])
        o_s5 = s5_mixer(proj[..., OFF_S5:OFF_GATE], s5_lambda_re[l], s5_lambda_im[l], s5_log_step[l],
                        s5_b_re[l], s5_b_im[l], s5_c_re[l], s5_c_im[l], s5_d[l], s5_glu_w[l], s5_glu_b[l])
        gates = jax.nn.sigmoid(proj[..., OFF_GATE:].astype(jnp.float32)).reshape(B, S, N_BRANCH, D)
        wb = w_branch[l]
        merged = (gates[:, :, 0] * (o_hg @ wb[:HG_WIDTH])
                  + gates[:, :, 1] * (o_rw @ wb[HG_WIDTH:HG_WIDTH + RW_WIDTH])
                  + gates[:, :, 2] * (o_s5 @ wb[HG_WIDTH + RW_WIDTH:]))
        h = h + (merged.astype(h.dtype) @ w_out[l]).astype(h.dtype)
        hn = rmsnorm(h, ffn_norm[l])
        h = h + ((jax.nn.silu(hn @ ffn_w_gate[l]) * (hn @ ffn_w_up[l])) @ ffn_w_down[l]).astype(h.dtype)
    return rmsnorm(h, final_norm)
```

```python
import functools
import math

import numpy as np
import jax
import jax.numpy as jnp
from jax import lax
from jax.experimental import pallas as pl
from jax.experimental.pallas import tpu as pltpu

F32 = jnp.float32
BF16 = jnp.bfloat16

D = 1024
DEPTH = 4
NORM_EPS = 1e-6
CHUNK = 64
LANE = 128
HG_HEADS = 8
RW_HEAD = 64
RW_GN_EPS = 64e-5
LORA_W, LORA_A, LORA_G, LORA_V = 64, 64, 160, 32
S5_GROUPS, S5_GROUP, S5_STATE = 64, 16, 64
FFN_HIDDEN = 2816
FFN_CHUNK = 256

COL_HG = 0
COL_RW = 4096
COL_S5 = 7168
COL_GATE = 8192
COL_LORA = 11264
IN_PACKED = 11776
VMEM_LIMIT = 56 * 1024 * 1024


def _cparams(*sem):
    return pltpu.CompilerParams(dimension_semantics=sem, vmem_limit_bytes=VMEM_LIMIT)


def _split3(x):
    hi = x.astype(BF16)
    r1 = x - hi.astype(F32)
    mid = r1.astype(BF16)
    lo = (r1 - mid.astype(F32)).astype(BF16)
    return hi, mid, lo


def _dot(a, b):
    return jnp.dot(a, b, preferred_element_type=F32)


def _dot_nt(a, b):
    return lax.dot_general(a, b, (((1,), (1,)), ((), ())), preferred_element_type=F32)


def _exact_dot01(m3, x):
    hi, mid, lo = _split3(x)
    return _dot(m3, jnp.concatenate([hi, mid, lo], axis=0))


def _rms(x, g):
    ms = jnp.mean(x * x, axis=-1, keepdims=True)
    return x * lax.rsqrt(ms + NORM_EPS) * g


def _sigmoid(x):
    return 1.0 / (1.0 + jnp.exp(-x))


def _silu(x):
    return x * _sigmoid(x)


def _inproj_body(x_ref, g_ref, w_ref, o_ref, xn_ref):
    @pl.when(pl.program_id(1) == 0)
    def _():
        xn_ref[...] = _rms(x_ref[...], g_ref[...]).astype(BF16)

    o_ref[...] = _dot(xn_ref[...], w_ref[...])


def _inproj(h, g, w, tm, tn):
    t = h.shape[0]
    n = w.shape[1]
    return pl.pallas_call(
        _inproj_body,
        grid=(t // tm, n // tn),
        in_specs=[
            pl.BlockSpec((tm, D), lambda i, j: (i, 0)),
            pl.BlockSpec((1, D), lambda i, j: (0, 0)),
            pl.BlockSpec((D, tn), lambda i, j: (0, j)),
        ],
        out_specs=pl.BlockSpec((tm, tn), lambda i, j: (i, j)),
        out_shape=jax.ShapeDtypeStruct((t, n), F32),
        scratch_shapes=[pltpu.VMEM((tm, D), BF16)],
        compiler_params=_cparams("parallel", "arbitrary"),
        name="inproj",
    )(h, g, w)


def _merge_body(h_ref, ohg_ref, orw_ref, os5_ref, g0_ref, g1_ref, g2_ref, wb_ref, wo_ref, o_ref):
    m = _sigmoid(g0_ref[...]) * _dot(ohg_ref[...], wb_ref[0])
    m = m + _sigmoid(g1_ref[...]) * _dot(orw_ref[...], wb_ref[1])
    m = m + _sigmoid(g2_ref[...]) * _dot(os5_ref[...], wb_ref[2])
    o_ref[...] = h_ref[...] + _dot(m.astype(BF16), wo_ref[...])


def _merge(h, o_hg, o_rw, o_s5, proj, wb, wo, tm):
    t = h.shape[0]
    row = lambda i: (i, 0)
    return pl.pallas_call(
        _merge_body,
        grid=(t // tm,),
        in_specs=[
            pl.BlockSpec((tm, D), row),
            pl.BlockSpec((tm, D), row),
            pl.BlockSpec((tm, D), row),
            pl.BlockSpec((tm, D), row),
            pl.BlockSpec((tm, D), lambda i: (i, COL_GATE // D)),
            pl.BlockSpec((tm, D), lambda i: (i, COL_GATE // D + 1)),
            pl.BlockSpec((tm, D), lambda i: (i, COL_GATE // D + 2)),
            pl.BlockSpec((3, D, D), lambda i: (0, 0, 0)),
            pl.BlockSpec((D, D), lambda i: (0, 0)),
        ],
        out_specs=pl.BlockSpec((tm, D), row),
        out_shape=jax.ShapeDtypeStruct((t, D), F32),
        compiler_params=_cparams("parallel"),
        name="merge",
    )(h, o_hg, o_rw, o_s5, proj, proj, proj, wb, wo)


def _ffn_body(final, h_ref, g_ref, wg_ref, wu_ref, wd_ref, fg_ref, o_ref, xn_ref, acc_ref):
    x = h_ref[...]
    xn_ref[...] = _rms(x, g_ref[...]).astype(BF16)
    for c in range(FFN_HIDDEN // FFN_CHUNK):
        sl = slice(c * FFN_CHUNK, (c + 1) * FFN_CHUNK)
        a = _dot(xn_ref[...], wg_ref[:, sl])
        b = _dot(xn_ref[...], wu_ref[:, sl])
        part = _dot((_silu(a) * b).astype(BF16), wd_ref[sl, :])
        if c == 0:
            acc_ref[...] = part
        else:
            acc_ref[...] += part
    out = x + acc_ref[...]
    if final:
        out = _rms(out, fg_ref[...])
    o_ref[...] = out


def _ffn(h, g, wg, wu, wd, fg, final, tm):
    t = h.shape[0]
    row = lambda i: (i, 0)
    fixed = lambda i: (0, 0)
    return pl.pallas_call(
        functools.partial(_ffn_body, final),
        grid=(t // tm,),
        in_specs=[
            pl.BlockSpec((tm, D), row),
            pl.BlockSpec((1, D), fixed),
            pl.BlockSpec((D, FFN_HIDDEN), fixed),
            pl.BlockSpec((D, FFN_HIDDEN), fixed),
            pl.BlockSpec((FFN_HIDDEN, D), fixed),
            pl.BlockSpec((1, D), fixed),
        ],
        out_specs=pl.BlockSpec((tm, D), row),
        out_shape=jax.ShapeDtypeStruct((t, D), F32),
        scratch_shapes=[pltpu.VMEM((tm, D), BF16), pltpu.VMEM((tm, D), F32)],
        compiler_params=_cparams("parallel"),
        name="ffn",
    )(h, g, wg, wu, wd, fg)


HG_LEVELS = (32, 16, 8, 4, 2, 1)


def _hg_dmat():
    t = np.arange(CHUNK)
    tri = (t[None, :] <= t[:, None]).astype(np.float32)
    blocks = [tri]
    for m in HG_LEVELS:
        r = (t // (2 * m)) * (2 * m) + m - 1
        blocks.append(tri - tri[r])
    d = np.concatenate(blocks, axis=0)
    return jnp.asarray(np.concatenate([d, d, d], axis=1), BF16)


def _hg_body(layer, q_ref, f_ref, i_ref, og_ref, lbl_ref, on_ref, dm_ref, o_ref, st_ref, d_ref):
    tb = q_ref.shape[0]

    @pl.when(pl.program_id(1) == 0)
    def _():
        st_ref[...] = jnp.zeros_like(st_ref)

    lg = lbl_ref[...]
    e = jnp.exp(lg - jnp.max(lg, axis=0, keepdims=True))
    sm = e / jnp.sum(e, axis=0, keepdims=True)
    lb = jnp.zeros((1, D), F32)
    for l in range(1, layer + 1):
        lb = lb + sm[l:l + 1, :]
    lb = jnp.maximum(lb, 0.0)
    log_lb = jnp.log(lb)
    log_1mlb = jnp.log1p(-lb)
    one_mlb = 1.0 - lb

    ti = lax.broadcasted_iota(jnp.int32, (CHUNK, CHUNK), 0)
    si = lax.broadcasted_iota(jnp.int32, (CHUNK, CHUNK), 1)
    xr = ti ^ si
    lower = ti > si
    pair_masks = [lower & (xr >= m) & (xr < 2 * m) for m in HG_LEVELS]
    diag_mask = ti == si
    trow = lax.broadcasted_iota(jnp.int32, (CHUNK, LANE), 0)
    second = [(trow & m) != 0 for m in HG_LEVELS]

    def chunk_step(c, carry):
        r0 = pl.multiple_of(c * CHUNK, CHUNK)
        rows = pl.ds(r0, CHUNK)
        z = f_ref[rows, :]
        ez = jnp.exp(-jnp.abs(z))
        l1p = jnp.log1p(ez)
        lsig = jnp.minimum(z, 0.0) - l1p
        cterm = log_1mlb + lsig
        mx = jnp.maximum(log_lb, cterm)
        logf = mx + jnp.log1p(jnp.exp(-jnp.abs(log_lb - cterm)))
        d_ref[...] = _exact_dot01(dm_ref[...], logf)
        kk_all = one_mlb * jnp.where(z >= 0.0, ez, 1.0) / (1.0 + ez)
        for h in range(HG_HEADS):
            ln = slice(h * LANE, (h + 1) * LANE)
            q = _silu(q_ref[rows, ln])
            k = kk_all[:, ln]
            v32 = i_ref[rows, ln]
            v = v32.astype(BF16)
            b = d_ref[0:CHUNK, ln]
            s = _dot_nt(q.astype(BF16), k.astype(BF16))
            scores = jnp.where(diag_mask, s, 0.0)
            for li in range(len(HG_LEVELS)):
                dl = d_ref[(li + 1) * CHUNK:(li + 2) * CHUNK, ln]
                x = (jnp.where(second[li], q, k) * jnp.exp(-jnp.abs(dl))).astype(BF16)
                scores = scores + jnp.where(pair_masks[li], _dot_nt(x, x), 0.0)
            st = st_ref[h]
            qe = (q * jnp.exp(b)).astype(BF16)
            o = _dot(scores.astype(BF16), v) + _dot_nt(qe, st.astype(BF16))
            b_last = b[CHUNK - 1:CHUNK, :]
            kd = (k * jnp.exp(b_last - b)).astype(BF16)
            st_ref[h] = st * jnp.exp(b_last) + _dot(v32.T.astype(BF16), kd)
            o = o * lax.rsqrt(jnp.mean(o * o, axis=-1, keepdims=True) + NORM_EPS)
            o = o * on_ref[:, ln] * _silu(og_ref[rows, ln])
            o_ref[rows, ln] = o.astype(o_ref.dtype)
        return carry

    lax.fori_loop(0, tb // CHUNK, chunk_step, 0)


def _hgrn2(proj, lb_logits, onorm, layer, batch, seq, tb):
    t = proj.shape[0]
    nb = seq // tb
    col = lambda j: (lambda b, s: (b * nb + s, COL_HG // D + j))
    fixed = lambda b, s: (0, 0)
    return pl.pallas_call(
        functools.partial(_hg_body, layer),
        grid=(batch, nb),
        in_specs=[
            pl.BlockSpec((tb, D), col(0)),
            pl.BlockSpec((tb, D), col(1)),
            pl.BlockSpec((tb, D), col(2)),
            pl.BlockSpec((tb, D), col(3)),
            pl.BlockSpec(lb_logits.shape, fixed),
            pl.BlockSpec((1, D), fixed),
            pl.BlockSpec((7 * CHUNK, 3 * CHUNK), fixed),
        ],
        out_specs=pl.BlockSpec((tb, D), lambda b, s: (b * nb + s, 0)),
        out_shape=jax.ShapeDtypeStruct((t, D), BF16),
        scratch_shapes=[pltpu.VMEM((HG_HEADS, LANE, LANE), F32),
                        pltpu.VMEM((7 * CHUNK, D), F32)],
        compiler_params=_cparams("parallel", "arbitrary"),
        name="hgrn2",
    )(proj, proj, proj, proj, lb_logits, onorm, _hg_dmat())


S5_ROW = CHUNK * S5_GROUP


def _s5_consts():
    tj = np.arange(S5_ROW)
    e_tj = (tj[None, :] // S5_GROUP == np.arange(CHUNK)[:, None]).astype(np.float32)
    f_tj = (tj[None, :] % S5_GROUP == np.arange(S5_GROUP)[:, None]).astype(np.float32)
    return (jnp.asarray(e_tj, BF16), jnp.asarray(f_tj, BF16),
            jnp.asarray(e_tj.T, BF16), jnp.asarray(f_tj.T, BF16))


def _right01(x, e):
    hi, mid, lo = _split3(x)
    return _dot(hi, e) + _dot(mid, e) + _dot(lo, e)


def _left01(e, x):
    hi, mid, lo = _split3(x)
    return _dot(e, hi) + _dot(e, mid) + _dot(e, lo)


def _dot_f32(a, b):
    ah, am, al = _split3(a)
    bh, bm, bl = _split3(b)
    return (_dot(ah, bh) + _dot(ah, bm) + _dot(am, bh)
            + _dot(ah, bl) + _dot(al, bh) + _dot(am, bm))


def _s5_prep_body(lrr_ref, lir_ref, lrc_ref, lic_ref, ls_ref, brt_ref, bit_ref, crt_ref, cit_ref,
                  etj_ref, ftj_ref, esh_ref, fsh_ref,
                  wt_ref, wi_ref, wis_ref, wo_ref, a1_ref, a2_ref):
    dt = jnp.exp(ls_ref[...])
    lam_re, lam_im = lrr_ref[...], lir_ref[...]
    lr_row, li_row = lam_re * dt, lam_im * dt
    lr_col, li_col = lrc_ref[...] * dt, lic_ref[...] * dt
    re_half = lax.broadcasted_iota(jnp.int32, (1, 2 * S5_STATE), 1) < S5_STATE

    mag = jnp.exp(lr_row)
    ab_re, ab_im = mag * jnp.cos(li_row), mag * jnp.sin(li_row)
    den = lam_re * lam_re + lam_im * lam_im
    nr, ni = ab_re - 1.0, ab_im
    coef_re = (nr * lam_re + ni * lam_im) / den
    coef_im = (ni * lam_re - nr * lam_im) / den
    bb_re = coef_re * brt_ref[...] - coef_im * bit_ref[...]
    bb_im = coef_re * bit_ref[...] + coef_im * brt_ref[...]

    def powers(lr, li, n):
        m = jnp.exp(lr * n)
        return m * jnp.cos(li * n), m * jnp.sin(li * n)

    tau = lax.broadcasted_iota(jnp.int32, (1, CHUNK), 1).astype(F32)
    p0_re, p0_im = powers(lr_col, li_col, tau)
    p1_re, p1_im = powers(lr_col, li_col, tau + 1.0)
    c_re = _right01(crt_ref[...], ftj_ref[...])
    c_im = _right01(cit_ref[...], ftj_ref[...])
    p0e_re, p0e_im = _right01(p0_re, etj_ref[...]), _right01(p0_im, etj_ref[...])
    p1e_re, p1e_im = _right01(p1_re, etj_ref[...]), _right01(p1_im, etj_ref[...])

    m_stack = jnp.concatenate([c_re * p0e_re - c_im * p0e_im, c_re * p0e_im + c_im * p0e_re], axis=0)
    r0 = _dot_f32(jnp.where(re_half, bb_re, -bb_im), m_stack)
    lane = lax.broadcasted_iota(jnp.int32, (S5_GROUP, S5_ROW), 1)
    for s in range(CHUNK):
        blk = r0 if s == 0 else jnp.where(lane >= S5_GROUP * s, pltpu.roll(r0, S5_GROUP * s, 1), 0.0)
        wt_ref[s * S5_GROUP:(s + 1) * S5_GROUP, :] = blk.astype(BF16)

    wo_ref[0:S5_STATE, :] = (c_re * p1e_re - c_im * p1e_im).astype(BF16)
    wo_ref[S5_STATE:2 * S5_STATE, :] = (-(c_re * p1e_im + c_im * p1e_re)).astype(BF16)

    back = (CHUNK - 1.0) - lax.broadcasted_iota(jnp.int32, (CHUNK, 1), 0).astype(F32)
    pt_re, pt_im = powers(lr_row, li_row, back)
    pte_re, pte_im = _left01(esh_ref[...], pt_re), _left01(esh_ref[...], pt_im)
    bbe_re, bbe_im = _left01(fsh_ref[...], bb_re), _left01(fsh_ref[...], bb_im)
    w_re = pte_re * bbe_re - pte_im * bbe_im
    w_im = pte_re * bbe_im + pte_im * bbe_re
    wi_ref[...] = jnp.where(re_half, w_re, w_im).astype(BF16)
    wis_ref[...] = jnp.where(re_half, w_im, w_re).astype(BF16)

    k = lax.broadcasted_iota(jnp.int32, (8, 1), 0)
    n = (CHUNK * (1 << k)).astype(F32)
    an_re, an_im = powers(lr_row, li_row, n)
    a1_ref[...] = an_re
    a2_ref[...] = jnp.where(re_half, -an_im, an_im)


def _s5_prep(lam_re, lam_im, log_step, b_re, b_im, c_re, c_im):
    nl = lam_re.shape[0]
    ns = 2 * S5_STATE
    g4 = lambda r, c: pl.BlockSpec((None, None, r, c), lambda l, g: (l, g, 0, 0))
    cst = lambda r, c: pl.BlockSpec((r, c), lambda l, g: (0, 0))
    sds = lambda r, c, dt: jax.ShapeDtypeStruct((nl, S5_GROUPS, r, c), dt)
    twice = lambda a: jnp.concatenate([a, a], axis=-1)
    e_tj, f_tj, e_sh, f_sh = _s5_consts()
    return pl.pallas_call(
        _s5_prep_body,
        grid=(nl, S5_GROUPS),
        in_specs=[g4(1, ns), g4(1, ns), g4(S5_STATE, 1), g4(S5_STATE, 1), g4(1, 1),
                  g4(S5_GROUP, ns), g4(S5_GROUP, ns), g4(S5_STATE, S5_GROUP), g4(S5_STATE, S5_GROUP),
                  cst(CHUNK, S5_ROW), cst(S5_GROUP, S5_ROW), cst(S5_ROW, CHUNK), cst(S5_ROW, S5_GROUP)],
        out_specs=[g4(S5_ROW, S5_ROW), g4(S5_ROW, ns), g4(S5_ROW, ns), g4(ns, S5_ROW), g4(8, ns), g4(8, ns)],
        out_shape=[sds(S5_ROW, S5_ROW, BF16), sds(S5_ROW, ns, BF16), sds(S5_ROW, ns, BF16),
                   sds(ns, S5_ROW, BF16), sds(8, ns, F32), sds(8, ns, F32)],
        compiler_params=_cparams("parallel", "parallel"),
        name="s5_prep",
    )(twice(lam_re)[:, :, None, :], twice(lam_im)[:, :, None, :], lam_re[:, :, :, None], lam_im[:, :, :, None],
      log_step[:, :, None, None],
      twice(jnp.swapaxes(b_re, 2, 3)), twice(jnp.swapaxes(b_im, 2, 3)),
      jnp.swapaxes(c_re, 2, 3), jnp.swapaxes(c_im, 2, 3),
      e_tj, f_tj, e_sh, f_sh)


def _s5_body(batch, u_ref, wt_ref, wi_ref, wis_ref, wo_ref, a1_ref, a2_ref, y_ref):
    u = u_ref[...]
    m = u.shape[0]
    x = _dot(u, wi_ref[...])
    xs = _dot(u, wis_ref[...])
    row = lax.broadcasted_iota(jnp.int32, (m, 2 * S5_STATE), 0)

    def shifted(z, n):
        return jnp.where(row >= n, pltpu.roll(z, n, 0), 0.0)

    k = 0
    while batch * (1 << k) < m:
        n = batch * (1 << k)
        a1, a2 = a1_ref[k:k + 1, :], a2_ref[k:k + 1, :]
        sx, sxs = shifted(x, n), shifted(xs, n)
        x, xs = x + a1 * sx + a2 * sxs, xs + a1 * sxs - a2 * sx
        k += 1
    y = _dot(u, wt_ref[...]) + _dot(shifted(x, batch).astype(BF16), wo_ref[...])
    y_ref[...] = y


def _s5_core(u_t, ops, layer, batch):
    wt, wi, wis, wo, a1, a2 = ops
    m = u_t.shape[1]
    ns = 2 * S5_STATE
    g3 = lambda r, c: pl.BlockSpec((None, r, c), lambda g: (g, 0, 0))
    lg = lambda r, c: pl.BlockSpec((None, None, r, c), lambda g: (layer, g, 0, 0))
    return pl.pallas_call(
        functools.partial(_s5_body, batch),
        grid=(S5_GROUPS,),
        in_specs=[g3(m, S5_ROW), lg(S5_ROW, S5_ROW), lg(S5_ROW, ns), lg(S5_ROW, ns),
                  lg(ns, S5_ROW), lg(8, ns), lg(8, ns)],
        out_specs=g3(m, S5_ROW),
        out_shape=jax.ShapeDtypeStruct((S5_GROUPS, m, S5_ROW), F32),
        compiler_params=_cparams("parallel"),
        name="s5_core",
    )(u_t, wt, wi, wis, wo, a1, a2)


def _s5_post_body(y_ref, u_ref, d_ref, gw_ref, gb_ref, o_ref):
    for i in range(D // LANE):
        ln = slice(i * LANE, (i + 1) * LANE)
        y = y_ref[:, ln] + d_ref[:, ln] * u_ref[:, ln]
        zg = 0.5 * y * (1.0 + jnp.tanh(math.sqrt(2.0 / math.pi) * (y + 0.044715 * (y * y * y))))
        gate = _dot(zg.astype(BF16), gw_ref[i]) + gb_ref[:, ln]
        o_ref[:, ln] = (zg * _sigmoid(gate)).astype(o_ref.dtype)


def _s5_post(ycore, proj, d, glu_tiles, glu_b, tm):
    t = ycore.shape[0]
    row = lambda i: (i, 0)
    fixed = lambda i: (0, 0)
    return pl.pallas_call(
        _s5_post_body,
        grid=(t // tm,),
        in_specs=[pl.BlockSpec((tm, D), row),
                  pl.BlockSpec((tm, D), lambda i: (i, COL_S5 // D)),
                  pl.BlockSpec((1, D), fixed),
                  pl.BlockSpec((D // LANE, LANE, LANE), lambda i: (0, 0, 0)),
                  pl.BlockSpec((1, D), fixed)],
        out_specs=pl.BlockSpec((tm, D), row),
        out_shape=jax.ShapeDtypeStruct((t, D), BF16),
        compiler_params=_cparams("parallel"),
        name="s5_post",
    )(ycore, proj, d, glu_tiles, glu_b)


def _ones_bd():
    i = np.arange(LANE)
    return jnp.asarray((i[:, None] // RW_HEAD == i[None, :] // RW_HEAD).astype(np.float32), BF16)


def _segsum(x, ones_bd):
    hi = x.astype(BF16)
    lo = (x - hi.astype(F32)).astype(BF16)
    return _dot(hi, ones_bd) + _dot(lo, ones_bd)


def _rw_prep_body(first, r_ref, k_ref, v_ref, lo_ref, vf_ref, mu_ref, mul_ref, w0_ref, w2_ref, a0_ref, a2_ref,
                  g2_ref, v0_ref, v1_ref, v2_ref, kk_ref, ka_ref, ones_ref,
                  ro_ref, lw_ref, ko_ref, vo_ref, kko_ref, bo_ref, go_ref, cr_ref, cl_ref):
    tb = r_ref.shape[0]

    @pl.when(pl.program_id(1) == 0)
    def _():
        cr_ref[...] = jnp.zeros_like(cr_ref)
        cl_ref[...] = jnp.zeros_like(cl_ref)

    def mix(z, carry, mu):
        row = lax.broadcasted_iota(jnp.int32, z.shape, 0)
        zs = jnp.where(row == 0, carry, pltpu.roll(z, 1, 0))
        return z + mu * (zs - z)

    def softplus(x):
        return jnp.maximum(x, 0.0) + jnp.log1p(jnp.exp(-jnp.abs(x)))

    zl = lo_ref[...]
    lora = mix(zl, cl_ref[0:1, :], mul_ref[...])
    cl_ref[0:1, :] = zl[tb - 1:tb, :]
    wd = jnp.tanh(lora[:, 0:LANE]).astype(BF16)
    ad = lora[:, LANE:2 * LANE].astype(BF16)
    gd = _sigmoid(lora[:, 2 * LANE:4 * LANE]).astype(BF16)
    w_log = -softplus(-(w0_ref[...] + _dot(wd, w2_ref[...]))) - 0.5
    lw_ref[...] = -jnp.exp(w_log)
    a = _sigmoid(a0_ref[...] + _dot(ad, a2_ref[...]))
    go_ref[...] = _dot(gd, g2_ref[...])

    zr = r_ref[...]
    ro_ref[...] = mix(zr, cr_ref[0:1, :], mu_ref[0:1, :])
    cr_ref[0:1, :] = zr[tb - 1:tb, :]
    zk = k_ref[...]
    k = mix(zk, cr_ref[1:2, :], mu_ref[1:2, :])
    cr_ref[1:2, :] = zk[tb - 1:tb, :]
    zv = v_ref[...]
    v = mix(zv, cr_ref[2:3, :], mu_ref[2:3, :])
    cr_ref[2:3, :] = zv[tb - 1:tb, :]
    if not first:
        lowr = _dot(v.astype(BF16), v1_ref[...]).astype(BF16)
        v = v + (vf_ref[...] - v) * _sigmoid(v0_ref[...] + _dot(lowr, v2_ref[...]))
    vo_ref[...] = v

    kk = k * kk_ref[...]
    ss = jnp.concatenate([_segsum(kk[:, i * LANE:(i + 1) * LANE] * kk[:, i * LANE:(i + 1) * LANE], ones_ref[...])
                          for i in range(D // LANE)], axis=1)
    kk = kk * lax.rsqrt(jnp.maximum(ss, 1e-24))
    kko_ref[...] = kk
    bo_ref[...] = kk * a
    ko_ref[...] = k * (1.0 + (a - 1.0) * ka_ref[...])


def _rw_prep(proj, v_first, p, first, batch, seq, tb):
    t = proj.shape[0]
    nb = seq // tb
    col = lambda j: (lambda b, s: (b * nb + s, COL_RW // D + j))
    fixed = lambda b, s: (0, 0)
    vec = pl.BlockSpec((1, D), fixed)
    rowblk = pl.BlockSpec((tb, D), lambda b, s: (b * nb + s, 0))
    out = jax.ShapeDtypeStruct((t, D), F32)
    return pl.pallas_call(
        functools.partial(_rw_prep_body, first),
        grid=(batch, nb),
        in_specs=[pl.BlockSpec((tb, D), col(0)), pl.BlockSpec((tb, D), col(1)), pl.BlockSpec((tb, D), col(2)),
                  pl.BlockSpec((tb, 4 * LANE), lambda b, s: (b * nb + s, COL_LORA // (4 * LANE))),
                  rowblk,
                  pl.BlockSpec((3, D), fixed), pl.BlockSpec((1, 4 * LANE), fixed),
                  vec, pl.BlockSpec((LANE, D), fixed), vec, pl.BlockSpec((LANE, D), fixed),
                  pl.BlockSpec((2 * LANE, D), fixed),
                  vec, pl.BlockSpec((D, LANE), fixed), pl.BlockSpec((LANE, D), fixed),
                  vec, vec, pl.BlockSpec((LANE, LANE), fixed)],
        out_specs=[rowblk] * 7,
        out_shape=[out] * 7,
        scratch_shapes=[pltpu.VMEM((8, D), F32), pltpu.VMEM((8, 4 * LANE), F32)],
        compiler_params=_cparams("parallel", "arbitrary"),
        name="rw_prep",
    )(proj, proj, proj, proj, v_first, p["mu"], p["mu_lora"], p["w0"], p["w2"], p["a0"], p["a2"], p["g2"],
      p["v0"], p["v1"], p["v2"], p["k_k"], p["k_a"], _ones_bd())


RW_LEVELS = (2, 4, 8, 16, 32)


def _rw_body(r_ref, lw_ref, k_ref, v_ref, kk_ref, b_ref, g_ref, rk_ref, lnw_ref, lnb_ref, tri_ref, ones_ref,
             o_ref, ht_ref, lg_ref):
    tb = r_ref.shape[0]

    @pl.when(pl.program_id(1) == 0)
    def _():
        ht_ref[...] = jnp.zeros_like(ht_ref)

    t64 = lax.broadcasted_iota(jnp.int32, (CHUNK, LANE), 0)
    l64 = lax.broadcasted_iota(jnp.int32, (CHUNK, LANE), 1)
    s64 = l64 & (RW_HEAD - 1)
    strict = t64 > s64
    incl = t64 >= s64
    eye = t64 == s64
    xr = t64 ^ s64
    lvl1 = strict & (xr == 1)
    lvl = [strict & (xr >= m) & (xr < 2 * m) for m in RW_LEVELS]
    head0 = l64 < RW_HEAD
    r128 = lax.broadcasted_iota(jnp.int32, (LANE, LANE), 0)
    l128 = lax.broadcasted_iota(jnp.int32, (LANE, LANE), 1)
    same_head = (r128 >> 6) == (l128 >> 6)

    def stack(x):
        return jnp.concatenate([jnp.where(head0, x, 0.0), jnp.where(head0, 0.0, x)], axis=0).astype(BF16)

    def chunk_step(c, carry):
        r0 = pl.multiple_of(c * CHUNK, CHUNK)
        rows = pl.ds(r0, CHUNK)
        lg_ref[...] = _exact_dot01(tri_ref[...], lw_ref[rows, :])
        for j in range(D // LANE):
            ln = slice(j * LANE, (j + 1) * LANE)
            lw = lw_ref[rows, ln]
            lg = lg_ref[:, ln]
            lg_end = lg[CHUNK - 1:CHUNK, :]
            r, k, v = r_ref[rows, ln], k_ref[rows, ln], v_ref[rows, ln]
            kk, bv = kk_ref[rows, ln], b_ref[rows, ln]
            e_neg = jnp.exp(-lg)
            e_end = jnp.exp(lg_end - lg)
            a_t = -kk * jnp.exp(lg - lw)
            r_t = r * jnp.exp(lg)
            lhs = jnp.concatenate([a_t, r_t], axis=0).astype(BF16)
            a_k = _dot_nt(lhs, stack(k * e_neg))
            a_b = _dot_nt(lhs, stack(bv * e_neg))
            a_ak = jnp.where(strict, a_k[0:CHUNK], 0.0)
            a_rk = jnp.where(incl, a_k[CHUNK:2 * CHUNK], 0.0)
            l_ab = jnp.where(strict, a_b[0:CHUNK], 0.0)
            a_rb = jnp.where(incl, a_b[CHUNK:2 * CHUNK], 0.0)

            x = jnp.where(eye, 1.0, 0.0) + jnp.where(lvl1, l_ab, 0.0)
            for li in range(len(RW_LEVELS)):
                z = _dot(x.astype(BF16), stack(jnp.where(lvl[li], l_ab, 0.0)))
                x = x + _dot(z.astype(BF16), stack(x))

            v_st = stack(v)
            xb = x.astype(BF16)
            u0 = _dot(xb, stack(_dot(a_ak.astype(BF16), v_st)))
            w = _dot(xb, stack(a_t))
            ht = ht_ref[j]
            htb = ht.astype(BF16)
            u = u0 + _dot_nt(w.astype(BF16), htb)
            u_st = stack(u)
            y = _dot_nt(r_t.astype(BF16), htb) + _dot(a_rk.astype(BF16), v_st) + _dot(a_rb.astype(BF16), u_st)
            vu = jnp.concatenate([v, u], axis=0)
            kb = jnp.concatenate([k * e_end, bv * e_end], axis=0).astype(BF16)
            upd = _dot(vu.T.astype(BF16), kb)
            ht_ref[j] = ht * jnp.exp(lg_end) + jnp.where(same_head, upd, 0.0)

            ones = ones_ref[...]
            mean = _segsum(y, ones) * (1.0 / RW_HEAD)
            yc = y - mean
            var = _segsum(yc * yc, ones) * (1.0 / RW_HEAD)
            yn = yc * lax.rsqrt(var + RW_GN_EPS) * lnw_ref[:, ln] + lnb_ref[:, ln]
            bonus = _segsum(r * k * rk_ref[:, ln], ones) * v
            o_ref[rows, ln] = ((yn + bonus) * g_ref[rows, ln]).astype(o_ref.dtype)
        return carry

    lax.fori_loop(0, tb // CHUNK, chunk_step, 0)


def _rw_core(feats, r_k, ln_w, ln_b, batch, seq, tb):
    t = feats[0].shape[0]
    nb = seq // tb
    fixed = lambda b, s: (0, 0)
    vec = pl.BlockSpec((1, D), fixed)
    rowblk = pl.BlockSpec((tb, D), lambda b, s: (b * nb + s, 0))
    tri = np.tril(np.ones((CHUNK, CHUNK), np.float32))
    tri3 = jnp.asarray(np.concatenate([tri, tri, tri], axis=1), BF16)
    return pl.pallas_call(
        _rw_body,
        grid=(batch, nb),
        in_specs=[rowblk] * 7 + [vec, vec, vec, pl.BlockSpec((CHUNK, 3 * CHUNK), fixed),
                                 pl.BlockSpec((LANE, LANE), fixed)],
        out_specs=rowblk,
        out_shape=jax.ShapeDtypeStruct((t, D), BF16),
        scratch_shapes=[pltpu.VMEM((D // LANE, LANE, LANE), F32), pltpu.VMEM((CHUNK, D), F32)],
        compiler_params=_cparams("parallel", "arbitrary"),
        name="rw_core",
    )(*feats, r_k, ln_w, ln_b, tri3, _ones_bd())


def _pack_w_in(w):
    o_rw, o_lora = 4096, 4096 + 3072
    o_s5 = o_lora + LORA_W + LORA_A + LORA_G
    o_gate = o_s5 + 1024
    z = lambda n: jnp.zeros((D, n), w.dtype)
    parts = [w[:, 0:o_lora], w[:, o_s5:o_gate], w[:, o_gate:o_gate + 3 * D],
             w[:, o_lora:o_lora + LORA_W], z(LANE - LORA_W),
             w[:, o_lora + LORA_W:o_lora + LORA_W + LORA_A], z(LANE - LORA_A),
             w[:, o_lora + LORA_W + LORA_A:o_s5], z(2 * LANE - LORA_G)]
    return jnp.concatenate(parts, axis=1).astype(BF16)


def _pad_rows(w, n):
    return jnp.concatenate([w, jnp.zeros((n - w.shape[0], w.shape[1]), w.dtype)], axis=0)


def _pad_lanes(v, n):
    return jnp.concatenate([v, jnp.zeros((n - v.shape[0],), v.dtype)])


def kernel(x, mix_norm, w_in, hg_lb_logits, hg_onorm, rw_shift_mu, rw_w0, rw_w2, rw_a0, rw_a2, rw_g2, rw_v0, rw_v1, rw_v2, rw_k_k, rw_k_a, rw_r_k, rw_ln_w, rw_ln_b, s5_lambda_re, s5_lambda_im, s5_log_step, s5_b_re, s5_b_im, s5_c_re, s5_c_im, s5_d, s5_glu_w, s5_glu_b, w_branch, w_out, ffn_norm, ffn_w_gate, ffn_w_up, ffn_w_down, final_norm):
    batch, seq, _ = x.shape
    t = batch * seq
    nc = seq // CHUNK
    depth = w_in.shape[0]
    tm = min(512, t)
    tb = min(256, seq)
    h = x.reshape(t, D)
    row = lambda v: v.reshape(1, -1)

    s5_ops = _s5_prep(s5_lambda_re, s5_lambda_im, s5_log_step, s5_b_re, s5_b_im, s5_c_re, s5_c_im)
    v_first = None
    for l in range(depth):
        proj = _inproj(h, row(mix_norm[l]), _pack_w_in(w_in[l]), min(1024, t), 512)

        o_hg = _hgrn2(proj, hg_lb_logits, row(hg_onorm[l]), l, batch, seq, tb)

        mu = rw_shift_mu[l]
        o1, o2, o3 = 3 * D + LORA_W, 3 * D + LORA_W + LORA_A, 3 * D + LORA_W + LORA_A + LORA_G
        first = l == 0
        rp = {
            "mu": mu[0:3 * D].reshape(3, D),
            "mu_lora": row(jnp.concatenate([_pad_lanes(mu[3 * D:o1], LANE), _pad_lanes(mu[o1:o2], LANE),
                                            _pad_lanes(mu[o2:o3], 2 * LANE)])),
            "w0": row(rw_w0[l]), "w2": _pad_rows(rw_w2[l], LANE).astype(BF16),
            "a0": row(rw_a0[l]), "a2": _pad_rows(rw_a2[l], LANE).astype(BF16),
            "g2": _pad_rows(rw_g2[l], 2 * LANE).astype(BF16),
            "v0": row(rw_v0[0 if first else l - 1]),
            "v1": jnp.concatenate([rw_v1[0 if first else l - 1],
                                   jnp.zeros((D, LANE - LORA_V), F32)], axis=1).astype(BF16),
            "v2": _pad_rows(rw_v2[0 if first else l - 1], LANE).astype(BF16),
            "k_k": row(rw_k_k[l]), "k_a": row(rw_k_a[l]),
        }
        feats = _rw_prep(proj, proj if first else v_first, rp, first, batch, seq, tb)
        if first:
            v_first = feats[3]
        o_rw = _rw_core(feats, row(rw_r_k[l]), row(rw_ln_w[l]), row(rw_ln_b[l]), batch, seq, tb)

        u = proj[:, COL_S5:COL_S5 + D].reshape(batch, nc, CHUNK, S5_GROUPS, S5_GROUP)
        u_t = u.transpose(3, 1, 0, 2, 4).reshape(S5_GROUPS, nc * batch, S5_ROW).astype(BF16)
        y_t = _s5_core(u_t, s5_ops, l, batch)
        ycore = (y_t.reshape(S5_GROUPS, nc, batch, CHUNK, S5_GROUP).transpose(2, 1, 3, 0, 4).reshape(t, D))
        eye8 = jnp.eye(LANE // S5_GROUP, dtype=F32)
        gw = s5_glu_w[l].reshape(D // LANE, LANE // S5_GROUP, S5_GROUP, S5_GROUP)
        glu_tiles = jnp.einsum("igjk,gh->igjhk", gw, eye8).reshape(D // LANE, LANE, LANE).astype(BF16)
        o_s5 = _s5_post(ycore, proj, row(s5_d[l]), glu_tiles, row(s5_glu_b[l]), tm)

        h = _merge(h, o_hg, o_rw, o_s5, proj, w_branch[l].reshape(3, D, D).astype(BF16),
                   w_out[l].astype(BF16), tm)
        h = _ffn(h, row(ffn_norm[l]), ffn_w_gate[l].astype(BF16), ffn_w_up[l].astype(BF16),
                 ffn_w_down[l].astype(BF16), row(final_norm), l == depth - 1, tm)
    return h.reshape(batch, seq, D)
```

```python
import functools
import math

import numpy as np
import jax
import jax.numpy as jnp
from jax import lax
from jax.experimental import pallas as pl
from jax.experimental.pallas import tpu as pltpu

F32 = jnp.float32
BF16 = jnp.bfloat16

D = 1024
DEPTH = 4
NORM_EPS = 1e-6
CHUNK = 64
LANE = 128
HG_HEADS = 8
RW_HEAD = 64
RW_GN_EPS = 64e-5
LORA_W, LORA_A, LORA_G, LORA_V = 64, 64, 160, 32
S5_GROUPS, S5_GROUP, S5_STATE = 64, 16, 64
FFN_HIDDEN = 2816
FFN_CHUNK = 256

COL_HG = 0
COL_RW = 4096
COL_S5 = 7168
COL_GATE = 8192
COL_LORA = 11264
IN_PACKED = 11776
VMEM_LIMIT = 56 * 1024 * 1024


def _cparams(*sem):
    return pltpu.CompilerParams(dimension_semantics=sem, vmem_limit_bytes=VMEM_LIMIT)


def _split3(x):
    hi = x.astype(BF16)
    r1 = x - hi.astype(F32)
    mid = r1.astype(BF16)
    lo = (r1 - mid.astype(F32)).astype(BF16)
    return hi, mid, lo


def _dot(a, b):
    return jnp.dot(a, b, preferred_element_type=F32)


def _dot_nt(a, b):
    return lax.dot_general(a, b, (((1,), (1,)), ((), ())), preferred_element_type=F32)


def _exact_dot01(m3, x):
    hi, mid, lo = _split3(x)
    return _dot(m3, jnp.concatenate([hi, mid, lo], axis=0))


def _rms(x, g):
    ms = jnp.mean(x * x, axis=-1, keepdims=True)
    return x * lax.rsqrt(ms + NORM_EPS) * g


def _sigmoid(x):
    return 1.0 / (1.0 + jnp.exp(-x))


def _silu(x):
    return x * _sigmoid(x)


def _inproj_body(x_ref, g_ref, w_ref, o_ref, xn_ref):
    @pl.when(pl.program_id(1) == 0)
    def _():
        xn_ref[...] = _rms(x_ref[...], g_ref[...]).astype(BF16)

    o_ref[...] = _dot(xn_ref[...], w_ref[...])


def _inproj(h, g, w, tm, tn):
    t = h.shape[0]
    n = w.shape[1]
    return pl.pallas_call(
        _inproj_body,
        grid=(t // tm, n // tn),
        in_specs=[
            pl.BlockSpec((tm, D), lambda i, j: (i, 0)),
            pl.BlockSpec((1, D), lambda i, j: (0, 0)),
            pl.BlockSpec((D, tn), lambda i, j: (0, j)),
        ],
        out_specs=pl.BlockSpec((tm, tn), lambda i, j: (i, j)),
        out_shape=jax.ShapeDtypeStruct((t, n), F32),
        scratch_shapes=[pltpu.VMEM((tm, D), BF16)],
        compiler_params=_cparams("parallel", "arbitrary"),
        name="inproj",
    )(h, g, w)


def _merge_body(h_ref, ohg_ref, orw_ref, os5_ref, g0_ref, g1_ref, g2_ref, wb_ref, wo_ref, o_ref):
    m = _sigmoid(g0_ref[...]) * _dot(ohg_ref[...], wb_ref[0])
    m = m + _sigmoid(g1_ref[...]) * _dot(orw_ref[...], wb_ref[1])
    m = m + _sigmoid(g2_ref[...]) * _dot(os5_ref[...], wb_ref[2])
    o_ref[...] = h_ref[...] + _dot(m.astype(BF16), wo_ref[...])


def _merge(h, o_hg, o_rw, o_s5, proj, wb, wo, tm):
    t = h.shape[0]
    row = lambda i: (i, 0)
    return pl.pallas_call(
        _merge_body,
        grid=(t // tm,),
        in_specs=[
            pl.BlockSpec((tm, D), row),
            pl.BlockSpec((tm, D), row),
            pl.BlockSpec((tm, D), row),
            pl.BlockSpec((tm, D), row),
            pl.BlockSpec((tm, D), lambda i: (i, COL_GATE // D)),
            pl.BlockSpec((tm, D), lambda i: (i, COL_GATE // D + 1)),
            pl.BlockSpec((tm, D), lambda i: (i, COL_GATE // D + 2)),
            pl.BlockSpec((3, D, D), lambda i: (0, 0, 0)),
            pl.BlockSpec((D, D), lambda i: (0, 0)),
        ],
        out_specs=pl.BlockSpec((tm, D), row),
        out_shape=jax.ShapeDtypeStruct((t, D), F32),
        compiler_params=_cparams("parallel"),
        name="merge",
    )(h, o_hg, o_rw, o_s5, proj, proj, proj, wb, wo)


def _ffn_body(final, h_ref, g_ref, wg_ref, wu_ref, wd_ref, fg_ref, o_ref, xn_ref, acc_ref):
    x = h_ref[...]
    xn_ref[...] = _rms(x, g_ref[...]).astype(BF16)
    for c in range(FFN_HIDDEN // FFN_CHUNK):
        sl = slice(c * FFN_CHUNK, (c + 1) * FFN_CHUNK)
        a = _dot(xn_ref[...], wg_ref[:, sl])
        b = _dot(xn_ref[...], wu_ref[:, sl])
        part = _dot((_silu(a) * b).astype(BF16), wd_ref[sl, :])
        if c == 0:
            acc_ref[...] = part
        else:
            acc_ref[...] += part
    out = x + acc_ref[...]
    if final:
        out = _rms(out, fg_ref[...])
    o_ref[...] = out


def _ffn(h, g, wg, wu, wd, fg, final, tm):
    t = h.shape[0]
    row = lambda i: (i, 0)
    fixed = lambda i: (0, 0)
    return pl.pallas_call(
        functools.partial(_ffn_body, final),
        grid=(t // tm,),
        in_specs=[
            pl.BlockSpec((tm, D), row),
            pl.BlockSpec((1, D), fixed),
            pl.BlockSpec((D, FFN_HIDDEN), fixed),
            pl.BlockSpec((D, FFN_HIDDEN), fixed),
            pl.BlockSpec((FFN_HIDDEN, D), fixed),
            pl.BlockSpec((1, D), fixed),
        ],
        out_specs=pl.BlockSpec((tm, D), row),
        out_shape=jax.ShapeDtypeStruct((t, D), F32),
        scratch_shapes=[pltpu.VMEM((tm, D), BF16), pltpu.VMEM((tm, D), F32)],
        compiler_params=_cparams("parallel"),
        name="ffn",
    )(h, g, wg, wu, wd, fg)


HG_LEVELS = (32, 16, 8, 4, 2, 1)


def _hg_dmat():
    t = np.arange(CHUNK)
    tri = (t[None, :] <= t[:, None]).astype(np.float32)
    blocks = [tri]
    for m in HG_LEVELS:
        r = (t // (2 * m)) * (2 * m) + m - 1
        blocks.append(tri - tri[r])
    d = np.concatenate(blocks, axis=0)
    return jnp.asarray(np.concatenate([d, d, d], axis=1), BF16)


def _hg_body(layer, q_ref, f_ref, i_ref, og_ref, lbl_ref, on_ref, dm_ref, o_ref, st_ref, d_ref):
    tb = q_ref.shape[0]

    @pl.when(pl.program_id(1) == 0)
    def _():
        st_ref[...] = jnp.zeros_like(st_ref)

    lg = lbl_ref[...]
    e = jnp.exp(lg - jnp.max(lg, axis=0, keepdims=True))
    sm = e / jnp.sum(e, axis=0, keepdims=True)
    lb = jnp.zeros((1, D), F32)
    for l in range(1, layer + 1):
        lb = lb + sm[l:l + 1, :]
    lb = jnp.maximum(lb, 0.0)
    log_lb = jnp.log(lb)
    log_1mlb = jnp.log1p(-lb)
    one_mlb = 1.0 - lb

    ti = lax.broadcasted_iota(jnp.int32, (CHUNK, CHUNK), 0)
    si = lax.broadcasted_iota(jnp.int32, (CHUNK, CHUNK), 1)
    xr = ti ^ si
    lower = ti > si
    pair_masks = [lower & (xr >= m) & (xr < 2 * m) for m in HG_LEVELS]
    diag_mask = ti == si
    trow = lax.broadcasted_iota(jnp.int32, (CHUNK, LANE), 0)
    second = [(trow & m) != 0 for m in HG_LEVELS]

    def chunk_step(c, carry):
        r0 = pl.multiple_of(c * CHUNK, CHUNK)
        rows = pl.ds(r0, CHUNK)
        z = f_ref[rows, :]
        ez = jnp.exp(-jnp.abs(z))
        l1p = jnp.log1p(ez)
        lsig = jnp.minimum(z, 0.0) - l1p
        cterm = log_1mlb + lsig
        mx = jnp.maximum(log_lb, cterm)
        logf = mx + jnp.log1p(jnp.exp(-jnp.abs(log_lb - cterm)))
        d_ref[...] = _exact_dot01(dm_ref[...], logf)
        kk_all = one_mlb * jnp.where(z >= 0.0, ez, 1.0) / (1.0 + ez)
        for h in range(HG_HEADS):
            ln = slice(h * LANE, (h + 1) * LANE)
            q = _silu(q_ref[rows, ln])
            k = kk_all[:, ln]
            v32 = i_ref[rows, ln]
            v = v32.astype(BF16)
            b = d_ref[0:CHUNK, ln]
            s = _dot_nt(q.astype(BF16), k.astype(BF16))
            scores = jnp.where(diag_mask, s, 0.0)
            for li in range(len(HG_LEVELS)):
                dl = d_ref[(li + 1) * CHUNK:(li + 2) * CHUNK, ln]
                x = (jnp.where(second[li], q, k) * jnp.exp(-jnp.abs(dl))).astype(BF16)
                scores = scores + jnp.where(pair_masks[li], _dot_nt(x, x), 0.0)
            st = st_ref[h]
            qe = (q * jnp.exp(b)).astype(BF16)
            o = _dot(scores.astype(BF16), v) + _dot_nt(qe, st.astype(BF16))
            b_last = b[CHUNK - 1:CHUNK, :]
            kd = (k * jnp.exp(b_last - b)).astype(BF16)
            st_ref[h] = st * jnp.exp(b_last) + _dot(v32.T.astype(BF16), kd)
            o = o * lax.rsqrt(jnp.mean(o * o, axis=-1, keepdims=True) + NORM_EPS)
            o = o * on_ref[:, ln] * _silu(og_ref[rows, ln])
            o_ref[rows, ln] = o.astype(o_ref.dtype)
        return carry

    lax.fori_loop(0, tb // CHUNK, chunk_step, 0)


def _hgrn2(proj, lb_logits, onorm, layer, batch, seq, tb):
    t = proj.shape[0]
    nb = seq // tb
    col = lambda j: (lambda b, s: (b * nb + s, COL_HG // D + j))
    fixed = lambda b, s: (0, 0)
    return pl.pallas_call(
        functools.partial(_hg_body, layer),
        grid=(batch, nb),
        in_specs=[
            pl.BlockSpec((tb, D), col(0)),
            pl.BlockSpec((tb, D), col(1)),
            pl.BlockSpec((tb, D), col(2)),
            pl.BlockSpec((tb, D), col(3)),
            pl.BlockSpec(lb_logits.shape, fixed),
            pl.BlockSpec((1, D), fixed),
            pl.BlockSpec((7 * CHUNK, 3 * CHUNK), fixed),
        ],
        out_specs=pl.BlockSpec((tb, D), lambda b, s: (b * nb + s, 0)),
        out_shape=jax.ShapeDtypeStruct((t, D), BF16),
        scratch_shapes=[pltpu.VMEM((HG_HEADS, LANE, LANE), F32),
                        pltpu.VMEM((7 * CHUNK, D), F32)],
        compiler_params=_cparams("parallel", "arbitrary"),
        name="hgrn2",
    )(proj, proj, proj, proj, lb_logits, onorm, _hg_dmat())


S5_ROW = CHUNK * S5_GROUP


def _s5_consts():
    tj = np.arange(S5_ROW)
    e_tj = (tj[None, :] // S5_GROUP == np.arange(CHUNK)[:, None]).astype(np.float32)
    f_tj = (tj[None, :] % S5_GROUP == np.arange(S5_GROUP)[:, None]).astype(np.float32)
    return (jnp.asarray(e_tj, BF16), jnp.asarray(f_tj, BF16),
            jnp.asarray(e_tj.T, BF16), jnp.asarray(f_tj.T, BF16))


def _right01(x, e):
    hi, mid, lo = _split3(x)
    return _dot(hi, e) + _dot(mid, e) + _dot(lo, e)


def _left01(e, x):
    hi, mid, lo = _split3(x)
    return _dot(e, hi) + _dot(e, mid) + _dot(e, lo)


def _dot_f32(a, b):
    ah, am, al = _split3(a)
    bh, bm, bl = _split3(b)
    return (_dot(ah, bh) + _dot(ah, bm) + _dot(am, bh)
            + _dot(ah, bl) + _dot(al, bh) + _dot(am, bm))


def _s5_prep_body(lrr_ref, lir_ref, lrc_ref, lic_ref, ls_ref, brt_ref, bit_ref, crt_ref, cit_ref,
                  etj_ref, ftj_ref, esh_ref, fsh_ref,
                  wt_ref, wi_ref, wis_ref, wo_ref, a1_ref, a2_ref):
    dt = jnp.exp(ls_ref[...])
    lam_re, lam_im = lrr_ref[...], lir_ref[...]
    lr_row, li_row = lam_re * dt, lam_im * dt
    lr_col, li_col = lrc_ref[...] * dt, lic_ref[...] * dt
    re_half = lax.broadcasted_iota(jnp.int32, (1, 2 * S5_STATE), 1) < S5_STATE

    mag = jnp.exp(lr_row)
    ab_re, ab_im = mag * jnp.cos(li_row), mag * jnp.sin(li_row)
    den = lam_re * lam_re + lam_im * lam_im
    nr, ni = ab_re - 1.0, ab_im
    coef_re = (nr * lam_re + ni * lam_im) / den
    coef_im = (ni * lam_re - nr * lam_im) / den
    bb_re = coef_re * brt_ref[...] - coef_im * bit_ref[...]
    bb_im = coef_re * bit_ref[...] + coef_im * brt_ref[...]

    def powers(lr, li, n):
        m = jnp.exp(lr * n)
        return m * jnp.cos(li * n), m * jnp.sin(li * n)

    tau = lax.broadcasted_iota(jnp.int32, (1, CHUNK), 1).astype(F32)
    p0_re, p0_im = powers(lr_col, li_col, tau)
    p1_re, p1_im = powers(lr_col, li_col, tau + 1.0)
    c_re = _right01(crt_ref[...], ftj_ref[...])
    c_im = _right01(cit_ref[...], ftj_ref[...])
    p0e_re, p0e_im = _right01(p0_re, etj_ref[...]), _right01(p0_im, etj_ref[...])
    p1e_re, p1e_im = _right01(p1_re, etj_ref[...]), _right01(p1_im, etj_ref[...])

    m_stack = jnp.concatenate([c_re * p0e_re - c_im * p0e_im, c_re * p0e_im + c_im * p0e_re], axis=0)
    r0 = _dot_f32(jnp.where(re_half, bb_re, -bb_im), m_stack)
    lane = lax.broadcasted_iota(jnp.int32, (S5_GROUP, S5_ROW), 1)
    for s in range(CHUNK):
        blk = r0 if s == 0 else jnp.where(lane >= S5_GROUP * s, pltpu.roll(r0, S5_GROUP * s, 1), 0.0)
        wt_ref[s * S5_GROUP:(s + 1) * S5_GROUP, :] = blk.astype(BF16)

    wo_ref[0:S5_STATE, :] = (c_re * p1e_re - c_im * p1e_im).astype(BF16)
    wo_ref[S5_STATE:2 * S5_STATE, :] = (-(c_re * p1e_im + c_im * p1e_re)).astype(BF16)

    back = (CHUNK - 1.0) - lax.broadcasted_iota(jnp.int32, (CHUNK, 1), 0).astype(F32)
    pt_re, pt_im = powers(lr_row, li_row, back)
    pte_re, pte_im = _left01(esh_ref[...], pt_re), _left01(esh_ref[...], pt_im)
    bbe_re, bbe_im = _left01(fsh_ref[...], bb_re), _left01(fsh_ref[...], bb_im)
    w_re = pte_re * bbe_re - pte_im * bbe_im
    w_im = pte_re * bbe_im + pte_im * bbe_re
    wi_ref[...] = jnp.where(re_half, w_re, w_im).astype(BF16)
    wis_ref[...] = jnp.where(re_half, w_im, w_re).astype(BF16)

    k = lax.broadcasted_iota(jnp.int32, (8, 1), 0)
    n = (CHUNK * (1 << k)).astype(F32)
    an_re, an_im = powers(lr_row, li_row, n)
    a1_ref[...] = an_re
    a2_ref[...] = jnp.where(re_half, -an_im, an_im)


def _s5_prep(lam_re, lam_im, log_step, b_re, b_im, c_re, c_im):
    nl = lam_re.shape[0]
    ns = 2 * S5_STATE
    g4 = lambda r, c: pl.BlockSpec((None, None, r, c), lambda l, g: (l, g, 0, 0))
    cst = lambda r, c: pl.BlockSpec((r, c), lambda l, g: (0, 0))
    sds = lambda r, c, dt: jax.ShapeDtypeStruct((nl, S5_GROUPS, r, c), dt)
    twice = lambda a: jnp.concatenate([a, a], axis=-1)
    e_tj, f_tj, e_sh, f_sh = _s5_consts()
    return pl.pallas_call(
        _s5_prep_body,
        grid=(nl, S5_GROUPS),
        in_specs=[g4(1, ns), g4(1, ns), g4(S5_STATE, 1), g4(S5_STATE, 1), g4(1, 1),
                  g4(S5_GROUP, ns), g4(S5_GROUP, ns), g4(S5_STATE, S5_GROUP), g4(S5_STATE, S5_GROUP),
                  cst(CHUNK, S5_ROW), cst(S5_GROUP, S5_ROW), cst(S5_ROW, CHUNK), cst(S5_ROW, S5_GROUP)],
        out_specs=[g4(S5_ROW, S5_ROW), g4(S5_ROW, ns), g4(S5_ROW, ns), g4(ns, S5_ROW), g4(8, ns), g4(8, ns)],
        out_shape=[sds(S5_ROW, S5_ROW, BF16), sds(S5_ROW, ns, BF16), sds(S5_ROW, ns, BF16),
                   sds(ns, S5_ROW, BF16), sds(8, ns, F32), sds(8, ns, F32)],
        compiler_params=_cparams("parallel", "parallel"),
        name="s5_prep",
    )(twice(lam_re)[:, :, None, :], twice(lam_im)[:, :, None, :], lam_re[:, :, :, None], lam_im[:, :, :, None],
      log_step[:, :, None, None],
      twice(jnp.swapaxes(b_re, 2, 3)), twice(jnp.swapaxes(b_im, 2, 3)),
      jnp.swapaxes(c_re, 2, 3), jnp.swapaxes(c_im, 2, 3),
      e_tj, f_tj, e_sh, f_sh)


def _s5_body(batch, u_ref, wt_ref, wi_ref, wis_ref, wo_ref, a1_ref, a2_ref, y_ref):
    u = u_ref[...]
    m = u.shape[0]
    x = _dot(u, wi_ref[...])
    xs = _dot(u, wis_ref[...])
    row = lax.broadcasted_iota(jnp.int32, (m, 2 * S5_STATE), 0)

    def shifted(z, n):
        return jnp.where(row >= n, pltpu.roll(z, n, 0), 0.0)

    k = 0
    while batch * (1 << k) < m:
        n = batch * (1 << k)
        a1, a2 = a1_ref[k:k + 1, :], a2_ref[k:k + 1, :]
        sx, sxs = shifted(x, n), shifted(xs, n)
        x, xs = x + a1 * sx + a2 * sxs, xs + a1 * sxs - a2 * sx
        k += 1
    y = _dot(u, wt_ref[...]) + _dot(shifted(x, batch).astype(BF16), wo_ref[...])
    y_ref[...] = y


def _s5_core(u_t, ops, layer, batch):
    wt, wi, wis, wo, a1, a2 = ops
    m = u_t.shape[1]
    ns = 2 * S5_STATE
    g3 = lambda r, c: pl.BlockSpec((None, r, c), lambda g: (g, 0, 0))
    lg = lambda r, c: pl.BlockSpec((None, None, r, c), lambda g: (layer, g, 0, 0))
    return pl.pallas_call(
        functools.partial(_s5_body, batch),
        grid=(S5_GROUPS,),
        in_specs=[g3(m, S5_ROW), lg(S5_ROW, S5_ROW), lg(S5_ROW, ns), lg(S5_ROW, ns),
                  lg(ns, S5_ROW), lg(8, ns), lg(8, ns)],
        out_specs=g3(m, S5_ROW),
        out_shape=jax.ShapeDtypeStruct((S5_GROUPS, m, S5_ROW), F32),
        compiler_params=_cparams("parallel"),
        name="s5_core",
    )(u_t, wt, wi, wis, wo, a1, a2)


def _s5_post_body(y_ref, u_ref, d_ref, gw_ref, gb_ref, o_ref):
    for i in range(D // LANE):
        ln = slice(i * LANE, (i + 1) * LANE)
        y = y_ref[:, ln] + d_ref[:, ln] * u_ref[:, ln]
        zg = 0.5 * y * (1.0 + jnp.tanh(math.sqrt(2.0 / math.pi) * (y + 0.044715 * (y * y * y))))
        gate = _dot(zg.astype(BF16), gw_ref[i]) + gb_ref[:, ln]
        o_ref[:, ln] = (zg * _sigmoid(gate)).astype(o_ref.dtype)


def _s5_post(ycore, proj, d, glu_tiles, glu_b, tm):
    t = ycore.shape[0]
    row = lambda i: (i, 0)
    fixed = lambda i: (0, 0)
    return pl.pallas_call(
        _s5_post_body,
        grid=(t // tm,),
        in_specs=[pl.BlockSpec((tm, D), row),
                  pl.BlockSpec((tm, D), lambda i: (i, COL_S5 // D)),
                  pl.BlockSpec((1, D), fixed),
                  pl.BlockSpec((D // LANE, LANE, LANE), lambda i: (0, 0, 0)),
                  pl.BlockSpec((1, D), fixed)],
        out_specs=pl.BlockSpec((tm, D), row),
        out_shape=jax.ShapeDtypeStruct((t, D), BF16),
        compiler_params=_cparams("parallel"),
        name="s5_post",
    )(ycore, proj, d, glu_tiles, glu_b)


def _ones_bd():
    i = np.arange(LANE)
    return jnp.asarray((i[:, None] // RW_HEAD == i[None, :] // RW_HEAD).astype(np.float32), BF16)


def _segsum(x, ones_bd):
    hi = x.astype(BF16)
    lo = (x - hi.astype(F32)).astype(BF16)
    return _dot(hi, ones_bd) + _dot(lo, ones_bd)


def _rw_prep_body(first, r_ref, k_ref, v_ref, lo_ref, vf_ref, mu_ref, mul_ref, w0_ref, w2_ref, a0_ref, a2_ref,
                  g2_ref, v0_ref, v1_ref, v2_ref, kk_ref, ka_ref, ones_ref,
                  ro_ref, lw_ref, ko_ref, vo_ref, kko_ref, bo_ref, go_ref, cr_ref, cl_ref):
    tb = r_ref.shape[0]

    @pl.when(pl.program_id(1) == 0)
    def _():
        cr_ref[...] = jnp.zeros_like(cr_ref)
        cl_ref[...] = jnp.zeros_like(cl_ref)

    def mix(z, carry, mu):
        row = lax.broadcasted_iota(jnp.int32, z.shape, 0)
        zs = jnp.where(row == 0, carry, pltpu.roll(z, 1, 0))
        return z + mu * (zs - z)

    def softplus(x):
        return jnp.maximum(x, 0.0) + jnp.log1p(jnp.exp(-jnp.abs(x)))

    zl = lo_ref[...]
    lora = mix(zl, cl_ref[0:1, :], mul_ref[...])
    cl_ref[0:1, :] = zl[tb - 1:tb, :]
    wd = jnp.tanh(lora[:, 0:LANE]).astype(BF16)
    ad = lora[:, LANE:2 * LANE].astype(BF16)
    gd = _sigmoid(lora[:, 2 * LANE:4 * LANE]).astype(BF16)
    w_log = -softplus(-(w0_ref[...] + _dot(wd, w2_ref[...]))) - 0.5
    lw_ref[...] = -jnp.exp(w_log)
    a = _sigmoid(a0_ref[...] + _dot(ad, a2_ref[...]))
    go_ref[...] = _dot(gd, g2_ref[...])

    zr = r_ref[...]
    ro_ref[...] = mix(zr, cr_ref[0:1, :], mu_ref[0:1, :])
    cr_ref[0:1, :] = zr[tb - 1:tb, :]
    zk = k_ref[...]
    k = mix(zk, cr_ref[1:2, :], mu_ref[1:2, :])
    cr_ref[1:2, :] = zk[tb - 1:tb, :]
    zv = v_ref[...]
    v = mix(zv, cr_ref[2:3, :], mu_ref[2:3, :])
    cr_ref[2:3, :] = zv[tb - 1:tb, :]
    if not first:
        lowr = _dot(v.astype(BF16), v1_ref[...]).astype(BF16)
        v = v + (vf_ref[...] - v) * _sigmoid(v0_ref[...] + _dot(lowr, v2_ref[...]))
    vo_ref[...] = v

    kk = k * kk_ref[...]
    ss = jnp.concatenate([_segsum(kk[:, i * LANE:(i + 1) * LANE] * kk[:, i * LANE:(i + 1) * LANE], ones_ref[...])
                          for i in range(D // LANE)], axis=1)
    kk = kk * lax.rsqrt(jnp.maximum(ss, 1e-24))
    kko_ref[...] = kk
    bo_ref[...] = kk * a
    ko_ref[...] = k * (1.0 + (a - 1.0) * ka_ref[...])


def _rw_prep(proj, v_first, p, first, batch, seq, tb):
    t = proj.shape[0]
    nb = seq // tb
    col = lambda j: (lambda b, s: (b * nb + s, COL_RW // D + j))
    fixed = lambda b, s: (0, 0)
    vec = pl.BlockSpec((1, D), fixed)
    rowblk = pl.BlockSpec((tb, D), lambda b, s: (b * nb + s, 0))
    out = jax.ShapeDtypeStruct((t, D), F32)
    return pl.pallas_call(
        functools.partial(_rw_prep_body, first),
        grid=(batch, nb),
        in_specs=[pl.BlockSpec((tb, D), col(0)), pl.BlockSpec((tb, D), col(1)), pl.BlockSpec((tb, D), col(2)),
                  pl.BlockSpec((tb, 4 * LANE), lambda b, s: (b * nb + s, COL_LORA // (4 * LANE))),
                  rowblk,
                  pl.BlockSpec((3, D), fixed), pl.BlockSpec((1, 4 * LANE), fixed),
                  vec, pl.BlockSpec((LANE, D), fixed), vec, pl.BlockSpec((LANE, D), fixed),
                  pl.BlockSpec((2 * LANE, D), fixed),
                  vec, pl.BlockSpec((D, LANE), fixed), pl.BlockSpec((LANE, D), fixed),
                  vec, vec, pl.BlockSpec((LANE, LANE), fixed)],
        out_specs=[rowblk] * 7,
        out_shape=[out] * 7,
        scratch_shapes=[pltpu.VMEM((8, D), F32), pltpu.VMEM((8, 4 * LANE), F32)],
        compiler_params=_cparams("parallel", "arbitrary"),
        name="rw_prep",
    )(proj, proj, proj, proj, v_first, p["mu"], p["mu_lora"], p["w0"], p["w2"], p["a0"], p["a2"], p["g2"],
      p["v0"], p["v1"], p["v2"], p["k_k"], p["k_a"], _ones_bd())


RW_LEVELS = (2, 4, 8, 16, 32)


def _rwc_body(r_ref, lw_ref, k_ref, v_ref, kk_ref, b_ref, g_ref, rk_ref, lnw_ref, lnb_ref, tri_ref, ones_ref,
              o_ref, ht_ref, lg_ref):
    nbat, tb = r_ref.shape[0], r_ref.shape[1]
    ntile = D // LANE

    @pl.when(pl.program_id(1) == 0)
    def _():
        ht_ref[...] = jnp.zeros_like(ht_ref)

    t64 = lax.broadcasted_iota(jnp.int32, (CHUNK, LANE), 0)
    l64 = lax.broadcasted_iota(jnp.int32, (CHUNK, LANE), 1)
    s64 = l64 & (RW_HEAD - 1)
    strict = t64 > s64
    incl = t64 >= s64
    eye = t64 == s64
    xr = t64 ^ s64
    lvl1 = strict & (xr == 1)
    lvl = [strict & (xr >= m) & (xr < 2 * m) for m in RW_LEVELS]
    head0 = l64 < RW_HEAD
    r128 = lax.broadcasted_iota(jnp.int32, (LANE, LANE), 0)
    l128 = lax.broadcasted_iota(jnp.int32, (LANE, LANE), 1)
    same_head = (r128 >> 6) == (l128 >> 6)

    def stack(x):
        return jnp.concatenate([jnp.where(head0, x, 0.0), jnp.where(head0, 0.0, x)], axis=0).astype(BF16)

    def chunk_step(c, carry):
        rows = pl.ds(pl.multiple_of(c * CHUNK, CHUNK), CHUNK)
        for bi in range(nbat):
            lg_ref[bi] = _exact_dot01(tri_ref[...], lw_ref[bi, rows, :])
        probs = [(bi, j) for bi in range(nbat) for j in range(ntile)]
        n = range(len(probs))

        def ld(ref, p):
            return ref[probs[p][0], rows, probs[p][1] * LANE:(probs[p][1] + 1) * LANE]

        def lgp(p):
            return lg_ref[probs[p][0], :, probs[p][1] * LANE:(probs[p][1] + 1) * LANE]

        a_t, r_t, a_ak, a_rk, l_ab, a_rb, x = [], [], [], [], [], [], []
        for p in n:
            lg = lgp(p)
            e_neg = jnp.exp(-lg)
            a_t.append(-ld(kk_ref, p) * jnp.exp(lg - ld(lw_ref, p)))
            r_t.append(ld(r_ref, p) * jnp.exp(lg))
            lhs = jnp.concatenate([a_t[p], r_t[p]], axis=0).astype(BF16)
            kb_neg = jnp.concatenate([stack(ld(k_ref, p) * e_neg), stack(ld(b_ref, p) * e_neg)], axis=0)
            a_kb = _dot_nt(lhs, kb_neg)
            a_ak.append(jnp.where(strict, a_kb[0:CHUNK, 0:LANE], 0.0))
            a_rk.append(jnp.where(incl, a_kb[CHUNK:2 * CHUNK, 0:LANE], 0.0))
            l_ab.append(jnp.where(strict, a_kb[0:CHUNK, LANE:2 * LANE], 0.0))
            a_rb.append(jnp.where(incl, a_kb[CHUNK:2 * CHUNK, LANE:2 * LANE], 0.0))
            x.append(jnp.where(eye, 1.0, 0.0) + jnp.where(lvl1, l_ab[p], 0.0))

        for li in range(len(RW_LEVELS)):
            z = [_dot(x[p].astype(BF16), stack(jnp.where(lvl[li], l_ab[p], 0.0))) for p in n]
            x = [x[p] + _dot(z[p].astype(BF16), stack(x[p])) for p in n]

        v_st = [stack(ld(v_ref, p)) for p in n]
        mv = [_dot(a_ak[p].astype(BF16), v_st[p]) for p in n]
        u0w = [_dot(x[p].astype(BF16), jnp.concatenate([stack(mv[p]), stack(a_t[p])], axis=1)) for p in n]
        htb = [ht_ref[p].astype(BF16) for p in n]
        u = [u0w[p][:, 0:LANE] + _dot_nt(u0w[p][:, LANE:2 * LANE].astype(BF16), htb[p]) for p in n]
        y = [_dot_nt(r_t[p].astype(BF16), htb[p]) + _dot(a_rk[p].astype(BF16), v_st[p])
             + _dot(a_rb[p].astype(BF16), stack(u[p])) for p in n]
        for p in n:
            lg = lgp(p)
            lg_end = lg[CHUNK - 1:CHUNK, :]
            e_end = jnp.exp(lg_end - lg)
            vu = jnp.concatenate([ld(v_ref, p), u[p]], axis=0)
            kb = jnp.concatenate([ld(k_ref, p) * e_end, ld(b_ref, p) * e_end], axis=0).astype(BF16)
            upd = _dot(vu.T.astype(BF16), kb)
            ht_ref[p] = ht_ref[p] * jnp.exp(lg_end) + jnp.where(same_head, upd, 0.0)

        ones = ones_ref[...]
        sl = lambda a, p: a[p * CHUNK:(p + 1) * CHUNK]
        mean = _segsum(jnp.concatenate(y, axis=0), ones) * (1.0 / RW_HEAD)
        yc = [y[p] - sl(mean, p) for p in n]
        var = _segsum(jnp.concatenate([c * c for c in yc], axis=0), ones) * (1.0 / RW_HEAD)
        rk = [ld(r_ref, p) * ld(k_ref, p) * rk_ref[:, probs[p][1] * LANE:(probs[p][1] + 1) * LANE] for p in n]
        bonus = _segsum(jnp.concatenate(rk, axis=0), ones)
        for p in n:
            bi, j = probs[p]
            ln = slice(j * LANE, (j + 1) * LANE)
            yn = yc[p] * lax.rsqrt(sl(var, p) + RW_GN_EPS) * lnw_ref[:, ln] + lnb_ref[:, ln]
            o_ref[bi, rows, ln] = ((yn + sl(bonus, p) * ld(v_ref, p)) * ld(g_ref, p)).astype(o_ref.dtype)
        return carry

    lax.fori_loop(0, tb // CHUNK, chunk_step, 0)


def _rwc_core(feats, r_k, ln_w, ln_b, batch, seq, nbat, tb):
    t = feats[0].shape[0]
    fixed = lambda b, s: (0, 0)
    vec = pl.BlockSpec((1, D), fixed)
    blk = pl.BlockSpec((nbat, tb, D), lambda b, s: (b, s, 0))
    tri = np.tril(np.ones((CHUNK, CHUNK), np.float32))
    tri3 = jnp.asarray(np.concatenate([tri, tri, tri], axis=1), BF16)
    out = pl.pallas_call(
        _rwc_body,
        grid=(batch // nbat, seq // tb),
        in_specs=[blk] * 7 + [vec, vec, vec, pl.BlockSpec((CHUNK, 3 * CHUNK), fixed),
                              pl.BlockSpec((LANE, LANE), fixed)],
        out_specs=blk,
        out_shape=jax.ShapeDtypeStruct((batch, seq, D), BF16),
        scratch_shapes=[pltpu.VMEM((nbat * (D // LANE), LANE, LANE), F32), pltpu.VMEM((nbat, CHUNK, D), F32)],
        compiler_params=_cparams("parallel", "arbitrary"),
        name="rw_core",
    )(*[f.reshape(batch, seq, D) for f in feats], r_k, ln_w, ln_b, tri3, _ones_bd())
    return out.reshape(t, D)


def _pack_w_in(w):
    o_rw, o_lora = 4096, 4096 + 3072
    o_s5 = o_lora + LORA_W + LORA_A + LORA_G
    o_gate = o_s5 + 1024
    z = lambda n: jnp.zeros((D, n), w.dtype)
    parts = [w[:, 0:o_lora], w[:, o_s5:o_gate], w[:, o_gate:o_gate + 3 * D],
             w[:, o_lora:o_lora + LORA_W], z(LANE - LORA_W),
             w[:, o_lora + LORA_W:o_lora + LORA_W + LORA_A], z(LANE - LORA_A),
             w[:, o_lora + LORA_W + LORA_A:o_s5], z(2 * LANE - LORA_G)]
    return jnp.concatenate(parts, axis=1).astype(BF16)


def _pad_rows(w, n):
    return jnp.concatenate([w, jnp.zeros((n - w.shape[0], w.shape[1]), w.dtype)], axis=0)


def _pad_lanes(v, n):
    return jnp.concatenate([v, jnp.zeros((n - v.shape[0],), v.dtype)])


def kernel(x, mix_norm, w_in, hg_lb_logits, hg_onorm, rw_shift_mu, rw_w0, rw_w2, rw_a0, rw_a2, rw_g2, rw_v0, rw_v1, rw_v2, rw_k_k, rw_k_a, rw_r_k, rw_ln_w, rw_ln_b, s5_lambda_re, s5_lambda_im, s5_log_step, s5_b_re, s5_b_im, s5_c_re, s5_c_im, s5_d, s5_glu_w, s5_glu_b, w_branch, w_out, ffn_norm, ffn_w_gate, ffn_w_up, ffn_w_down, final_norm):
    batch, seq, _ = x.shape
    t = batch * seq
    nc = seq // CHUNK
    depth = w_in.shape[0]
    tm = min(512, t)
    tb = min(256, seq)
    h = x.reshape(t, D)
    row = lambda v: v.reshape(1, -1)

    s5_ops = _s5_prep(s5_lambda_re, s5_lambda_im, s5_log_step, s5_b_re, s5_b_im, s5_c_re, s5_c_im)
    v_first = None
    for l in range(depth):
        proj = _inproj(h, row(mix_norm[l]), _pack_w_in(w_in[l]), min(1024, t), 512)

        o_hg = _hgrn2(proj, hg_lb_logits, row(hg_onorm[l]), l, batch, seq, tb)

        mu = rw_shift_mu[l]
        o1, o2, o3 = 3 * D + LORA_W, 3 * D + LORA_W + LORA_A, 3 * D + LORA_W + LORA_A + LORA_G
        first = l == 0
        rp = {
            "mu": mu[0:3 * D].reshape(3, D),
            "mu_lora": row(jnp.concatenate([_pad_lanes(mu[3 * D:o1], LANE), _pad_lanes(mu[o1:o2], LANE),
                                            _pad_lanes(mu[o2:o3], 2 * LANE)])),
            "w0": row(rw_w0[l]), "w2": _pad_rows(rw_w2[l], LANE).astype(BF16),
            "a0": row(rw_a0[l]), "a2": _pad_rows(rw_a2[l], LANE).astype(BF16),
            "g2": _pad_rows(rw_g2[l], 2 * LANE).astype(BF16),
            "v0": row(rw_v0[0 if first else l - 1]),
            "v1": jnp.concatenate([rw_v1[0 if first else l - 1],
                                   jnp.zeros((D, LANE - LORA_V), F32)], axis=1).astype(BF16),
            "v2": _pad_rows(rw_v2[0 if first else l - 1], LANE).astype(BF16),
            "k_k": row(rw_k_k[l]), "k_a": row(rw_k_a[l]),
        }
        feats = _rw_prep(proj, proj if first else v_first, rp, first, batch, seq, tb)
        if first:
            v_first = feats[3]
        o_rw = _rwc_core(feats, row(rw_r_k[l]), row(rw_ln_w[l]), row(rw_ln_b[l]), batch, seq,
                         math.gcd(batch, 4), min(128, seq))

        u = proj[:, COL_S5:COL_S5 + D].reshape(batch, nc, CHUNK, S5_GROUPS, S5_GROUP)
        u_t = u.transpose(3, 1, 0, 2, 4).reshape(S5_GROUPS, nc * batch, S5_ROW).astype(BF16)
        y_t = _s5_core(u_t, s5_ops, l, batch)
        ycore = (y_t.reshape(S5_GROUPS, nc, batch, CHUNK, S5_GROUP).transpose(2, 1, 3, 0, 4).reshape(t, D))
        eye8 = jnp.eye(LANE // S5_GROUP, dtype=F32)
        gw = s5_glu_w[l].reshape(D // LANE, LANE // S5_GROUP, S5_GROUP, S5_GROUP)
        glu_tiles = jnp.einsum("igjk,gh->igjhk", gw, eye8).reshape(D // LANE, LANE, LANE).astype(BF16)
        o_s5 = _s5_post(ycore, proj, row(s5_d[l]), glu_tiles, row(s5_glu_b[l]), tm)

        h = _merge(h, o_hg, o_rw, o_s5, proj, w_branch[l].reshape(3, D, D).astype(BF16),
                   w_out[l].astype(BF16), tm)
        h = _ffn(h, row(ffn_norm[l]), ffn_w_gate[l].astype(BF16), ffn_w_up[l].astype(BF16),
                 ffn_w_down[l].astype(BF16), row(final_norm), l == depth - 1, tm)
    return h.reshape(batch, seq, D)
```

```python
import functools
import math

import numpy as np
import jax
import jax.numpy as jnp
from jax import lax
from jax.experimental import pallas as pl
from jax.experimental.pallas import tpu as pltpu

F32 = jnp.float32
BF16 = jnp.bfloat16

D = 1024
DEPTH = 4
NORM_EPS = 1e-6
CHUNK = 64
LANE = 128
HG_HEADS = 8
RW_HEAD = 64
RW_GN_EPS = 64e-5
LORA_W, LORA_A, LORA_G, LORA_V = 64, 64, 160, 32
S5_GROUPS, S5_GROUP, S5_STATE = 64, 16, 64
FFN_HIDDEN = 2816
FFN_CHUNK = 256

C16_Q, C16_I, C16_OG, C16_R, C16_K, C16_V, C16_S5, C16_GATE = 0, 1, 2, 3, 4, 5, 6, 7
C32_F = 0
C32_LORA = 2
LOG2E = 1.4426950408889634
VMEM_LIMIT = 56 * 1024 * 1024


def _cparams(*sem):
    return pltpu.CompilerParams(dimension_semantics=sem, vmem_limit_bytes=VMEM_LIMIT)


def _split3(x):
    hi = x.astype(BF16)
    r1 = x - hi.astype(F32)
    mid = r1.astype(BF16)
    lo = (r1 - mid.astype(F32)).astype(BF16)
    return hi, mid, lo


def _dot(a, b):
    return jnp.dot(a, b, preferred_element_type=F32)


def _dot_nt(a, b):
    return lax.dot_general(a, b, (((1,), (1,)), ((), ())), preferred_element_type=F32)


def _exact_dot01(m3, x):
    hi, mid, lo = _split3(x)
    return _dot(m3, jnp.concatenate([hi, mid, lo], axis=0))


def _rms(x, g):
    ms = jnp.mean(x * x, axis=-1, keepdims=True)
    return x * lax.rsqrt(ms + NORM_EPS) * g


def _sigmoid(x):
    return 1.0 / (1.0 + jnp.exp(-x))


def _silu(x):
    return x * _sigmoid(x)


def _inproj_body(x_ref, g_ref, w_ref, o_ref, xn_ref):
    @pl.when(pl.program_id(1) == 0)
    def _():
        xn_ref[...] = _rms(x_ref[...], g_ref[...]).astype(BF16)

    o_ref[...] = _dot(xn_ref[...], w_ref[...]).astype(o_ref.dtype)


def _inproj(h, g, w, dtype, tm, tn):
    t = h.shape[0]
    n = w.shape[1]
    return pl.pallas_call(
        _inproj_body,
        grid=(t // tm, n // tn),
        in_specs=[
            pl.BlockSpec((tm, D), lambda i, j: (i, 0)),
            pl.BlockSpec((1, D), lambda i, j: (0, 0)),
            pl.BlockSpec((D, tn), lambda i, j: (0, j)),
        ],
        out_specs=pl.BlockSpec((tm, tn), lambda i, j: (i, j)),
        out_shape=jax.ShapeDtypeStruct((t, n), dtype),
        scratch_shapes=[pltpu.VMEM((tm, D), BF16)],
        compiler_params=_cparams("parallel", "arbitrary"),
        name="inproj",
    )(h, g, w)


def _merge_body(h_ref, ohg_ref, orw_ref, os5_ref, g0_ref, g1_ref, g2_ref, wb_ref, wo_ref, o_ref):
    m = _sigmoid(g0_ref[...].astype(F32)) * _dot(ohg_ref[...], wb_ref[0])
    m = m + _sigmoid(g1_ref[...].astype(F32)) * _dot(orw_ref[...], wb_ref[1])
    m = m + _sigmoid(g2_ref[...].astype(F32)) * _dot(os5_ref[...], wb_ref[2])
    o_ref[...] = h_ref[...] + _dot(m.astype(BF16), wo_ref[...])


def _merge(h, o_hg, o_rw, o_s5, proj, wb, wo, tm):
    t = h.shape[0]
    row = lambda i: (i, 0)
    return pl.pallas_call(
        _merge_body,
        grid=(t // tm,),
        in_specs=[
            pl.BlockSpec((tm, D), row),
            pl.BlockSpec((tm, D), row),
            pl.BlockSpec((tm, D), row),
            pl.BlockSpec((tm, D), row),
            pl.BlockSpec((tm, D), lambda i: (i, C16_GATE)),
            pl.BlockSpec((tm, D), lambda i: (i, C16_GATE + 1)),
            pl.BlockSpec((tm, D), lambda i: (i, C16_GATE + 2)),
            pl.BlockSpec((3, D, D), lambda i: (0, 0, 0)),
            pl.BlockSpec((D, D), lambda i: (0, 0)),
        ],
        out_specs=pl.BlockSpec((tm, D), row),
        out_shape=jax.ShapeDtypeStruct((t, D), F32),
        compiler_params=_cparams("parallel"),
        name="merge",
    )(h, o_hg, o_rw, o_s5, proj, proj, proj, wb, wo)


def _ffn_body(final, h_ref, g_ref, wg_ref, wu_ref, wd_ref, fg_ref, o_ref, xn_ref, acc_ref):
    x = h_ref[...]
    xn_ref[...] = _rms(x, g_ref[...]).astype(BF16)
    for c in range(FFN_HIDDEN // FFN_CHUNK):
        sl = slice(c * FFN_CHUNK, (c + 1) * FFN_CHUNK)
        a = _dot(xn_ref[...], wg_ref[:, sl])
        b = _dot(xn_ref[...], wu_ref[:, sl])
        part = _dot((_silu(a) * b).astype(BF16), wd_ref[sl, :])
        if c == 0:
            acc_ref[...] = part
        else:
            acc_ref[...] += part
    out = x + acc_ref[...]
    if final:
        out = _rms(out, fg_ref[...])
    o_ref[...] = out


def _ffn(h, g, wg, wu, wd, fg, final, tm):
    t = h.shape[0]
    row = lambda i: (i, 0)
    fixed = lambda i: (0, 0)
    return pl.pallas_call(
        functools.partial(_ffn_body, final),
        grid=(t // tm,),
        in_specs=[
            pl.BlockSpec((tm, D), row),
            pl.BlockSpec((1, D), fixed),
            pl.BlockSpec((D, FFN_HIDDEN), fixed),
            pl.BlockSpec((D, FFN_HIDDEN), fixed),
            pl.BlockSpec((FFN_HIDDEN, D), fixed),
            pl.BlockSpec((1, D), fixed),
        ],
        out_specs=pl.BlockSpec((tm, D), row),
        out_shape=jax.ShapeDtypeStruct((t, D), F32),
        scratch_shapes=[pltpu.VMEM((tm, D), BF16), pltpu.VMEM((tm, D), F32)],
        compiler_params=_cparams("parallel"),
        name="ffn",
    )(h, g, wg, wu, wd, fg)


HG_LEVELS = (32, 16, 8, 4, 2, 1)


def _hg_dmat():
    t = np.arange(CHUNK)
    tri = (t[None, :] <= t[:, None]).astype(np.float32)
    blocks = [tri]
    for m in HG_LEVELS:
        r = (t // (2 * m)) * (2 * m) + m - 1
        blocks.append(tri - tri[r])
    d = np.concatenate(blocks, axis=0)
    return jnp.asarray(np.concatenate([d, d, d], axis=1), BF16)


def _hg_body(layer, q_ref, f_ref, i_ref, og_ref, lbl_ref, on_ref, dm_ref, o_ref, st_ref, d_ref):
    tb = q_ref.shape[0]

    @pl.when(pl.program_id(1) == 0)
    def _():
        st_ref[...] = jnp.zeros_like(st_ref)

    lg = lbl_ref[...]
    e = jnp.exp(lg - jnp.max(lg, axis=0, keepdims=True))
    sm = e / jnp.sum(e, axis=0, keepdims=True)
    lb = jnp.zeros((1, D), F32)
    for l in range(1, layer + 1):
        lb = lb + sm[l:l + 1, :]
    lb = jnp.maximum(lb, 0.0)
    log_lb = jnp.log(lb)
    log_1mlb = jnp.log1p(-lb)
    one_mlb = 1.0 - lb

    ti = lax.broadcasted_iota(jnp.int32, (CHUNK, CHUNK), 0)
    si = lax.broadcasted_iota(jnp.int32, (CHUNK, CHUNK), 1)
    xr = ti ^ si
    lower = ti > si
    pair_masks = [lower & (xr >= m) & (xr < 2 * m) for m in HG_LEVELS]
    diag_mask = ti == si
    trow = lax.broadcasted_iota(jnp.int32, (CHUNK, LANE), 0)
    second = [(trow & m) != 0 for m in HG_LEVELS]

    def chunk_step(c, carry):
        r0 = pl.multiple_of(c * CHUNK, CHUNK)
        rows = pl.ds(r0, CHUNK)
        z = f_ref[rows, :]
        ez = jnp.exp(-jnp.abs(z))
        l1p = jnp.log1p(ez)
        lsig = jnp.minimum(z, 0.0) - l1p
        cterm = log_1mlb + lsig
        mx = jnp.maximum(log_lb, cterm)
        logf = mx + jnp.log1p(jnp.exp(-jnp.abs(log_lb - cterm)))
        d_ref[...] = _exact_dot01(dm_ref[...], logf * LOG2E)
        kk_all = one_mlb * jnp.where(z >= 0.0, ez, 1.0) / (1.0 + ez)
        heads = range(HG_HEADS)
        lns = [slice(h * LANE, (h + 1) * LANE) for h in heads]
        q = [_silu(q_ref[rows, lns[h]].astype(F32)) for h in heads]
        k = [kk_all[:, lns[h]] for h in heads]
        s0 = [_dot_nt(q[h].astype(BF16), k[h].astype(BF16)) for h in heads]
        scores = [jnp.where(diag_mask, s0[h], 0.0) for h in heads]
        for li in range(len(HG_LEVELS)):
            x = [(jnp.where(second[li], q[h], k[h])
                  * jnp.exp2(-jnp.abs(d_ref[(li + 1) * CHUNK:(li + 2) * CHUNK, lns[h]]))).astype(BF16)
                 for h in heads]
            sl = [_dot_nt(x[h], x[h]) for h in heads]
            scores = [scores[h] + jnp.where(pair_masks[li], sl[h], 0.0) for h in heads]
        o = []
        for h in heads:
            b = d_ref[0:CHUNK, lns[h]]
            qe = (q[h] * jnp.exp2(b)).astype(BF16)
            o.append(_dot(scores[h].astype(BF16), i_ref[rows, lns[h]].astype(BF16))
                     + _dot_nt(qe, st_ref[h].astype(BF16)))
        for h in heads:
            b = d_ref[0:CHUNK, lns[h]]
            b_last = b[CHUNK - 1:CHUNK, :]
            kd = (k[h] * jnp.exp2(b_last - b)).astype(BF16)
            vt = i_ref[rows, lns[h]].astype(F32).T.astype(BF16)
            st_ref[h] = st_ref[h] * jnp.exp2(b_last) + _dot(vt, kd)
        for h in heads:
            oh = o[h] * lax.rsqrt(jnp.mean(o[h] * o[h], axis=-1, keepdims=True) + NORM_EPS)
            oh = oh * on_ref[:, lns[h]] * _silu(og_ref[rows, lns[h]].astype(F32))
            o_ref[rows, lns[h]] = oh.astype(o_ref.dtype)
        return carry

    lax.fori_loop(0, tb // CHUNK, chunk_step, 0)


def _hgrn2(p16, p32, lb_logits, onorm, layer, batch, seq, tb):
    t = p16.shape[0]
    nb = seq // tb
    col = lambda j: (lambda b, s: (b * nb + s, j))
    fixed = lambda b, s: (0, 0)
    return pl.pallas_call(
        functools.partial(_hg_body, layer),
        grid=(batch, nb),
        in_specs=[
            pl.BlockSpec((tb, D), col(C16_Q)),
            pl.BlockSpec((tb, D), col(C32_F)),
            pl.BlockSpec((tb, D), col(C16_I)),
            pl.BlockSpec((tb, D), col(C16_OG)),
            pl.BlockSpec(lb_logits.shape, fixed),
            pl.BlockSpec((1, D), fixed),
            pl.BlockSpec((7 * CHUNK, 3 * CHUNK), fixed),
        ],
        out_specs=pl.BlockSpec((tb, D), lambda b, s: (b * nb + s, 0)),
        out_shape=jax.ShapeDtypeStruct((t, D), BF16),
        scratch_shapes=[pltpu.VMEM((HG_HEADS, LANE, LANE), F32),
                        pltpu.VMEM((7 * CHUNK, D), F32)],
        compiler_params=_cparams("parallel", "arbitrary"),
        name="hgrn2",
    )(p16, p32, p16, p16, lb_logits, onorm, _hg_dmat())


S5_ROW = CHUNK * S5_GROUP


def _s5_consts():
    tj = np.arange(S5_ROW)
    e_tj = (tj[None, :] // S5_GROUP == np.arange(CHUNK)[:, None]).astype(np.float32)
    f_tj = (tj[None, :] % S5_GROUP == np.arange(S5_GROUP)[:, None]).astype(np.float32)
    return (jnp.asarray(e_tj, BF16), jnp.asarray(f_tj, BF16),
            jnp.asarray(e_tj.T, BF16), jnp.asarray(f_tj.T, BF16))


def _right01(x, e):
    hi, mid, lo = _split3(x)
    return _dot(hi, e) + _dot(mid, e) + _dot(lo, e)


def _left01(e, x):
    hi, mid, lo = _split3(x)
    return _dot(e, hi) + _dot(e, mid) + _dot(e, lo)


def _dot_f32(a, b):
    ah, am, al = _split3(a)
    bh, bm, bl = _split3(b)
    return (_dot(ah, bh) + _dot(ah, bm) + _dot(am, bh)
            + _dot(ah, bl) + _dot(al, bh) + _dot(am, bm))


def _s5_prep_body(lrr_ref, lir_ref, lrc_ref, lic_ref, ls_ref, brt_ref, bit_ref, crt_ref, cit_ref,
                  etj_ref, ftj_ref, esh_ref, fsh_ref,
                  wt_ref, wi_ref, wis_ref, wo_ref, a1_ref, a2_ref):
    dt = jnp.exp(ls_ref[...])
    lam_re, lam_im = lrr_ref[...], lir_ref[...]
    lr_row, li_row = lam_re * dt, lam_im * dt
    lr_col, li_col = lrc_ref[...] * dt, lic_ref[...] * dt
    re_half = lax.broadcasted_iota(jnp.int32, (1, 2 * S5_STATE), 1) < S5_STATE

    mag = jnp.exp(lr_row)
    ab_re, ab_im = mag * jnp.cos(li_row), mag * jnp.sin(li_row)
    den = lam_re * lam_re + lam_im * lam_im
    nr, ni = ab_re - 1.0, ab_im
    coef_re = (nr * lam_re + ni * lam_im) / den
    coef_im = (ni * lam_re - nr * lam_im) / den
    bb_re = coef_re * brt_ref[...] - coef_im * bit_ref[...]
    bb_im = coef_re * bit_ref[...] + coef_im * brt_ref[...]

    def powers(lr, li, n):
        m = jnp.exp(lr * n)
        return m * jnp.cos(li * n), m * jnp.sin(li * n)

    tau = lax.broadcasted_iota(jnp.int32, (1, CHUNK), 1).astype(F32)
    p0_re, p0_im = powers(lr_col, li_col, tau)
    p1_re, p1_im = powers(lr_col, li_col, tau + 1.0)
    c_re = _right01(crt_ref[...], ftj_ref[...])
    c_im = _right01(cit_ref[...], ftj_ref[...])
    p0e_re, p0e_im = _right01(p0_re, etj_ref[...]), _right01(p0_im, etj_ref[...])
    p1e_re, p1e_im = _right01(p1_re, etj_ref[...]), _right01(p1_im, etj_ref[...])

    m_stack = jnp.concatenate([c_re * p0e_re - c_im * p0e_im, c_re * p0e_im + c_im * p0e_re], axis=0)
    r0 = _dot_f32(jnp.where(re_half, bb_re, -bb_im), m_stack)
    lane = lax.broadcasted_iota(jnp.int32, (S5_GROUP, S5_ROW), 1)
    for s in range(CHUNK):
        blk = r0 if s == 0 else jnp.where(lane >= S5_GROUP * s, pltpu.roll(r0, S5_GROUP * s, 1), 0.0)
        wt_ref[s * S5_GROUP:(s + 1) * S5_GROUP, :] = blk.astype(BF16)

    wo_ref[0:S5_STATE, :] = (c_re * p1e_re - c_im * p1e_im).astype(BF16)
    wo_ref[S5_STATE:2 * S5_STATE, :] = (-(c_re * p1e_im + c_im * p1e_re)).astype(BF16)

    back = (CHUNK - 1.0) - lax.broadcasted_iota(jnp.int32, (CHUNK, 1), 0).astype(F32)
    pt_re, pt_im = powers(lr_row, li_row, back)
    pte_re, pte_im = _left01(esh_ref[...], pt_re), _left01(esh_ref[...], pt_im)
    bbe_re, bbe_im = _left01(fsh_ref[...], bb_re), _left01(fsh_ref[...], bb_im)
    w_re = pte_re * bbe_re - pte_im * bbe_im
    w_im = pte_re * bbe_im + pte_im * bbe_re
    wi_ref[...] = jnp.where(re_half, w_re, w_im).astype(BF16)
    wis_ref[...] = jnp.where(re_half, w_im, w_re).astype(BF16)

    k = lax.broadcasted_iota(jnp.int32, (8, 1), 0)
    n = (CHUNK * (1 << k)).astype(F32)
    an_re, an_im = powers(lr_row, li_row, n)
    a1_ref[...] = an_re
    a2_ref[...] = jnp.where(re_half, -an_im, an_im)


def _s5_prep(lam_re, lam_im, log_step, b_re, b_im, c_re, c_im):
    nl = lam_re.shape[0]
    ns = 2 * S5_STATE
    g4 = lambda r, c: pl.BlockSpec((None, None, r, c), lambda l, g: (l, g, 0, 0))
    cst = lambda r, c: pl.BlockSpec((r, c), lambda l, g: (0, 0))
    sds = lambda r, c, dt: jax.ShapeDtypeStruct((nl, S5_GROUPS, r, c), dt)
    twice = lambda a: jnp.concatenate([a, a], axis=-1)
    e_tj, f_tj, e_sh, f_sh = _s5_consts()
    return pl.pallas_call(
        _s5_prep_body,
        grid=(nl, S5_GROUPS),
        in_specs=[g4(1, ns), g4(1, ns), g4(S5_STATE, 1), g4(S5_STATE, 1), g4(1, 1),
                  g4(S5_GROUP, ns), g4(S5_GROUP, ns), g4(S5_STATE, S5_GROUP), g4(S5_STATE, S5_GROUP),
                  cst(CHUNK, S5_ROW), cst(S5_GROUP, S5_ROW), cst(S5_ROW, CHUNK), cst(S5_ROW, S5_GROUP)],
        out_specs=[g4(S5_ROW, S5_ROW), g4(S5_ROW, ns), g4(S5_ROW, ns), g4(ns, S5_ROW), g4(8, ns), g4(8, ns)],
        out_shape=[sds(S5_ROW, S5_ROW, BF16), sds(S5_ROW, ns, BF16), sds(S5_ROW, ns, BF16),
                   sds(ns, S5_ROW, BF16), sds(8, ns, F32), sds(8, ns, F32)],
        compiler_params=_cparams("parallel", "parallel"),
        name="s5_prep",
    )(twice(lam_re)[:, :, None, :], twice(lam_im)[:, :, None, :], lam_re[:, :, :, None], lam_im[:, :, :, None],
      log_step[:, :, None, None],
      twice(jnp.swapaxes(b_re, 2, 3)), twice(jnp.swapaxes(b_im, 2, 3)),
      jnp.swapaxes(c_re, 2, 3), jnp.swapaxes(c_im, 2, 3),
      e_tj, f_tj, e_sh, f_sh)


def _s5_body(batch, u_ref, wt_ref, wi_ref, wis_ref, wo_ref, a1_ref, a2_ref, y_ref):
    u = u_ref[...]
    m = u.shape[0]
    x = _dot(u, wi_ref[...])
    xs = _dot(u, wis_ref[...])
    row = lax.broadcasted_iota(jnp.int32, (m, 2 * S5_STATE), 0)

    def shifted(z, n):
        return jnp.where(row >= n, pltpu.roll(z, n, 0), 0.0)

    k = 0
    while batch * (1 << k) < m:
        n = batch * (1 << k)
        a1, a2 = a1_ref[k:k + 1, :], a2_ref[k:k + 1, :]
        sx, sxs = shifted(x, n), shifted(xs, n)
        x, xs = x + a1 * sx + a2 * sxs, xs + a1 * sxs - a2 * sx
        k += 1
    y = _dot(u, wt_ref[...]) + _dot(shifted(x, batch).astype(BF16), wo_ref[...])
    y_ref[...] = y.astype(y_ref.dtype)


def _s5_core(u_t, ops, layer, batch):
    wt, wi, wis, wo, a1, a2 = ops
    m = u_t.shape[1]
    ns = 2 * S5_STATE
    g3 = lambda r, c: pl.BlockSpec((None, r, c), lambda g: (g, 0, 0))
    lg = lambda r, c: pl.BlockSpec((None, None, r, c), lambda g: (layer, g, 0, 0))
    return pl.pallas_call(
        functools.partial(_s5_body, batch),
        grid=(S5_GROUPS,),
        in_specs=[g3(m, S5_ROW), lg(S5_ROW, S5_ROW), lg(S5_ROW, ns), lg(S5_ROW, ns),
                  lg(ns, S5_ROW), lg(8, ns), lg(8, ns)],
        out_specs=g3(m, S5_ROW),
        out_shape=jax.ShapeDtypeStruct((S5_GROUPS, m, S5_ROW), BF16),
        compiler_params=_cparams("parallel"),
        name="s5_core",
    )(u_t, wt, wi, wis, wo, a1, a2)


def _s5_post_body(y_ref, u_ref, d_ref, gw_ref, gb_ref, o_ref):
    for i in range(D // LANE):
        ln = slice(i * LANE, (i + 1) * LANE)
        y = y_ref[:, ln].astype(F32) + d_ref[:, ln] * u_ref[:, ln].astype(F32)
        zg = 0.5 * y * (1.0 + jnp.tanh(math.sqrt(2.0 / math.pi) * (y + 0.044715 * (y * y * y))))
        gate = _dot(zg.astype(BF16), gw_ref[i]) + gb_ref[:, ln]
        o_ref[:, ln] = (zg * _sigmoid(gate)).astype(o_ref.dtype)


def _s5_post(ycore, proj, d, glu_tiles, glu_b, tm):
    t = ycore.shape[0]
    row = lambda i: (i, 0)
    fixed = lambda i: (0, 0)
    return pl.pallas_call(
        _s5_post_body,
        grid=(t // tm,),
        in_specs=[pl.BlockSpec((tm, D), row),
                  pl.BlockSpec((tm, D), lambda i: (i, C16_S5)),
                  pl.BlockSpec((1, D), fixed),
                  pl.BlockSpec((D // LANE, LANE, LANE), lambda i: (0, 0, 0)),
                  pl.BlockSpec((1, D), fixed)],
        out_specs=pl.BlockSpec((tm, D), row),
        out_shape=jax.ShapeDtypeStruct((t, D), BF16),
        compiler_params=_cparams("parallel"),
        name="s5_post",
    )(ycore, proj, d, glu_tiles, glu_b)


def _ones_bd():
    i = np.arange(LANE)
    return jnp.asarray((i[:, None] // RW_HEAD == i[None, :] // RW_HEAD).astype(np.float32), BF16)


def _segsum(x, ones_bd):
    hi = x.astype(BF16)
    lo = (x - hi.astype(F32)).astype(BF16)
    return _dot(hi, ones_bd) + _dot(lo, ones_bd)


def _rw_prep_body(first, r_ref, k_ref, v_ref, lo_ref, vf_ref, mu_ref, mul_ref, w0_ref, w2_ref, a0_ref, a2_ref,
                  g2_ref, v0_ref, v1_ref, v2_ref, kk_ref, ka_ref, ones_ref,
                  ro_ref, lw_ref, ko_ref, vo_ref, kko_ref, bo_ref, go_ref, cr_ref, cl_ref):
    tb = r_ref.shape[0]

    @pl.when(pl.program_id(1) == 0)
    def _():
        cr_ref[...] = jnp.zeros_like(cr_ref)
        cl_ref[...] = jnp.zeros_like(cl_ref)

    def mix(z, carry, mu):
        row = lax.broadcasted_iota(jnp.int32, z.shape, 0)
        zs = jnp.where(row == 0, carry, pltpu.roll(z, 1, 0))
        return z + mu * (zs - z)

    def softplus(x):
        return jnp.maximum(x, 0.0) + jnp.log1p(jnp.exp(-jnp.abs(x)))

    zl = lo_ref[...]
    lora = mix(zl, cl_ref[0:1, :], mul_ref[...])
    cl_ref[0:1, :] = zl[tb - 1:tb, :]
    wd = jnp.tanh(lora[:, 0:LANE]).astype(BF16)
    ad = lora[:, LANE:2 * LANE].astype(BF16)
    gd = _sigmoid(lora[:, 2 * LANE:4 * LANE]).astype(BF16)
    w_log = -softplus(-(w0_ref[...] + _dot(wd, w2_ref[...]))) - 0.5
    lw_ref[...] = -jnp.exp(w_log)
    a = _sigmoid(a0_ref[...] + _dot(ad, a2_ref[...]))
    go_ref[...] = _dot(gd, g2_ref[...]).astype(go_ref.dtype)

    zr = r_ref[...].astype(F32)
    ro_ref[...] = mix(zr, cr_ref[0:1, :], mu_ref[0:1, :]).astype(ro_ref.dtype)
    cr_ref[0:1, :] = zr[tb - 1:tb, :]
    zk = k_ref[...].astype(F32)
    k = mix(zk, cr_ref[1:2, :], mu_ref[1:2, :])
    cr_ref[1:2, :] = zk[tb - 1:tb, :]
    zv = v_ref[...].astype(F32)
    v = mix(zv, cr_ref[2:3, :], mu_ref[2:3, :])
    cr_ref[2:3, :] = zv[tb - 1:tb, :]
    if not first:
        lowr = _dot(v.astype(BF16), v1_ref[...]).astype(BF16)
        v = v + (vf_ref[...].astype(F32) - v) * _sigmoid(v0_ref[...] + _dot(lowr, v2_ref[...]))
    vo_ref[...] = v.astype(vo_ref.dtype)

    kk = k * kk_ref[...]
    ss = jnp.concatenate([_segsum(kk[:, i * LANE:(i + 1) * LANE] * kk[:, i * LANE:(i + 1) * LANE], ones_ref[...])
                          for i in range(D // LANE)], axis=1)
    kk = kk * lax.rsqrt(jnp.maximum(ss, 1e-24))
    kko_ref[...] = kk.astype(kko_ref.dtype)
    bo_ref[...] = (kk * a).astype(bo_ref.dtype)
    ko_ref[...] = (k * (1.0 + (a - 1.0) * ka_ref[...])).astype(ko_ref.dtype)


def _rw_prep(p16, p32, v_first, p, first, batch, seq, tb):
    t = p16.shape[0]
    nb = seq // tb
    col = lambda j: (lambda b, s: (b * nb + s, j))
    fixed = lambda b, s: (0, 0)
    vec = pl.BlockSpec((1, D), fixed)
    rowblk = pl.BlockSpec((tb, D), lambda b, s: (b * nb + s, 0))
    out = jax.ShapeDtypeStruct((t, D), BF16)
    return pl.pallas_call(
        functools.partial(_rw_prep_body, first),
        grid=(batch, nb),
        in_specs=[pl.BlockSpec((tb, D), col(C16_R)), pl.BlockSpec((tb, D), col(C16_K)),
                  pl.BlockSpec((tb, D), col(C16_V)),
                  pl.BlockSpec((tb, 4 * LANE), col(C32_LORA)),
                  rowblk,
                  pl.BlockSpec((3, D), fixed), pl.BlockSpec((1, 4 * LANE), fixed),
                  vec, pl.BlockSpec((LANE, D), fixed), vec, pl.BlockSpec((LANE, D), fixed),
                  pl.BlockSpec((2 * LANE, D), fixed),
                  vec, pl.BlockSpec((D, LANE), fixed), pl.BlockSpec((LANE, D), fixed),
                  vec, vec, pl.BlockSpec((LANE, LANE), fixed)],
        out_specs=[rowblk] * 7,
        out_shape=[out, jax.ShapeDtypeStruct((t, D), F32)] + [out] * 5,
        scratch_shapes=[pltpu.VMEM((8, D), F32), pltpu.VMEM((8, 4 * LANE), F32)],
        compiler_params=_cparams("parallel", "arbitrary"),
        name="rw_prep",
    )(p16, p16, p16, p32, v_first, p["mu"], p["mu_lora"], p["w0"], p["w2"], p["a0"], p["a2"], p["g2"],
      p["v0"], p["v1"], p["v2"], p["k_k"], p["k_a"], _ones_bd())


RW_LEVELS = (2, 4, 8, 16, 32)


def _rwc_body(r_ref, lw_ref, k_ref, v_ref, kk_ref, b_ref, g_ref, rk_ref, lnw_ref, lnb_ref, tri_ref, ones_ref,
              o_ref, ht_ref, lg_ref):
    nbat, tb = r_ref.shape[0], r_ref.shape[1]
    ntile = D // LANE

    @pl.when(pl.program_id(1) == 0)
    def _():
        ht_ref[...] = jnp.zeros_like(ht_ref)

    t64 = lax.broadcasted_iota(jnp.int32, (CHUNK, LANE), 0)
    l64 = lax.broadcasted_iota(jnp.int32, (CHUNK, LANE), 1)
    s64 = l64 & (RW_HEAD - 1)
    strict = t64 > s64
    incl = t64 >= s64
    eye = t64 == s64
    xr = t64 ^ s64
    lvl1 = strict & (xr == 1)
    lvl = [strict & (xr >= m) & (xr < 2 * m) for m in RW_LEVELS]
    head0 = l64 < RW_HEAD
    r128 = lax.broadcasted_iota(jnp.int32, (LANE, LANE), 0)
    l128 = lax.broadcasted_iota(jnp.int32, (LANE, LANE), 1)
    same_head = (r128 >> 6) == (l128 >> 6)

    def stack(x):
        return jnp.concatenate([jnp.where(head0, x, 0.0), jnp.where(head0, 0.0, x)], axis=0).astype(BF16)

    def chunk_step(c, carry):
        rows = pl.ds(pl.multiple_of(c * CHUNK, CHUNK), CHUNK)
        for bi in range(nbat):
            lg_ref[bi] = _exact_dot01(tri_ref[...], lw_ref[bi, rows, :])
        probs = [(bi, j) for bi in range(nbat) for j in range(ntile)]
        n = range(len(probs))

        def ld(ref, p):
            return ref[probs[p][0], rows, probs[p][1] * LANE:(probs[p][1] + 1) * LANE].astype(F32)

        def lgp(p):
            return lg_ref[probs[p][0], :, probs[p][1] * LANE:(probs[p][1] + 1) * LANE]

        a_t, r_t, a_ak, a_rk, l_ab, a_rb, x = [], [], [], [], [], [], []
        for p in n:
            lg = lgp(p)
            e_neg = jnp.exp(-lg)
            a_t.append(-ld(kk_ref, p) * jnp.exp(lg - ld(lw_ref, p)))
            r_t.append(ld(r_ref, p) * jnp.exp(lg))
            lhs = jnp.concatenate([a_t[p], r_t[p]], axis=0).astype(BF16)
            kb_neg = jnp.concatenate([stack(ld(k_ref, p) * e_neg), stack(ld(b_ref, p) * e_neg)], axis=0)
            a_kb = _dot_nt(lhs, kb_neg)
            a_ak.append(jnp.where(strict, a_kb[0:CHUNK, 0:LANE], 0.0))
            a_rk.append(jnp.where(incl, a_kb[CHUNK:2 * CHUNK, 0:LANE], 0.0))
            l_ab.append(jnp.where(strict, a_kb[0:CHUNK, LANE:2 * LANE], 0.0))
            a_rb.append(jnp.where(incl, a_kb[CHUNK:2 * CHUNK, LANE:2 * LANE], 0.0))
            x.append(jnp.where(eye, 1.0, 0.0) + jnp.where(lvl1, l_ab[p], 0.0))

        for li in range(len(RW_LEVELS)):
            z = [_dot(x[p].astype(BF16), stack(jnp.where(lvl[li], l_ab[p], 0.0))) for p in n]
            x = [x[p] + _dot(z[p].astype(BF16), stack(x[p])) for p in n]

        v_st = [stack(ld(v_ref, p)) for p in n]
        mv = [_dot(a_ak[p].astype(BF16), v_st[p]) for p in n]
        u0w = [_dot(x[p].astype(BF16), jnp.concatenate([stack(mv[p]), stack(a_t[p])], axis=1)) for p in n]
        htb = [ht_ref[p].astype(BF16) for p in n]
        u = [u0w[p][:, 0:LANE] + _dot_nt(u0w[p][:, LANE:2 * LANE].astype(BF16), htb[p]) for p in n]
        y = [_dot_nt(r_t[p].astype(BF16), htb[p]) + _dot(a_rk[p].astype(BF16), v_st[p])
             + _dot(a_rb[p].astype(BF16), stack(u[p])) for p in n]
        for p in n:
            lg = lgp(p)
            lg_end = lg[CHUNK - 1:CHUNK, :]
            e_end = jnp.exp(lg_end - lg)
            vu = jnp.concatenate([ld(v_ref, p), u[p]], axis=0)
            kb = jnp.concatenate([ld(k_ref, p) * e_end, ld(b_ref, p) * e_end], axis=0).astype(BF16)
            upd = _dot(vu.T.astype(BF16), kb)
            ht_ref[p] = ht_ref[p] * jnp.exp(lg_end) + jnp.where(same_head, upd, 0.0)

        ones = ones_ref[...]
        sl = lambda a, p: a[p * CHUNK:(p + 1) * CHUNK]
        mean = _segsum(jnp.concatenate(y, axis=0), ones) * (1.0 / RW_HEAD)
        yc = [y[p] - sl(mean, p) for p in n]
        var = _segsum(jnp.concatenate([c * c for c in yc], axis=0), ones) * (1.0 / RW_HEAD)
        rk = [ld(r_ref, p) * ld(k_ref, p) * rk_ref[:, probs[p][1] * LANE:(probs[p][1] + 1) * LANE] for p in n]
        bonus = _segsum(jnp.concatenate(rk, axis=0), ones)
        for p in n:
            bi, j = probs[p]
            ln = slice(j * LANE, (j + 1) * LANE)
            yn = yc[p] * lax.rsqrt(sl(var, p) + RW_GN_EPS) * lnw_ref[:, ln] + lnb_ref[:, ln]
            o_ref[bi, rows, ln] = ((yn + sl(bonus, p) * ld(v_ref, p)) * ld(g_ref, p)).astype(o_ref.dtype)
        return carry

    lax.fori_loop(0, tb // CHUNK, chunk_step, 0)


def _rwc_core(feats, r_k, ln_w, ln_b, batch, seq, nbat, tb):
    t = feats[0].shape[0]
    fixed = lambda b, s: (0, 0)
    vec = pl.BlockSpec((1, D), fixed)
    blk = pl.BlockSpec((nbat, tb, D), lambda b, s: (b, s, 0))
    tri = np.tril(np.ones((CHUNK, CHUNK), np.float32))
    tri3 = jnp.asarray(np.concatenate([tri, tri, tri], axis=1), BF16)
    out = pl.pallas_call(
        _rwc_body,
        grid=(batch // nbat, seq // tb),
        in_specs=[blk] * 7 + [vec, vec, vec, pl.BlockSpec((CHUNK, 3 * CHUNK), fixed),
                              pl.BlockSpec((LANE, LANE), fixed)],
        out_specs=blk,
        out_shape=jax.ShapeDtypeStruct((batch, seq, D), BF16),
        scratch_shapes=[pltpu.VMEM((nbat * (D // LANE), LANE, LANE), F32), pltpu.VMEM((nbat, CHUNK, D), F32)],
        compiler_params=_cparams("parallel", "arbitrary"),
        name="rw_core",
    )(*[f.reshape(batch, seq, D) for f in feats], r_k, ln_w, ln_b, tri3, _ones_bd())
    return out.reshape(t, D)


def _pack_w_in(w):
    o_lora = 4 * D + 3 * D
    o_a, o_g = o_lora + LORA_W, o_lora + LORA_W + LORA_A
    o_s5 = o_g + LORA_G
    o_gate = o_s5 + D
    z = lambda n: jnp.zeros((D, n), w.dtype)
    w16 = jnp.concatenate([w[:, 0:D], w[:, 2 * D:o_lora], w[:, o_s5:o_gate + 3 * D]], axis=1)
    w32 = jnp.concatenate([w[:, D:2 * D], w[:, o_lora:o_a], z(LANE - LORA_W), w[:, o_a:o_g], z(LANE - LORA_A),
                           w[:, o_g:o_s5], z(2 * LANE - LORA_G)], axis=1)
    return w16, w32


def _pad_rows(w, n):
    return jnp.concatenate([w, jnp.zeros((n - w.shape[0], w.shape[1]), w.dtype)], axis=0)


def _pad_lanes(v, n):
    return jnp.concatenate([v, jnp.zeros((n - v.shape[0],), v.dtype)])


def kernel(x, mix_norm, w_in, hg_lb_logits, hg_onorm, rw_shift_mu, rw_w0, rw_w2, rw_a0, rw_a2, rw_g2, rw_v0, rw_v1, rw_v2, rw_k_k, rw_k_a, rw_r_k, rw_ln_w, rw_ln_b, s5_lambda_re, s5_lambda_im, s5_log_step, s5_b_re, s5_b_im, s5_c_re, s5_c_im, s5_d, s5_glu_w, s5_glu_b, w_branch, w_out, ffn_norm, ffn_w_gate, ffn_w_up, ffn_w_down, final_norm):
    batch, seq, _ = x.shape
    t = batch * seq
    nc = seq // CHUNK
    depth = w_in.shape[0]
    tm = min(512, t)
    tb = min(256, seq)
    h = x.reshape(t, D)
    row = lambda v: v.reshape(1, -1)

    s5_ops = _s5_prep(s5_lambda_re, s5_lambda_im, s5_log_step, s5_b_re, s5_b_im, s5_c_re, s5_c_im)
    w_in_bf = w_in.astype(BF16)
    v_first = None
    for l in range(depth):
        w16, w32 = _pack_w_in(w_in_bf[l])
        p16 = _inproj(h, row(mix_norm[l]), w16, BF16, min(1024, t), 512)
        p32 = _inproj(h, row(mix_norm[l]), w32, F32, min(1024, t), 512)

        o_hg = _hgrn2(p16, p32, hg_lb_logits, row(hg_onorm[l]), l, batch, seq, tb)

        mu = rw_shift_mu[l]
        o1, o2, o3 = 3 * D + LORA_W, 3 * D + LORA_W + LORA_A, 3 * D + LORA_W + LORA_A + LORA_G
        first = l == 0
        rp = {
            "mu": mu[0:3 * D].reshape(3, D),
            "mu_lora": row(jnp.concatenate([_pad_lanes(mu[3 * D:o1], LANE), _pad_lanes(mu[o1:o2], LANE),
                                            _pad_lanes(mu[o2:o3], 2 * LANE)])),
            "w0": row(rw_w0[l]), "w2": _pad_rows(rw_w2[l], LANE).astype(BF16),
            "a0": row(rw_a0[l]), "a2": _pad_rows(rw_a2[l], LANE).astype(BF16),
            "g2": _pad_rows(rw_g2[l], 2 * LANE).astype(BF16),
            "v0": row(rw_v0[0 if first else l - 1]),
            "v1": jnp.concatenate([rw_v1[0 if first else l - 1],
                                   jnp.zeros((D, LANE - LORA_V), F32)], axis=1).astype(BF16),
            "v2": _pad_rows(rw_v2[0 if first else l - 1], LANE).astype(BF16),
            "k_k": row(rw_k_k[l]), "k_a": row(rw_k_a[l]),
        }
        feats = _rw_prep(p16, p32, p16 if first else v_first, rp, first, batch, seq, tb)
        if first:
            v_first = feats[3]
        o_rw = _rwc_core(feats, row(rw_r_k[l]), row(rw_ln_w[l]), row(rw_ln_b[l]), batch, seq,
                         math.gcd(batch, 4), min(128, seq))

        u = p16[:, C16_S5 * D:(C16_S5 + 1) * D].reshape(batch, nc, CHUNK, S5_GROUPS, S5_GROUP)
        u_t = u.transpose(3, 1, 0, 2, 4).reshape(S5_GROUPS, nc * batch, S5_ROW)
        y_t = _s5_core(u_t, s5_ops, l, batch)
        ycore = (y_t.reshape(S5_GROUPS, nc, batch, CHUNK, S5_GROUP).transpose(2, 1, 3, 0, 4).reshape(t, D))
        eye8 = jnp.eye(LANE // S5_GROUP, dtype=F32)
        gw = s5_glu_w[l].reshape(D // LANE, LANE // S5_GROUP, S5_GROUP, S5_GROUP)
        glu_tiles = jnp.einsum("igjk,gh->igjhk", gw, eye8).reshape(D // LANE, LANE, LANE).astype(BF16)
        o_s5 = _s5_post(ycore, p16, row(s5_d[l]), glu_tiles, row(s5_glu_b[l]), tm)

        h = _merge(h, o_hg, o_rw, o_s5, p16, w_branch[l].reshape(3, D, D).astype(BF16),
                   w_out[l].astype(BF16), tm)
        h = _ffn(h, row(ffn_norm[l]), ffn_w_gate[l].astype(BF16), ffn_w_up[l].astype(BF16),
                 ffn_w_down[l].astype(BF16), row(final_norm), l == depth - 1, tm)
    return h.reshape(batch, seq, D)
```

```python
import functools
import math

import numpy as np
import jax
import jax.numpy as jnp
from jax import lax
from jax.experimental import pallas as pl
from jax.experimental.pallas import tpu as pltpu

F32 = jnp.float32
BF16 = jnp.bfloat16

D = 1024
DEPTH = 4
NORM_EPS = 1e-6
CHUNK = 64
LANE = 128
HG_HEADS = 8
RW_HEAD = 64
RW_GN_EPS = 64e-5
LORA_W, LORA_A, LORA_G, LORA_V = 64, 64, 160, 32
S5_GROUPS, S5_GROUP, S5_STATE = 64, 16, 64
FFN_HIDDEN = 2816
FFN_CHUNK = 256

C16_Q, C16_I, C16_OG, C16_R, C16_K, C16_V, C16_S5, C16_GATE = 0, 1, 2, 3, 4, 5, 6, 7
C32_F = 0
C32_LORA = 2
LOG2E = 1.4426950408889634
VMEM_LIMIT = 56 * 1024 * 1024


def _cparams(*sem):
    return pltpu.CompilerParams(dimension_semantics=sem, vmem_limit_bytes=VMEM_LIMIT)


def _split3(x):
    hi = x.astype(BF16)
    r1 = x - hi.astype(F32)
    mid = r1.astype(BF16)
    lo = (r1 - mid.astype(F32)).astype(BF16)
    return hi, mid, lo


def _dot(a, b):
    return jnp.dot(a, b, preferred_element_type=F32)


def _dot_nt(a, b):
    return lax.dot_general(a, b, (((1,), (1,)), ((), ())), preferred_element_type=F32)


def _exact_dot01(m3, x):
    hi, mid, lo = _split3(x)
    return _dot(m3, jnp.concatenate([hi, mid, lo], axis=0))


def _rms(x, g):
    ms = jnp.mean(x * x, axis=-1, keepdims=True)
    return x * lax.rsqrt(ms + NORM_EPS) * g


def _sigmoid(x):
    return 1.0 / (1.0 + jnp.exp(-x))


def _silu(x):
    return x * _sigmoid(x)


INPROJ_TN = 1024


def _inproj_body(x_ref, g_ref, w16_ref, w32_ref, o16_ref, o32_ref, ut_ref, xn_ref):
    xn_ref[...] = _rms(x_ref[...], g_ref[...]).astype(BF16)
    for j in range(w16_ref.shape[1] // INPROJ_TN):
        sl = slice(j * INPROJ_TN, (j + 1) * INPROJ_TN)
        y = _dot(xn_ref[...], w16_ref[:, sl])
        o16_ref[:, sl] = y.astype(o16_ref.dtype)
        if j == C16_S5:
            for c in range(ut_ref.shape[0]):
                ut_ref[c] = y[c * CHUNK:(c + 1) * CHUNK, :].T.astype(ut_ref.dtype)
    for j in range(w32_ref.shape[1] // (4 * LANE)):
        sl = slice(j * 4 * LANE, (j + 1) * 4 * LANE)
        o32_ref[:, sl] = _dot(xn_ref[...], w32_ref[:, sl])


def _inproj(h, g, w16, w32, tm):
    t = h.shape[0]
    n16, n32 = w16.shape[1], w32.shape[1]
    row = lambda i: (i, 0)
    fixed = lambda i: (0, 0)
    resident = lambda shape: pl.BlockSpec(shape, fixed, pipeline_mode=pl.Buffered(1))
    return pl.pallas_call(
        _inproj_body,
        grid=(t // tm,),
        in_specs=[pl.BlockSpec((tm, D), row), pl.BlockSpec((1, D), fixed),
                  resident((D, n16)), resident((D, n32))],
        out_specs=[pl.BlockSpec((tm, n16), row), pl.BlockSpec((tm, n32), row),
                   pl.BlockSpec((tm // CHUNK, D, CHUNK), lambda i: (i, 0, 0))],
        out_shape=[jax.ShapeDtypeStruct((t, n16), BF16), jax.ShapeDtypeStruct((t, n32), F32),
                   jax.ShapeDtypeStruct((t // CHUNK, D, CHUNK), BF16)],
        scratch_shapes=[pltpu.VMEM((tm, D), BF16)],
        compiler_params=_cparams("parallel"),
        name="inproj",
    )(h, g, w16, w32)


def _merge_body(h_ref, ohg_ref, orw_ref, os5_ref, g0_ref, g1_ref, g2_ref, wb_ref, wo_ref, o_ref):
    m = _sigmoid(g0_ref[...].astype(F32)) * _dot(ohg_ref[...], wb_ref[0])
    m = m + _sigmoid(g1_ref[...].astype(F32)) * _dot(orw_ref[...], wb_ref[1])
    m = m + _sigmoid(g2_ref[...].astype(F32)) * _dot(os5_ref[...], wb_ref[2])
    o_ref[...] = h_ref[...] + _dot(m.astype(BF16), wo_ref[...])


def _merge(h, o_hg, o_rw, o_s5, proj, wb, wo, tm):
    t = h.shape[0]
    row = lambda i: (i, 0)
    return pl.pallas_call(
        _merge_body,
        grid=(t // tm,),
        in_specs=[
            pl.BlockSpec((tm, D), row),
            pl.BlockSpec((tm, D), row),
            pl.BlockSpec((tm, D), row),
            pl.BlockSpec((tm, D), row),
            pl.BlockSpec((tm, D), lambda i: (i, C16_GATE)),
            pl.BlockSpec((tm, D), lambda i: (i, C16_GATE + 1)),
            pl.BlockSpec((tm, D), lambda i: (i, C16_GATE + 2)),
            pl.BlockSpec((3, D, D), lambda i: (0, 0, 0)),
            pl.BlockSpec((D, D), lambda i: (0, 0)),
        ],
        out_specs=pl.BlockSpec((tm, D), row),
        out_shape=jax.ShapeDtypeStruct((t, D), F32),
        compiler_params=_cparams("parallel"),
        name="merge",
    )(h, o_hg, o_rw, o_s5, proj, proj, proj, wb, wo)


def _ffn_body(final, h_ref, g_ref, wg_ref, wu_ref, wd_ref, fg_ref, o_ref, xn_ref, acc_ref):
    x = h_ref[...]
    xn_ref[...] = _rms(x, g_ref[...]).astype(BF16)
    for c in range(FFN_HIDDEN // FFN_CHUNK):
        sl = slice(c * FFN_CHUNK, (c + 1) * FFN_CHUNK)
        a = _dot(xn_ref[...], wg_ref[:, sl])
        b = _dot(xn_ref[...], wu_ref[:, sl])
        part = _dot((_silu(a) * b).astype(BF16), wd_ref[sl, :])
        if c == 0:
            acc_ref[...] = part
        else:
            acc_ref[...] += part
    out = x + acc_ref[...]
    if final:
        out = _rms(out, fg_ref[...])
    o_ref[...] = out


def _ffn(h, g, wg, wu, wd, fg, final, tm):
    t = h.shape[0]
    row = lambda i: (i, 0)
    fixed = lambda i: (0, 0)
    return pl.pallas_call(
        functools.partial(_ffn_body, final),
        grid=(t // tm,),
        in_specs=[
            pl.BlockSpec((tm, D), row),
            pl.BlockSpec((1, D), fixed),
            pl.BlockSpec((D, FFN_HIDDEN), fixed),
            pl.BlockSpec((D, FFN_HIDDEN), fixed),
            pl.BlockSpec((FFN_HIDDEN, D), fixed),
            pl.BlockSpec((1, D), fixed),
        ],
        out_specs=pl.BlockSpec((tm, D), row),
        out_shape=jax.ShapeDtypeStruct((t, D), F32),
        scratch_shapes=[pltpu.VMEM((tm, D), BF16), pltpu.VMEM((tm, D), F32)],
        compiler_params=_cparams("parallel"),
        name="ffn",
    )(h, g, wg, wu, wd, fg)


HG_LEVELS = (32, 16, 8, 4, 2, 1)


def _hg_dmat():
    t = np.arange(CHUNK)
    tri = (t[None, :] <= t[:, None]).astype(np.float32)
    blocks = [tri]
    for m in HG_LEVELS:
        r = (t // (2 * m)) * (2 * m) + m - 1
        blocks.append(tri - tri[r])
    d = np.concatenate(blocks, axis=0)
    return jnp.asarray(np.concatenate([d, d, d], axis=1), BF16)


def _hg_body(layer, q_ref, f_ref, i_ref, og_ref, lbl_ref, on_ref, dm_ref, o_ref, st_ref, d_ref):
    tb = q_ref.shape[0]

    @pl.when(pl.program_id(1) == 0)
    def _():
        st_ref[...] = jnp.zeros_like(st_ref)

    lg = lbl_ref[...]
    e = jnp.exp(lg - jnp.max(lg, axis=0, keepdims=True))
    sm = e / jnp.sum(e, axis=0, keepdims=True)
    lb = jnp.zeros((1, D), F32)
    for l in range(1, layer + 1):
        lb = lb + sm[l:l + 1, :]
    lb = jnp.maximum(lb, 0.0)
    log_lb = jnp.log(lb)
    log_1mlb = jnp.log1p(-lb)
    one_mlb = 1.0 - lb

    ti = lax.broadcasted_iota(jnp.int32, (CHUNK, CHUNK), 0)
    si = lax.broadcasted_iota(jnp.int32, (CHUNK, CHUNK), 1)
    xr = ti ^ si
    lower = ti > si
    pair_masks = [lower & (xr >= m) & (xr < 2 * m) for m in HG_LEVELS]
    diag_mask = ti == si
    trow = lax.broadcasted_iota(jnp.int32, (CHUNK, LANE), 0)
    second = [(trow & m) != 0 for m in HG_LEVELS]

    def chunk_step(c, carry):
        r0 = pl.multiple_of(c * CHUNK, CHUNK)
        rows = pl.ds(r0, CHUNK)
        z = f_ref[rows, :]
        ez = jnp.exp(-jnp.abs(z))
        l1p = jnp.log1p(ez)
        lsig = jnp.minimum(z, 0.0) - l1p
        cterm = log_1mlb + lsig
        mx = jnp.maximum(log_lb, cterm)
        logf = mx + jnp.log1p(jnp.exp(-jnp.abs(log_lb - cterm)))
        d_ref[...] = _exact_dot01(dm_ref[...], logf * LOG2E)
        kk_all = one_mlb * jnp.where(z >= 0.0, ez, 1.0) / (1.0 + ez)
        heads = range(HG_HEADS)
        lns = [slice(h * LANE, (h + 1) * LANE) for h in heads]
        q = [_silu(q_ref[rows, lns[h]].astype(F32)) for h in heads]
        k = [kk_all[:, lns[h]] for h in heads]
        s0 = [_dot_nt(q[h].astype(BF16), k[h].astype(BF16)) for h in heads]
        scores = [jnp.where(diag_mask, s0[h], 0.0) for h in heads]
        for li in range(len(HG_LEVELS)):
            x = [(jnp.where(second[li], q[h], k[h])
                  * jnp.exp2(-jnp.abs(d_ref[(li + 1) * CHUNK:(li + 2) * CHUNK, lns[h]]))).astype(BF16)
                 for h in heads]
            sl = [_dot_nt(x[h], x[h]) for h in heads]
            scores = [scores[h] + jnp.where(pair_masks[li], sl[h], 0.0) for h in heads]
        o = []
        for h in heads:
            b = d_ref[0:CHUNK, lns[h]]
            qe = (q[h] * jnp.exp2(b)).astype(BF16)
            o.append(_dot(scores[h].astype(BF16), i_ref[rows, lns[h]].astype(BF16))
                     + _dot_nt(qe, st_ref[h].astype(BF16)))
        for h in heads:
            b = d_ref[0:CHUNK, lns[h]]
            b_last = b[CHUNK - 1:CHUNK, :]
            kd = (k[h] * jnp.exp2(b_last - b)).astype(BF16)
            vt = i_ref[rows, lns[h]].astype(F32).T.astype(BF16)
            st_ref[h] = st_ref[h] * jnp.exp2(b_last) + _dot(vt, kd)
        for h in heads:
            oh = o[h] * lax.rsqrt(jnp.mean(o[h] * o[h], axis=-1, keepdims=True) + NORM_EPS)
            oh = oh * on_ref[:, lns[h]] * _silu(og_ref[rows, lns[h]].astype(F32))
            o_ref[rows, lns[h]] = oh.astype(o_ref.dtype)
        return carry

    lax.fori_loop(0, tb // CHUNK, chunk_step, 0)


def _hgrn2(p16, p32, lb_logits, onorm, layer, batch, seq, tb):
    t = p16.shape[0]
    nb = seq // tb
    col = lambda j: (lambda b, s: (b * nb + s, j))
    fixed = lambda b, s: (0, 0)
    return pl.pallas_call(
        functools.partial(_hg_body, layer),
        grid=(batch, nb),
        in_specs=[
            pl.BlockSpec((tb, D), col(C16_Q)),
            pl.BlockSpec((tb, D), col(C32_F)),
            pl.BlockSpec((tb, D), col(C16_I)),
            pl.BlockSpec((tb, D), col(C16_OG)),
            pl.BlockSpec(lb_logits.shape, fixed),
            pl.BlockSpec((1, D), fixed),
            pl.BlockSpec((7 * CHUNK, 3 * CHUNK), fixed),
        ],
        out_specs=pl.BlockSpec((tb, D), lambda b, s: (b * nb + s, 0)),
        out_shape=jax.ShapeDtypeStruct((t, D), BF16),
        scratch_shapes=[pltpu.VMEM((HG_HEADS, LANE, LANE), F32),
                        pltpu.VMEM((7 * CHUNK, D), F32)],
        compiler_params=_cparams("parallel", "arbitrary"),
        name="hgrn2",
    )(p16, p32, p16, p16, lb_logits, onorm, _hg_dmat())


S5_ROW = CHUNK * S5_GROUP


def _s5_consts():
    tj = np.arange(S5_ROW)
    e_tj = (tj[None, :] % CHUNK == np.arange(CHUNK)[:, None]).astype(np.float32)
    f_tj = (tj[None, :] // CHUNK == np.arange(S5_GROUP)[:, None]).astype(np.float32)
    return (jnp.asarray(e_tj, BF16), jnp.asarray(f_tj, BF16),
            jnp.asarray(e_tj.T, BF16), jnp.asarray(f_tj.T, BF16))


def _right01(x, e):
    hi, mid, lo = _split3(x)
    return _dot(hi, e) + _dot(mid, e) + _dot(lo, e)


def _left01(e, x):
    hi, mid, lo = _split3(x)
    return _dot(e, hi) + _dot(e, mid) + _dot(e, lo)


def _dot_f32(a, b):
    ah, am, al = _split3(a)
    bh, bm, bl = _split3(b)
    return (_dot(ah, bh) + _dot(ah, bm) + _dot(am, bh)
            + _dot(ah, bl) + _dot(al, bh) + _dot(am, bm))


def _s5_prep_body(lrr_ref, lir_ref, lrc_ref, lic_ref, ls_ref, brt_ref, bit_ref, crt_ref, cit_ref,
                  etj_ref, ftj_ref, esh_ref, fsh_ref,
                  wt_ref, wi_ref, wis_ref, wo_ref, a1_ref, a2_ref):
    dt = jnp.exp(ls_ref[...])
    lam_re, lam_im = lrr_ref[...], lir_ref[...]
    lr_row, li_row = lam_re * dt, lam_im * dt
    lr_col, li_col = lrc_ref[...] * dt, lic_ref[...] * dt
    re_half = lax.broadcasted_iota(jnp.int32, (1, 2 * S5_STATE), 1) < S5_STATE

    mag = jnp.exp(lr_row)
    ab_re, ab_im = mag * jnp.cos(li_row), mag * jnp.sin(li_row)
    den = lam_re * lam_re + lam_im * lam_im
    nr, ni = ab_re - 1.0, ab_im
    coef_re = (nr * lam_re + ni * lam_im) / den
    coef_im = (ni * lam_re - nr * lam_im) / den
    bb_re = coef_re * brt_ref[...] - coef_im * bit_ref[...]
    bb_im = coef_re * bit_ref[...] + coef_im * brt_ref[...]

    def powers(lr, li, n):
        m = jnp.exp(lr * n)
        return m * jnp.cos(li * n), m * jnp.sin(li * n)

    tau = lax.broadcasted_iota(jnp.int32, (1, CHUNK), 1).astype(F32)
    p0_re, p0_im = powers(lr_col, li_col, tau)
    p1_re, p1_im = powers(lr_col, li_col, tau + 1.0)
    c_re = _right01(crt_ref[...], ftj_ref[...])
    c_im = _right01(cit_ref[...], ftj_ref[...])
    p0e_re, p0e_im = _right01(p0_re, etj_ref[...]), _right01(p0_im, etj_ref[...])
    p1e_re, p1e_im = _right01(p1_re, etj_ref[...]), _right01(p1_im, etj_ref[...])

    m_stack = jnp.concatenate([c_re * p0e_re - c_im * p0e_im, c_re * p0e_im + c_im * p0e_re], axis=0)
    r0 = _dot_f32(jnp.where(re_half, bb_re, -bb_im), m_stack)
    t_lane = lax.broadcasted_iota(jnp.int32, (CHUNK, S5_ROW), 1) & (CHUNK - 1)
    s_row = lax.broadcasted_iota(jnp.int32, (CHUNK, S5_ROW), 0)
    for h in range(S5_GROUP):
        base = jnp.broadcast_to(r0[h:h + 1, :], (CHUNK, S5_ROW))
        rolled = pltpu.roll(base, 0, 1, stride=1, stride_axis=0)
        wt_ref[h * CHUNK:(h + 1) * CHUNK, :] = jnp.where(t_lane >= s_row, rolled, 0.0).astype(BF16)

    wo_ref[0:S5_STATE, :] = (c_re * p1e_re - c_im * p1e_im).astype(BF16)
    wo_ref[S5_STATE:2 * S5_STATE, :] = (-(c_re * p1e_im + c_im * p1e_re)).astype(BF16)

    back = (CHUNK - 1.0) - lax.broadcasted_iota(jnp.int32, (CHUNK, 1), 0).astype(F32)
    pt_re, pt_im = powers(lr_row, li_row, back)
    pte_re, pte_im = _left01(esh_ref[...], pt_re), _left01(esh_ref[...], pt_im)
    bbe_re, bbe_im = _left01(fsh_ref[...], bb_re), _left01(fsh_ref[...], bb_im)
    w_re = pte_re * bbe_re - pte_im * bbe_im
    w_im = pte_re * bbe_im + pte_im * bbe_re
    wi_ref[...] = jnp.where(re_half, w_re, w_im).astype(BF16)
    wis_ref[...] = jnp.where(re_half, w_im, w_re).astype(BF16)

    k = lax.broadcasted_iota(jnp.int32, (8, 1), 0)
    n = (CHUNK * (1 << k)).astype(F32)
    an_re, an_im = powers(lr_row, li_row, n)
    a1_ref[...] = an_re
    a2_ref[...] = jnp.where(re_half, -an_im, an_im)


def _s5_prep(lam_re, lam_im, log_step, b_re, b_im, c_re, c_im):
    nl = lam_re.shape[0]
    ns = 2 * S5_STATE
    g4 = lambda r, c: pl.BlockSpec((None, None, r, c), lambda l, g: (l, g, 0, 0))
    cst = lambda r, c: pl.BlockSpec((r, c), lambda l, g: (0, 0))
    sds = lambda r, c, dt: jax.ShapeDtypeStruct((nl, S5_GROUPS, r, c), dt)
    twice = lambda a: jnp.concatenate([a, a], axis=-1)
    e_tj, f_tj, e_sh, f_sh = _s5_consts()
    return pl.pallas_call(
        _s5_prep_body,
        grid=(nl, S5_GROUPS),
        in_specs=[g4(1, ns), g4(1, ns), g4(S5_STATE, 1), g4(S5_STATE, 1), g4(1, 1),
                  g4(S5_GROUP, ns), g4(S5_GROUP, ns), g4(S5_STATE, S5_GROUP), g4(S5_STATE, S5_GROUP),
                  cst(CHUNK, S5_ROW), cst(S5_GROUP, S5_ROW), cst(S5_ROW, CHUNK), cst(S5_ROW, S5_GROUP)],
        out_specs=[g4(S5_ROW, S5_ROW), g4(S5_ROW, ns), g4(S5_ROW, ns), g4(ns, S5_ROW), g4(8, ns), g4(8, ns)],
        out_shape=[sds(S5_ROW, S5_ROW, BF16), sds(S5_ROW, ns, BF16), sds(S5_ROW, ns, BF16),
                   sds(ns, S5_ROW, BF16), sds(8, ns, F32), sds(8, ns, F32)],
        compiler_params=_cparams("parallel", "parallel"),
        name="s5_prep",
    )(twice(lam_re)[:, :, None, :], twice(lam_im)[:, :, None, :], lam_re[:, :, :, None], lam_im[:, :, :, None],
      log_step[:, :, None, None],
      twice(jnp.swapaxes(b_re, 2, 3)), twice(jnp.swapaxes(b_im, 2, 3)),
      jnp.swapaxes(c_re, 2, 3), jnp.swapaxes(c_im, 2, 3),
      e_tj, f_tj, e_sh, f_sh)


def _s5_body(nc, u_ref, wt_ref, wi_ref, wis_ref, wo_ref, a1_ref, a2_ref, y_ref):
    u = u_ref[...]
    m = u.shape[0]
    x = _dot(u, wi_ref[...])
    xs = _dot(u, wis_ref[...])
    chunk = lax.broadcasted_iota(jnp.int32, (m, 2 * S5_STATE), 0) % nc

    def shifted(z, n):
        return jnp.where(chunk >= n, pltpu.roll(z, n, 0), 0.0)

    k = 0
    while (1 << k) < nc:
        a1, a2 = a1_ref[k:k + 1, :], a2_ref[k:k + 1, :]
        sx, sxs = shifted(x, 1 << k), shifted(xs, 1 << k)
        x, xs = x + a1 * sx + a2 * sxs, xs + a1 * sxs - a2 * sx
        k += 1
    y = _dot(u, wt_ref[...]) + _dot(shifted(x, 1).astype(BF16), wo_ref[...])
    y_ref[...] = y.astype(y_ref.dtype)


def _s5_core(u_t, ops, layer, nc):
    wt, wi, wis, wo, a1, a2 = ops
    m = u_t.shape[0]
    ns = 2 * S5_STATE
    grp = pl.BlockSpec((m, S5_ROW), lambda g: (0, g))
    lg = lambda r, c: pl.BlockSpec((None, None, r, c), lambda g: (layer, g, 0, 0))
    return pl.pallas_call(
        functools.partial(_s5_body, nc),
        grid=(S5_GROUPS,),
        in_specs=[grp, lg(S5_ROW, S5_ROW), lg(S5_ROW, ns), lg(S5_ROW, ns),
                  lg(ns, S5_ROW), lg(8, ns), lg(8, ns)],
        out_specs=grp,
        out_shape=jax.ShapeDtypeStruct((m, S5_GROUPS * S5_ROW), BF16),
        compiler_params=_cparams("parallel"),
        name="s5_core",
    )(u_t, wt, wi, wis, wo, a1, a2)


def _s5_post_body(y_ref, u_ref, d_ref, gw_ref, gb_ref, o_ref, yt_ref):
    for c in range(y_ref.shape[0]):
        yt_ref[c * CHUNK:(c + 1) * CHUNK, :] = y_ref[c].astype(F32).T
    for i in range(D // LANE):
        ln = slice(i * LANE, (i + 1) * LANE)
        y = yt_ref[:, ln] + d_ref[:, ln] * u_ref[:, ln].astype(F32)
        zg = 0.5 * y * (1.0 + jnp.tanh(math.sqrt(2.0 / math.pi) * (y + 0.044715 * (y * y * y))))
        gate = _dot(zg.astype(BF16), gw_ref[i]) + gb_ref[:, ln]
        o_ref[:, ln] = (zg * _sigmoid(gate)).astype(o_ref.dtype)


def _s5_post(ycore, proj, d, glu_tiles, glu_b, tm):
    t = proj.shape[0]
    row = lambda i: (i, 0)
    fixed = lambda i: (0, 0)
    return pl.pallas_call(
        _s5_post_body,
        grid=(t // tm,),
        in_specs=[pl.BlockSpec((tm // CHUNK, D, CHUNK), lambda i: (i, 0, 0)),
                  pl.BlockSpec((tm, D), lambda i: (i, C16_S5)),
                  pl.BlockSpec((1, D), fixed),
                  pl.BlockSpec((D // LANE, LANE, LANE), lambda i: (0, 0, 0)),
                  pl.BlockSpec((1, D), fixed)],
        out_specs=pl.BlockSpec((tm, D), row),
        out_shape=jax.ShapeDtypeStruct((t, D), BF16),
        scratch_shapes=[pltpu.VMEM((tm, D), F32)],
        compiler_params=_cparams("parallel"),
        name="s5_post",
    )(ycore, proj, d, glu_tiles, glu_b)


def _ones_bd():
    i = np.arange(LANE)
    return jnp.asarray((i[:, None] // RW_HEAD == i[None, :] // RW_HEAD).astype(np.float32), BF16)


def _segsum(x, ones_bd):
    hi = x.astype(BF16)
    lo = (x - hi.astype(F32)).astype(BF16)
    return _dot(hi, ones_bd) + _dot(lo, ones_bd)


def _rw_prep_body(first, r_ref, k_ref, v_ref, lo_ref, vf_ref, mu_ref, mul_ref, w0_ref, w2_ref, a0_ref, a2_ref,
                  g2_ref, v0_ref, v1_ref, v2_ref, kk_ref, ka_ref, ones_ref,
                  ro_ref, lw_ref, ko_ref, vo_ref, kko_ref, bo_ref, go_ref, cr_ref, cl_ref):
    tb = r_ref.shape[0]

    @pl.when(pl.program_id(1) == 0)
    def _():
        cr_ref[...] = jnp.zeros_like(cr_ref)
        cl_ref[...] = jnp.zeros_like(cl_ref)

    def mix(z, carry, mu):
        row = lax.broadcasted_iota(jnp.int32, z.shape, 0)
        zs = jnp.where(row == 0, carry, pltpu.roll(z, 1, 0))
        return z + mu * (zs - z)

    def softplus(x):
        return jnp.maximum(x, 0.0) + jnp.log1p(jnp.exp(-jnp.abs(x)))

    zl = lo_ref[...]
    lora = mix(zl, cl_ref[0:1, :], mul_ref[...])
    cl_ref[0:1, :] = zl[tb - 1:tb, :]
    wd = jnp.tanh(lora[:, 0:LANE]).astype(BF16)
    ad = lora[:, LANE:2 * LANE].astype(BF16)
    gd = _sigmoid(lora[:, 2 * LANE:4 * LANE]).astype(BF16)
    w_log = -softplus(-(w0_ref[...] + _dot(wd, w2_ref[...]))) - 0.5
    lw_ref[...] = -jnp.exp(w_log)
    a = _sigmoid(a0_ref[...] + _dot(ad, a2_ref[...]))
    go_ref[...] = _dot(gd, g2_ref[...]).astype(go_ref.dtype)

    zr = r_ref[...].astype(F32)
    ro_ref[...] = mix(zr, cr_ref[0:1, :], mu_ref[0:1, :]).astype(ro_ref.dtype)
    cr_ref[0:1, :] = zr[tb - 1:tb, :]
    zk = k_ref[...].astype(F32)
    k = mix(zk, cr_ref[1:2, :], mu_ref[1:2, :])
    cr_ref[1:2, :] = zk[tb - 1:tb, :]
    zv = v_ref[...].astype(F32)
    v = mix(zv, cr_ref[2:3, :], mu_ref[2:3, :])
    cr_ref[2:3, :] = zv[tb - 1:tb, :]
    if not first:
        lowr = _dot(v.astype(BF16), v1_ref[...]).astype(BF16)
        v = v + (vf_ref[...].astype(F32) - v) * _sigmoid(v0_ref[...] + _dot(lowr, v2_ref[...]))
    vo_ref[...] = v.astype(vo_ref.dtype)

    kk = k * kk_ref[...]
    ss = jnp.concatenate([_segsum(kk[:, i * LANE:(i + 1) * LANE] * kk[:, i * LANE:(i + 1) * LANE], ones_ref[...])
                          for i in range(D // LANE)], axis=1)
    kk = kk * lax.rsqrt(jnp.maximum(ss, 1e-24))
    kko_ref[...] = kk.astype(kko_ref.dtype)
    bo_ref[...] = (kk * a).astype(bo_ref.dtype)
    ko_ref[...] = (k * (1.0 + (a - 1.0) * ka_ref[...])).astype(ko_ref.dtype)


def _rw_prep(p16, p32, v_first, p, first, batch, seq, tb):
    t = p16.shape[0]
    nb = seq // tb
    col = lambda j: (lambda b, s: (b * nb + s, j))
    fixed = lambda b, s: (0, 0)
    vec = pl.BlockSpec((1, D), fixed)
    rowblk = pl.BlockSpec((tb, D), lambda b, s: (b * nb + s, 0))
    out = jax.ShapeDtypeStruct((t, D), BF16)
    return pl.pallas_call(
        functools.partial(_rw_prep_body, first),
        grid=(batch, nb),
        in_specs=[pl.BlockSpec((tb, D), col(C16_R)), pl.BlockSpec((tb, D), col(C16_K)),
                  pl.BlockSpec((tb, D), col(C16_V)),
                  pl.BlockSpec((tb, 4 * LANE), col(C32_LORA)),
                  rowblk,
                  pl.BlockSpec((3, D), fixed), pl.BlockSpec((1, 4 * LANE), fixed),
                  vec, pl.BlockSpec((LANE, D), fixed), vec, pl.BlockSpec((LANE, D), fixed),
                  pl.BlockSpec((2 * LANE, D), fixed),
                  vec, pl.BlockSpec((D, LANE), fixed), pl.BlockSpec((LANE, D), fixed),
                  vec, vec, pl.BlockSpec((LANE, LANE), fixed)],
        out_specs=[rowblk] * 7,
        out_shape=[out, jax.ShapeDtypeStruct((t, D), F32)] + [out] * 5,
        scratch_shapes=[pltpu.VMEM((8, D), F32), pltpu.VMEM((8, 4 * LANE), F32)],
        compiler_params=_cparams("parallel", "arbitrary"),
        name="rw_prep",
    )(p16, p16, p16, p32, v_first, p["mu"], p["mu_lora"], p["w0"], p["w2"], p["a0"], p["a2"], p["g2"],
      p["v0"], p["v1"], p["v2"], p["k_k"], p["k_a"], _ones_bd())


RW_LEVELS = (2, 4, 8, 16, 32)


def _rwc_body(r_ref, lw_ref, k_ref, v_ref, kk_ref, b_ref, g_ref, rk_ref, lnw_ref, lnb_ref, tri_ref, ones_ref,
              o_ref, ht_ref, lg_ref):
    nbat, tb = r_ref.shape[0], r_ref.shape[1]
    ntile = D // LANE

    @pl.when(pl.program_id(1) == 0)
    def _():
        ht_ref[...] = jnp.zeros_like(ht_ref)

    t64 = lax.broadcasted_iota(jnp.int32, (CHUNK, LANE), 0)
    l64 = lax.broadcasted_iota(jnp.int32, (CHUNK, LANE), 1)
    s64 = l64 & (RW_HEAD - 1)
    strict = t64 > s64
    incl = t64 >= s64
    eye = t64 == s64
    xr = t64 ^ s64
    lvl1 = strict & (xr == 1)
    lvl = [strict & (xr >= m) & (xr < 2 * m) for m in RW_LEVELS]
    head0 = l64 < RW_HEAD
    r128 = lax.broadcasted_iota(jnp.int32, (LANE, LANE), 0)
    l128 = lax.broadcasted_iota(jnp.int32, (LANE, LANE), 1)
    same_head = (r128 >> 6) == (l128 >> 6)

    def stack(x):
        return jnp.concatenate([jnp.where(head0, x, 0.0), jnp.where(head0, 0.0, x)], axis=0).astype(BF16)

    def chunk_step(c, carry):
        rows = pl.ds(pl.multiple_of(c * CHUNK, CHUNK), CHUNK)
        for bi in range(nbat):
            lg_ref[bi] = _exact_dot01(tri_ref[...], lw_ref[bi, rows, :])
        probs = [(bi, j) for bi in range(nbat) for j in range(ntile)]
        n = range(len(probs))

        def ld(ref, p):
            return ref[probs[p][0], rows, probs[p][1] * LANE:(probs[p][1] + 1) * LANE].astype(F32)

        def lgp(p):
            return lg_ref[probs[p][0], :, probs[p][1] * LANE:(probs[p][1] + 1) * LANE]

        a_t, r_t, a_ak, a_rk, l_ab, a_rb, x = [], [], [], [], [], [], []
        for p in n:
            lg = lgp(p)
            e_neg = jnp.exp(-lg)
            a_t.append(-ld(kk_ref, p) * jnp.exp(lg - ld(lw_ref, p)))
            r_t.append(ld(r_ref, p) * jnp.exp(lg))
            lhs = jnp.concatenate([a_t[p], r_t[p]], axis=0).astype(BF16)
            kb_neg = jnp.concatenate([stack(ld(k_ref, p) * e_neg), stack(ld(b_ref, p) * e_neg)], axis=0)
            a_kb = _dot_nt(lhs, kb_neg)
            a_ak.append(jnp.where(strict, a_kb[0:CHUNK, 0:LANE], 0.0))
            a_rk.append(jnp.where(incl, a_kb[CHUNK:2 * CHUNK, 0:LANE], 0.0))
            l_ab.append(jnp.where(strict, a_kb[0:CHUNK, LANE:2 * LANE], 0.0))
            a_rb.append(jnp.where(incl, a_kb[CHUNK:2 * CHUNK, LANE:2 * LANE], 0.0))
            x.append(jnp.where(eye, 1.0, 0.0) + jnp.where(lvl1, l_ab[p], 0.0))

        for li in range(len(RW_LEVELS)):
            z = [_dot(x[p].astype(BF16), stack(jnp.where(lvl[li], l_ab[p], 0.0))) for p in n]
            x = [x[p] + _dot(z[p].astype(BF16), stack(x[p])) for p in n]

        v_st = [stack(ld(v_ref, p)) for p in n]
        mv = [_dot(a_ak[p].astype(BF16), v_st[p]) for p in n]
        u0w = [_dot(x[p].astype(BF16), jnp.concatenate([stack(mv[p]), stack(a_t[p])], axis=1)) for p in n]
        htb = [ht_ref[p].astype(BF16) for p in n]
        u = [u0w[p][:, 0:LANE] + _dot_nt(u0w[p][:, LANE:2 * LANE].astype(BF16), htb[p]) for p in n]
        y = [_dot_nt(r_t[p].astype(BF16), htb[p]) + _dot(a_rk[p].astype(BF16), v_st[p])
             + _dot(a_rb[p].astype(BF16), stack(u[p])) for p in n]
        for p in n:
            lg = lgp(p)
            lg_end = lg[CHUNK - 1:CHUNK, :]
            e_end = jnp.exp(lg_end - lg)
            vu = jnp.concatenate([ld(v_ref, p), u[p]], axis=0)
            kb = jnp.concatenate([ld(k_ref, p) * e_end, ld(b_ref, p) * e_end], axis=0).astype(BF16)
            upd = _dot(vu.T.astype(BF16), kb)
            ht_ref[p] = ht_ref[p] * jnp.exp(lg_end) + jnp.where(same_head, upd, 0.0)

        ones = ones_ref[...]
        sl = lambda a, p: a[p * CHUNK:(p + 1) * CHUNK]
        mean = _segsum(jnp.concatenate(y, axis=0), ones) * (1.0 / RW_HEAD)
        yc = [y[p] - sl(mean, p) for p in n]
        var = _segsum(jnp.concatenate([c * c for c in yc], axis=0), ones) * (1.0 / RW_HEAD)
        rk = [ld(r_ref, p) * ld(k_ref, p) * rk_ref[:, probs[p][1] * LANE:(probs[p][1] + 1) * LANE] for p in n]
        bonus = _segsum(jnp.concatenate(rk, axis=0), ones)
        for p in n:
            bi, j = probs[p]
            ln = slice(j * LANE, (j + 1) * LANE)
            yn = yc[p] * lax.rsqrt(sl(var, p) + RW_GN_EPS) * lnw_ref[:, ln] + lnb_ref[:, ln]
            o_ref[bi, rows, ln] = ((yn + sl(bonus, p) * ld(v_ref, p)) * ld(g_ref, p)).astype(o_ref.dtype)
        return carry

    lax.fori_loop(0, tb // CHUNK, chunk_step, 0)


def _rwc_core(feats, r_k, ln_w, ln_b, batch, seq, nbat, tb):
    t = feats[0].shape[0]
    fixed = lambda b, s: (0, 0)
    vec = pl.BlockSpec((1, D), fixed)
    blk = pl.BlockSpec((nbat, tb, D), lambda b, s: (b, s, 0))
    tri = np.tril(np.ones((CHUNK, CHUNK), np.float32))
    tri3 = jnp.asarray(np.concatenate([tri, tri, tri], axis=1), BF16)
    out = pl.pallas_call(
        _rwc_body,
        grid=(batch // nbat, seq // tb),
        in_specs=[blk] * 7 + [vec, vec, vec, pl.BlockSpec((CHUNK, 3 * CHUNK), fixed),
                              pl.BlockSpec((LANE, LANE), fixed)],
        out_specs=blk,
        out_shape=jax.ShapeDtypeStruct((batch, seq, D), BF16),
        scratch_shapes=[pltpu.VMEM((nbat * (D // LANE), LANE, LANE), F32), pltpu.VMEM((nbat, CHUNK, D), F32)],
        compiler_params=_cparams("parallel", "arbitrary"),
        name="rw_core",
    )(*[f.reshape(batch, seq, D) for f in feats], r_k, ln_w, ln_b, tri3, _ones_bd())
    return out.reshape(t, D)


def _pack_w_in(w):
    o_lora = 4 * D + 3 * D
    o_a, o_g = o_lora + LORA_W, o_lora + LORA_W + LORA_A
    o_s5 = o_g + LORA_G
    o_gate = o_s5 + D
    z = lambda n: jnp.zeros((D, n), w.dtype)
    w16 = jnp.concatenate([w[:, 0:D], w[:, 2 * D:o_lora], w[:, o_s5:o_gate + 3 * D]], axis=1)
    w32 = jnp.concatenate([w[:, D:2 * D], w[:, o_lora:o_a], z(LANE - LORA_W), w[:, o_a:o_g], z(LANE - LORA_A),
                           w[:, o_g:o_s5], z(2 * LANE - LORA_G)], axis=1)
    return w16, w32


def _pad_rows(w, n):
    return jnp.concatenate([w, jnp.zeros((n - w.shape[0], w.shape[1]), w.dtype)], axis=0)


def _pad_lanes(v, n):
    return jnp.concatenate([v, jnp.zeros((n - v.shape[0],), v.dtype)])


def kernel(x, mix_norm, w_in, hg_lb_logits, hg_onorm, rw_shift_mu, rw_w0, rw_w2, rw_a0, rw_a2, rw_g2, rw_v0, rw_v1, rw_v2, rw_k_k, rw_k_a, rw_r_k, rw_ln_w, rw_ln_b, s5_lambda_re, s5_lambda_im, s5_log_step, s5_b_re, s5_b_im, s5_c_re, s5_c_im, s5_d, s5_glu_w, s5_glu_b, w_branch, w_out, ffn_norm, ffn_w_gate, ffn_w_up, ffn_w_down, final_norm):
    batch, seq, _ = x.shape
    t = batch * seq
    nc = seq // CHUNK
    depth = w_in.shape[0]
    tm = min(512, t)
    tb = min(256, seq)
    h = x.reshape(t, D)
    row = lambda v: v.reshape(1, -1)

    s5_ops = _s5_prep(s5_lambda_re, s5_lambda_im, s5_log_step, s5_b_re, s5_b_im, s5_c_re, s5_c_im)
    w_in_bf = w_in.astype(BF16)
    v_first = None
    for l in range(depth):
        w16, w32 = _pack_w_in(w_in_bf[l])
        p16, p32, u_t = _inproj(h, row(mix_norm[l]), w16, w32, min(256, t))

        o_hg = _hgrn2(p16, p32, hg_lb_logits, row(hg_onorm[l]), l, batch, seq, tb)

        mu = rw_shift_mu[l]
        o1, o2, o3 = 3 * D + LORA_W, 3 * D + LORA_W + LORA_A, 3 * D + LORA_W + LORA_A + LORA_G
        first = l == 0
        rp = {
            "mu": mu[0:3 * D].reshape(3, D),
            "mu_lora": row(jnp.concatenate([_pad_lanes(mu[3 * D:o1], LANE), _pad_lanes(mu[o1:o2], LANE),
                                            _pad_lanes(mu[o2:o3], 2 * LANE)])),
            "w0": row(rw_w0[l]), "w2": _pad_rows(rw_w2[l], LANE).astype(BF16),
            "a0": row(rw_a0[l]), "a2": _pad_rows(rw_a2[l], LANE).astype(BF16),
            "g2": _pad_rows(rw_g2[l], 2 * LANE).astype(BF16),
            "v0": row(rw_v0[0 if first else l - 1]),
            "v1": jnp.concatenate([rw_v1[0 if first else l - 1],
                                   jnp.zeros((D, LANE - LORA_V), F32)], axis=1).astype(BF16),
            "v2": _pad_rows(rw_v2[0 if first else l - 1], LANE).astype(BF16),
            "k_k": row(rw_k_k[l]), "k_a": row(rw_k_a[l]),
        }
        feats = _rw_prep(p16, p32, p16 if first else v_first, rp, first, batch, seq, tb)
        if first:
            v_first = feats[3]
        o_rw = _rwc_core(feats, row(rw_r_k[l]), row(rw_ln_w[l]), row(rw_ln_b[l]), batch, seq,
                         math.gcd(batch, 4), min(128, seq))

        y_t = _s5_core(u_t.reshape(t // CHUNK, S5_GROUPS * S5_ROW), s5_ops, l, nc)
        ycore = y_t.reshape(t // CHUNK, D, CHUNK)
        eye8 = jnp.eye(LANE // S5_GROUP, dtype=F32)
        gw = s5_glu_w[l].reshape(D // LANE, LANE // S5_GROUP, S5_GROUP, S5_GROUP)
        glu_tiles = jnp.einsum("igjk,gh->igjhk", gw, eye8).reshape(D // LANE, LANE, LANE).astype(BF16)
        o_s5 = _s5_post(ycore, p16, row(s5_d[l]), glu_tiles, row(s5_glu_b[l]), tm)

        h = _merge(h, o_hg, o_rw, o_s5, p16, w_branch[l].reshape(3, D, D).astype(BF16),
                   w_out[l].astype(BF16), tm)
        h = _ffn(h, row(ffn_norm[l]), ffn_w_gate[l].astype(BF16), ffn_w_up[l].astype(BF16),
                 ffn_w_down[l].astype(BF16), row(final_norm), l == depth - 1, tm)
    return h.reshape(batch, seq, D)
```

```python
import functools
import math

import numpy as np
import jax
import jax.numpy as jnp
from jax import lax
from jax.experimental import pallas as pl
from jax.experimental.pallas import tpu as pltpu

F32 = jnp.float32
BF16 = jnp.bfloat16

D = 1024
DEPTH = 4
NORM_EPS = 1e-6
CHUNK = 64
LANE = 128
HG_HEADS = 8
RW_HEAD = 64
RW_GN_EPS = 64e-5
LORA_W, LORA_A, LORA_G, LORA_V = 64, 64, 160, 32
S5_GROUPS, S5_GROUP, S5_STATE = 64, 16, 64
FFN_HIDDEN = 2816
FFN_CHUNK = 256

C16_Q, C16_I, C16_OG, C16_R, C16_K, C16_V, C16_S5, C16_GATE = 0, 1, 2, 3, 4, 5, 6, 7
C32_F = 0
C32_LORA = 2
LOG2E = 1.4426950408889634
VMEM_LIMIT = 56 * 1024 * 1024


def _cparams(*sem):
    return pltpu.CompilerParams(dimension_semantics=sem, vmem_limit_bytes=VMEM_LIMIT)


def _split3(x):
    hi = x.astype(BF16)
    r1 = x - hi.astype(F32)
    mid = r1.astype(BF16)
    lo = (r1 - mid.astype(F32)).astype(BF16)
    return hi, mid, lo


def _dot(a, b):
    return jnp.dot(a, b, preferred_element_type=F32)


def _dot_nt(a, b):
    return lax.dot_general(a, b, (((1,), (1,)), ((), ())), preferred_element_type=F32)


def _exact_dot01(m3, x):
    hi, mid, lo = _split3(x)
    return _dot(m3, jnp.concatenate([hi, mid, lo], axis=0))


def _rms(x, g):
    ms = jnp.mean(x * x, axis=-1, keepdims=True)
    return x * lax.rsqrt(ms + NORM_EPS) * g


def _neg_abs(x):
    return -jnp.abs(x)


def _sigmoid(x):
    return 1.0 / (1.0 + jnp.exp(-x))


def _silu(x):
    return x * _sigmoid(x)


IN_ALIGNED_DST = (C16_Q, None, C16_I, C16_OG, C16_R, C16_K, C16_V)
N16 = (C16_GATE + 3) * D
N32 = D + 4 * LANE


def _inproj_body(x_ref, g_ref, *refs):
    na = len(IN_ALIGNED_DST)
    w_refs, wt_ref = refs[:na], refs[na]
    o16_ref, o32_ref, ut_ref, xn_ref = refs[na + 1:]
    xn_ref[...] = _rms(x_ref[...], g_ref[...]).astype(BF16)
    blk = lambda j: slice(j * D, (j + 1) * D)
    for w_ref, dst in zip(w_refs, IN_ALIGNED_DST):
        y = _dot(xn_ref[...], w_ref[...])
        if dst is None:
            o32_ref[:, blk(C32_F)] = y
        else:
            o16_ref[:, blk(dst)] = y.astype(o16_ref.dtype)
    y = _dot(xn_ref[...], wt_ref[:, blk(0)])
    o16_ref[:, blk(C16_S5)] = y.astype(o16_ref.dtype)
    for c in range(ut_ref.shape[0]):
        ut_ref[c] = y[c * CHUNK:(c + 1) * CHUNK, :].T.astype(ut_ref.dtype)
    for j in range(3):
        o16_ref[:, blk(C16_GATE + j)] = _dot(xn_ref[...], wt_ref[:, blk(1 + j)]).astype(o16_ref.dtype)
    o32_ref[:, D:N32] = _dot(xn_ref[...], wt_ref[:, 4 * D:4 * D + 4 * LANE])


def _inproj(h, g, w_all, layer, w_tail, tm):
    t = h.shape[0]
    row = lambda i: (i, 0)
    fixed = lambda i: (0, 0)
    once = dict(pipeline_mode=pl.Buffered(1))
    aligned = [pl.BlockSpec((None, D, D), functools.partial(lambda j, i: (layer, 0, j), j), **once)
               for j in range(len(IN_ALIGNED_DST))]
    return pl.pallas_call(
        _inproj_body,
        grid=(t // tm,),
        in_specs=[pl.BlockSpec((tm, D), row), pl.BlockSpec((1, D), fixed)] + aligned
                 + [pl.BlockSpec(w_tail.shape, fixed, **once)],
        out_specs=[pl.BlockSpec((tm, N16), row), pl.BlockSpec((tm, N32), row),
                   pl.BlockSpec((tm // CHUNK, D, CHUNK), lambda i: (i, 0, 0))],
        out_shape=[jax.ShapeDtypeStruct((t, N16), BF16), jax.ShapeDtypeStruct((t, N32), F32),
                   jax.ShapeDtypeStruct((t // CHUNK, D, CHUNK), BF16)],
        scratch_shapes=[pltpu.VMEM((tm, D), BF16)],
        compiler_params=_cparams("parallel"),
        name="inproj",
    )(h, g, *([w_all] * len(IN_ALIGNED_DST)), w_tail)


def _tail_body(final, h_ref, ohg_ref, orw_ref, y_ref, u_ref, g0_ref, g1_ref, g2_ref,
               d_ref, gw_ref, gb_ref, wb_ref, wo_ref, fn_ref, wg_ref, wu_ref, wd_ref, fg_ref,
               o_ref, yt_ref, s5_ref, xn_ref, acc_ref):
    for c in range(y_ref.shape[0]):
        yt_ref[c * CHUNK:(c + 1) * CHUNK, :] = y_ref[c].astype(F32).T
    for i in range(D // LANE):
        ln = slice(i * LANE, (i + 1) * LANE)
        y = yt_ref[:, ln] + d_ref[:, ln] * u_ref[:, ln].astype(F32)
        zg = 0.5 * y * (1.0 + jnp.tanh(math.sqrt(2.0 / math.pi) * (y + 0.044715 * (y * y * y))))
        gate = _dot(zg.astype(BF16), gw_ref[i]) + gb_ref[:, ln]
        s5_ref[:, ln] = (zg * _sigmoid(gate)).astype(BF16)

    m = _sigmoid(g0_ref[...].astype(F32)) * _dot(ohg_ref[...], wb_ref[0])
    m = m + _sigmoid(g1_ref[...].astype(F32)) * _dot(orw_ref[...], wb_ref[1])
    m = m + _sigmoid(g2_ref[...].astype(F32)) * _dot(s5_ref[...], wb_ref[2])
    x = h_ref[...] + _dot(m.astype(BF16), wo_ref[...])

    xn_ref[...] = _rms(x, fn_ref[...]).astype(BF16)
    for c in range(FFN_HIDDEN // FFN_CHUNK):
        sl = slice(c * FFN_CHUNK, (c + 1) * FFN_CHUNK)
        a = _dot(xn_ref[...], wg_ref[:, sl])
        b = _dot(xn_ref[...], wu_ref[:, sl])
        part = _dot((_silu(a) * b).astype(BF16), wd_ref[sl, :])
        if c == 0:
            acc_ref[...] = x + part
        else:
            acc_ref[...] += part
    out = acc_ref[...]
    if final:
        out = _rms(out, fg_ref[...])
    o_ref[...] = out


def _tail(h, o_hg, o_rw, ycore, p16, d, glu_tiles, glu_b, wb, wo, fn, wg, wu, wd, fg, final, tm):
    t = h.shape[0]
    row = lambda i: (i, 0)
    once = dict(pipeline_mode=pl.Buffered(1))
    vec = pl.BlockSpec((1, D), lambda i: (0, 0))
    rowblk = pl.BlockSpec((tm, D), row)
    col = lambda j: pl.BlockSpec((tm, D), lambda i: (i, j))
    return pl.pallas_call(
        functools.partial(_tail_body, final),
        grid=(t // tm,),
        in_specs=[rowblk, rowblk, rowblk,
                  pl.BlockSpec((tm // CHUNK, D, CHUNK), lambda i: (i, 0, 0)),
                  col(C16_S5), col(C16_GATE), col(C16_GATE + 1), col(C16_GATE + 2),
                  vec, pl.BlockSpec((D // LANE, LANE, LANE), lambda i: (0, 0, 0), **once), vec,
                  pl.BlockSpec((3, D, D), lambda i: (0, 0, 0), **once),
                  pl.BlockSpec((D, D), lambda i: (0, 0), **once),
                  vec,
                  pl.BlockSpec((D, FFN_HIDDEN), lambda i: (0, 0), **once),
                  pl.BlockSpec((D, FFN_HIDDEN), lambda i: (0, 0), **once),
                  pl.BlockSpec((FFN_HIDDEN, D), lambda i: (0, 0), **once),
                  vec],
        out_specs=rowblk,
        out_shape=jax.ShapeDtypeStruct((t, D), F32),
        scratch_shapes=[pltpu.VMEM((tm, D), F32), pltpu.VMEM((tm, D), BF16),
                        pltpu.VMEM((tm, D), BF16), pltpu.VMEM((tm, D), F32)],
        compiler_params=_cparams("parallel"),
        name="tail",
    )(h, o_hg, o_rw, ycore, p16, p16, p16, p16, d, glu_tiles, glu_b, wb, wo, fn, wg, wu, wd, fg)


HG_LEVELS = (32, 16, 8, 4, 2, 1)


def _hg_dmat():
    t = np.arange(CHUNK)
    tri = (t[None, :] <= t[:, None]).astype(np.float32)
    blocks = [tri]
    for m in HG_LEVELS:
        r = (t // (2 * m)) * (2 * m) + m - 1
        blocks.append(tri - tri[r])
    d = np.concatenate(blocks, axis=0)
    return jnp.asarray(np.concatenate([d, d, d], axis=1), BF16)


def _hg_body(layer, q_ref, f_ref, i_ref, og_ref, lbl_ref, on_ref, dm_ref, o_ref, st_ref, d_ref):
    tb = q_ref.shape[0]

    @pl.when(pl.program_id(1) == 0)
    def _():
        st_ref[...] = jnp.zeros_like(st_ref)

    lg = lbl_ref[...]
    e = jnp.exp(lg - jnp.max(lg, axis=0, keepdims=True))
    sm = e / jnp.sum(e, axis=0, keepdims=True)
    lb = jnp.zeros((1, D), F32)
    for l in range(1, layer + 1):
        lb = lb + sm[l:l + 1, :]
    lb = jnp.maximum(lb, 0.0)
    log_lb = jnp.log(lb)
    log_1mlb = jnp.log1p(-lb)
    one_mlb = 1.0 - lb

    ti = lax.broadcasted_iota(jnp.int32, (CHUNK, CHUNK), 0)
    si = lax.broadcasted_iota(jnp.int32, (CHUNK, CHUNK), 1)
    xr = ti ^ si
    lower = ti > si
    pair_masks = [lower & (xr >= m) & (xr < 2 * m) for m in HG_LEVELS]
    diag_mask = ti == si
    trow = lax.broadcasted_iota(jnp.int32, (CHUNK, LANE), 0)
    second = [(trow & m) != 0 for m in HG_LEVELS]

    def chunk_step(c, carry):
        r0 = pl.multiple_of(c * CHUNK, CHUNK)
        rows = pl.ds(r0, CHUNK)
        z = f_ref[rows, :]
        ez = jnp.exp(_neg_abs(z))
        lsig = jnp.minimum(z, 0.0) - jnp.log(1.0 + ez)
        cterm = log_1mlb + lsig
        mx = jnp.maximum(log_lb, cterm)
        logf = mx + jnp.log(1.0 + jnp.exp(_neg_abs(log_lb - cterm)))
        d_ref[...] = _exact_dot01(dm_ref[...], logf * LOG2E)
        kk_all = one_mlb * jnp.where(z >= 0.0, ez, 1.0) / (1.0 + ez)
        heads = range(HG_HEADS)
        lns = [slice(h * LANE, (h + 1) * LANE) for h in heads]
        q = [_silu(q_ref[rows, lns[h]].astype(F32)) for h in heads]
        k = [kk_all[:, lns[h]] for h in heads]
        s0 = [_dot_nt(q[h].astype(BF16), k[h].astype(BF16)) for h in heads]
        scores = [jnp.where(diag_mask, s0[h], 0.0) for h in heads]
        for li in range(len(HG_LEVELS)):
            x = [(jnp.where(second[li], q[h], k[h])
                  * jnp.exp2(_neg_abs(d_ref[(li + 1) * CHUNK:(li + 2) * CHUNK, lns[h]]))).astype(BF16)
                 for h in heads]
            sl = [_dot_nt(x[h], x[h]) for h in heads]
            scores = [jnp.where(pair_masks[li], sl[h], scores[h]) for h in heads]
        o = []
        for h in heads:
            b = d_ref[0:CHUNK, lns[h]]
            qe = (q[h] * jnp.exp2(b)).astype(BF16)
            o.append(_dot(scores[h].astype(BF16), i_ref[rows, lns[h]].astype(BF16))
                     + _dot_nt(qe, st_ref[h].astype(BF16)))
        for h in heads:
            b = d_ref[0:CHUNK, lns[h]]
            b_last = b[CHUNK - 1:CHUNK, :]
            kd = (k[h] * jnp.exp2(b_last - b)).astype(BF16)
            vt = i_ref[rows, lns[h]].astype(F32).T.astype(BF16)
            st_ref[h] = st_ref[h] * jnp.exp2(b_last) + _dot(vt, kd)
        for h in heads:
            oh = o[h] * lax.rsqrt(jnp.mean(o[h] * o[h], axis=-1, keepdims=True) + NORM_EPS)
            oh = oh * on_ref[:, lns[h]] * _silu(og_ref[rows, lns[h]].astype(F32))
            o_ref[rows, lns[h]] = oh.astype(o_ref.dtype)
        return carry

    lax.fori_loop(0, tb // CHUNK, chunk_step, 0)


def _hgrn2(p16, p32, lb_logits, onorm, layer, batch, seq, tb):
    t = p16.shape[0]
    nb = seq // tb
    col = lambda j: (lambda b, s: (b * nb + s, j))
    fixed = lambda b, s: (0, 0)
    return pl.pallas_call(
        functools.partial(_hg_body, layer),
        grid=(batch, nb),
        in_specs=[
            pl.BlockSpec((tb, D), col(C16_Q)),
            pl.BlockSpec((tb, D), col(C32_F)),
            pl.BlockSpec((tb, D), col(C16_I)),
            pl.BlockSpec((tb, D), col(C16_OG)),
            pl.BlockSpec(lb_logits.shape, fixed),
            pl.BlockSpec((1, D), fixed),
            pl.BlockSpec((7 * CHUNK, 3 * CHUNK), fixed),
        ],
        out_specs=pl.BlockSpec((tb, D), lambda b, s: (b * nb + s, 0)),
        out_shape=jax.ShapeDtypeStruct((t, D), BF16),
        scratch_shapes=[pltpu.VMEM((HG_HEADS, LANE, LANE), F32),
                        pltpu.VMEM((7 * CHUNK, D), F32)],
        compiler_params=_cparams("parallel", "arbitrary"),
        name="hgrn2",
    )(p16, p32, p16, p16, lb_logits, onorm, _hg_dmat())


S5_ROW = CHUNK * S5_GROUP


def _s5_consts():
    tj = np.arange(S5_ROW)
    e_tj = (tj[None, :] % CHUNK == np.arange(CHUNK)[:, None]).astype(np.float32)
    f_tj = (tj[None, :] // CHUNK == np.arange(S5_GROUP)[:, None]).astype(np.float32)
    return (jnp.asarray(e_tj, BF16), jnp.asarray(f_tj, BF16),
            jnp.asarray(e_tj.T, BF16), jnp.asarray(f_tj.T, BF16))


def _right01(x, e):
    hi, mid, lo = _split3(x)
    return _dot(hi, e) + _dot(mid, e) + _dot(lo, e)


def _left01(e, x):
    hi, mid, lo = _split3(x)
    return _dot(e, hi) + _dot(e, mid) + _dot(e, lo)


def _dot_f32(a, b):
    ah, am, al = _split3(a)
    bh, bm, bl = _split3(b)
    return (_dot(ah, bh) + _dot(ah, bm) + _dot(am, bh)
            + _dot(ah, bl) + _dot(al, bh) + _dot(am, bm))


def _s5_prep_body(lrr_ref, lir_ref, lrc_ref, lic_ref, ls_ref, brt_ref, bit_ref, crt_ref, cit_ref,
                  etj_ref, ftj_ref, esh_ref, fsh_ref,
                  wt_ref, wi_ref, wis_ref, wo_ref, a1_ref, a2_ref):
    dt = jnp.exp(ls_ref[...])
    lam_re, lam_im = lrr_ref[...], lir_ref[...]
    lr_row, li_row = lam_re * dt, lam_im * dt
    lr_col, li_col = lrc_ref[...] * dt, lic_ref[...] * dt
    re_half = lax.broadcasted_iota(jnp.int32, (1, 2 * S5_STATE), 1) < S5_STATE

    mag = jnp.exp(lr_row)
    ab_re, ab_im = mag * jnp.cos(li_row), mag * jnp.sin(li_row)
    den = lam_re * lam_re + lam_im * lam_im
    nr, ni = ab_re - 1.0, ab_im
    coef_re = (nr * lam_re + ni * lam_im) / den
    coef_im = (ni * lam_re - nr * lam_im) / den
    bb_re = coef_re * brt_ref[...] - coef_im * bit_ref[...]
    bb_im = coef_re * bit_ref[...] + coef_im * brt_ref[...]

    def powers(lr, li, n):
        m = jnp.exp(lr * n)
        return m * jnp.cos(li * n), m * jnp.sin(li * n)

    tau = lax.broadcasted_iota(jnp.int32, (1, CHUNK), 1).astype(F32)
    p0_re, p0_im = powers(lr_col, li_col, tau)
    p1_re, p1_im = powers(lr_col, li_col, tau + 1.0)
    c_re = _right01(crt_ref[...], ftj_ref[...])
    c_im = _right01(cit_ref[...], ftj_ref[...])
    p0e_re, p0e_im = _right01(p0_re, etj_ref[...]), _right01(p0_im, etj_ref[...])
    p1e_re, p1e_im = _right01(p1_re, etj_ref[...]), _right01(p1_im, etj_ref[...])

    m_stack = jnp.concatenate([c_re * p0e_re - c_im * p0e_im, c_re * p0e_im + c_im * p0e_re], axis=0)
    r0 = _dot_f32(jnp.where(re_half, bb_re, -bb_im), m_stack)
    t_lane = lax.broadcasted_iota(jnp.int32, (CHUNK, S5_ROW), 1) & (CHUNK - 1)
    s_row = lax.broadcasted_iota(jnp.int32, (CHUNK, S5_ROW), 0)
    for h in range(S5_GROUP):
        base = jnp.broadcast_to(r0[h:h + 1, :], (CHUNK, S5_ROW))
        rolled = pltpu.roll(base, 0, 1, stride=1, stride_axis=0)
        wt_ref[h * CHUNK:(h + 1) * CHUNK, :] = jnp.where(t_lane >= s_row, rolled, 0.0).astype(BF16)

    wo_ref[0:S5_STATE, :] = (c_re * p1e_re - c_im * p1e_im).astype(BF16)
    wo_ref[S5_STATE:2 * S5_STATE, :] = (-(c_re * p1e_im + c_im * p1e_re)).astype(BF16)

    back = (CHUNK - 1.0) - lax.broadcasted_iota(jnp.int32, (CHUNK, 1), 0).astype(F32)
    pt_re, pt_im = powers(lr_row, li_row, back)
    pte_re, pte_im = _left01(esh_ref[...], pt_re), _left01(esh_ref[...], pt_im)
    bbe_re, bbe_im = _left01(fsh_ref[...], bb_re), _left01(fsh_ref[...], bb_im)
    w_re = pte_re * bbe_re - pte_im * bbe_im
    w_im = pte_re * bbe_im + pte_im * bbe_re
    wi_ref[...] = jnp.where(re_half, w_re, w_im).astype(BF16)
    wis_ref[...] = jnp.where(re_half, w_im, w_re).astype(BF16)

    k = lax.broadcasted_iota(jnp.int32, (8, 1), 0)
    n = (CHUNK * (1 << k)).astype(F32)
    an_re, an_im = powers(lr_row, li_row, n)
    a1_ref[...] = an_re
    a2_ref[...] = jnp.where(re_half, -an_im, an_im)


def _s5_prep(lam_re, lam_im, log_step, b_re, b_im, c_re, c_im):
    nl = lam_re.shape[0]
    ns = 2 * S5_STATE
    g4 = lambda r, c: pl.BlockSpec((None, None, r, c), lambda l, g: (l, g, 0, 0))
    cst = lambda r, c: pl.BlockSpec((r, c), lambda l, g: (0, 0))
    sds = lambda r, c, dt: jax.ShapeDtypeStruct((nl, S5_GROUPS, r, c), dt)
    twice = lambda a: jnp.concatenate([a, a], axis=-1)
    e_tj, f_tj, e_sh, f_sh = _s5_consts()
    return pl.pallas_call(
        _s5_prep_body,
        grid=(nl, S5_GROUPS),
        in_specs=[g4(1, ns), g4(1, ns), g4(S5_STATE, 1), g4(S5_STATE, 1), g4(1, 1),
                  g4(S5_GROUP, ns), g4(S5_GROUP, ns), g4(S5_STATE, S5_GROUP), g4(S5_STATE, S5_GROUP),
                  cst(CHUNK, S5_ROW), cst(S5_GROUP, S5_ROW), cst(S5_ROW, CHUNK), cst(S5_ROW, S5_GROUP)],
        out_specs=[g4(S5_ROW, S5_ROW), g4(S5_ROW, ns), g4(S5_ROW, ns), g4(ns, S5_ROW), g4(8, ns), g4(8, ns)],
        out_shape=[sds(S5_ROW, S5_ROW, BF16), sds(S5_ROW, ns, BF16), sds(S5_ROW, ns, BF16),
                   sds(ns, S5_ROW, BF16), sds(8, ns, F32), sds(8, ns, F32)],
        compiler_params=_cparams("parallel", "parallel"),
        name="s5_prep",
    )(twice(lam_re)[:, :, None, :], twice(lam_im)[:, :, None, :], lam_re[:, :, :, None], lam_im[:, :, :, None],
      log_step[:, :, None, None],
      twice(jnp.swapaxes(b_re, 2, 3)), twice(jnp.swapaxes(b_im, 2, 3)),
      jnp.swapaxes(c_re, 2, 3), jnp.swapaxes(c_im, 2, 3),
      e_tj, f_tj, e_sh, f_sh)


def _s5_body(nc, u_ref, wt_ref, wi_ref, wis_ref, wo_ref, a1_ref, a2_ref, y_ref):
    u = u_ref[...]
    m = u.shape[0]
    x = _dot(u, wi_ref[...])
    xs = _dot(u, wis_ref[...])
    chunk = lax.broadcasted_iota(jnp.int32, (m, 2 * S5_STATE), 0) % nc

    def shifted(z, n):
        return jnp.where(chunk >= n, pltpu.roll(z, n, 0), 0.0)

    k = 0
    while (1 << k) < nc:
        a1, a2 = a1_ref[k:k + 1, :], a2_ref[k:k + 1, :]
        sx, sxs = shifted(x, 1 << k), shifted(xs, 1 << k)
        x, xs = x + a1 * sx + a2 * sxs, xs + a1 * sxs - a2 * sx
        k += 1
    y = _dot(u, wt_ref[...]) + _dot(shifted(x, 1).astype(BF16), wo_ref[...])
    y_ref[...] = y.astype(y_ref.dtype)


def _s5_core(u_t, ops, layer, nc):
    wt, wi, wis, wo, a1, a2 = ops
    m = u_t.shape[0]
    ns = 2 * S5_STATE
    grp = pl.BlockSpec((m, S5_ROW), lambda g: (0, g))
    lg = lambda r, c: pl.BlockSpec((None, None, r, c), lambda g: (layer, g, 0, 0))
    return pl.pallas_call(
        functools.partial(_s5_body, nc),
        grid=(S5_GROUPS,),
        in_specs=[grp, lg(S5_ROW, S5_ROW), lg(S5_ROW, ns), lg(S5_ROW, ns),
                  lg(ns, S5_ROW), lg(8, ns), lg(8, ns)],
        out_specs=grp,
        out_shape=jax.ShapeDtypeStruct((m, S5_GROUPS * S5_ROW), BF16),
        compiler_params=_cparams("parallel"),
        name="s5_core",
    )(u_t, wt, wi, wis, wo, a1, a2)


def _ones_bd():
    i = np.arange(LANE)
    return jnp.asarray((i[:, None] // RW_HEAD == i[None, :] // RW_HEAD).astype(np.float32), BF16)


def _segsum(x, ones_bd):
    hi = x.astype(BF16)
    lo = (x - hi.astype(F32)).astype(BF16)
    return _dot(hi, ones_bd) + _dot(lo, ones_bd)


def _rw_prep_body(first, r_ref, k_ref, v_ref, lo_ref, vf_ref, mu_ref, mul_ref, w0_ref, w2_ref, a0_ref, a2_ref,
                  g2_ref, v0_ref, v1_ref, v2_ref, kk_ref, ka_ref, ones_ref,
                  ro_ref, lw_ref, ko_ref, vo_ref, kko_ref, bo_ref, go_ref, cr_ref, cl_ref):
    tb = r_ref.shape[0]

    @pl.when(pl.program_id(1) == 0)
    def _():
        cr_ref[...] = jnp.zeros_like(cr_ref)
        cl_ref[...] = jnp.zeros_like(cl_ref)

    def mix(z, carry, mu):
        row = lax.broadcasted_iota(jnp.int32, z.shape, 0)
        zs = jnp.where(row == 0, carry, pltpu.roll(z, 1, 0))
        return z + mu * (zs - z)

    def softplus(x):
        return jnp.maximum(x, 0.0) + jnp.log(1.0 + jnp.exp(_neg_abs(x)))

    zl = lo_ref[...]
    lora = mix(zl, cl_ref[0:1, :], mul_ref[...])
    cl_ref[0:1, :] = zl[tb - 1:tb, :]
    wd = jnp.tanh(lora[:, 0:LANE]).astype(BF16)
    ad = lora[:, LANE:2 * LANE].astype(BF16)
    gd = _sigmoid(lora[:, 2 * LANE:4 * LANE]).astype(BF16)
    w_log = -softplus(-(w0_ref[...] + _dot(wd, w2_ref[...]))) - 0.5
    lw_ref[...] = -jnp.exp(w_log)
    a = _sigmoid(a0_ref[...] + _dot(ad, a2_ref[...]))
    go_ref[...] = _dot(gd, g2_ref[...]).astype(go_ref.dtype)

    zr = r_ref[...].astype(F32)
    ro_ref[...] = mix(zr, cr_ref[0:1, :], mu_ref[0:1, :]).astype(ro_ref.dtype)
    cr_ref[0:1, :] = zr[tb - 1:tb, :]
    zk = k_ref[...].astype(F32)
    k = mix(zk, cr_ref[1:2, :], mu_ref[1:2, :])
    cr_ref[1:2, :] = zk[tb - 1:tb, :]
    zv = v_ref[...].astype(F32)
    v = mix(zv, cr_ref[2:3, :], mu_ref[2:3, :])
    cr_ref[2:3, :] = zv[tb - 1:tb, :]
    if not first:
        lowr = _dot(v.astype(BF16), v1_ref[...]).astype(BF16)
        v = v + (vf_ref[...].astype(F32) - v) * _sigmoid(v0_ref[...] + _dot(lowr, v2_ref[...]))
    vo_ref[...] = v.astype(vo_ref.dtype)

    kk = k * kk_ref[...]
    ss = jnp.concatenate([_segsum(kk[:, i * LANE:(i + 1) * LANE] * kk[:, i * LANE:(i + 1) * LANE], ones_ref[...])
                          for i in range(D // LANE)], axis=1)
    kk = kk * lax.rsqrt(jnp.maximum(ss, 1e-24))
    kko_ref[...] = kk.astype(kko_ref.dtype)
    bo_ref[...] = (kk * a).astype(bo_ref.dtype)
    ko_ref[...] = (k * (1.0 + (a - 1.0) * ka_ref[...])).astype(ko_ref.dtype)


def _rw_prep(p16, p32, v_first, p, first, batch, seq, tb):
    t = p16.shape[0]
    nb = seq // tb
    col = lambda j: (lambda b, s: (b * nb + s, j))
    fixed = lambda b, s: (0, 0)
    vec = pl.BlockSpec((1, D), fixed)
    rowblk = pl.BlockSpec((tb, D), lambda b, s: (b * nb + s, 0))
    out = jax.ShapeDtypeStruct((t, D), BF16)
    return pl.pallas_call(
        functools.partial(_rw_prep_body, first),
        grid=(batch, nb),
        in_specs=[pl.BlockSpec((tb, D), col(C16_R)), pl.BlockSpec((tb, D), col(C16_K)),
                  pl.BlockSpec((tb, D), col(C16_V)),
                  pl.BlockSpec((tb, 4 * LANE), col(C32_LORA)),
                  rowblk,
                  pl.BlockSpec((3, D), fixed), pl.BlockSpec((1, 4 * LANE), fixed),
                  vec, pl.BlockSpec((LANE, D), fixed), vec, pl.BlockSpec((LANE, D), fixed),
                  pl.BlockSpec((2 * LANE, D), fixed),
                  vec, pl.BlockSpec((D, LANE), fixed), pl.BlockSpec((LANE, D), fixed),
                  vec, vec, pl.BlockSpec((LANE, LANE), fixed)],
        out_specs=[rowblk] * 7,
        out_shape=[out, jax.ShapeDtypeStruct((t, D), F32)] + [out] * 5,
        scratch_shapes=[pltpu.VMEM((8, D), F32), pltpu.VMEM((8, 4 * LANE), F32)],
        compiler_params=_cparams("parallel", "arbitrary"),
        name="rw_prep",
    )(p16, p16, p16, p32, v_first, p["mu"], p["mu_lora"], p["w0"], p["w2"], p["a0"], p["a2"], p["g2"],
      p["v0"], p["v1"], p["v2"], p["k_k"], p["k_a"], _ones_bd())


RW_LEVELS = (2, 4, 8, 16, 32)


def _rwc_body(r_ref, lw_ref, k_ref, v_ref, kk_ref, b_ref, g_ref, rk_ref, lnw_ref, lnb_ref, tri_ref, ones_ref,
              o_ref, ht_ref, lg_ref):
    nbat, tb = r_ref.shape[0], r_ref.shape[1]
    ntile = D // LANE

    @pl.when(pl.program_id(1) == 0)
    def _():
        ht_ref[...] = jnp.zeros_like(ht_ref)

    t64 = lax.broadcasted_iota(jnp.int32, (CHUNK, LANE), 0)
    l64 = lax.broadcasted_iota(jnp.int32, (CHUNK, LANE), 1)
    s64 = l64 & (RW_HEAD - 1)
    strict = t64 > s64
    incl = t64 >= s64
    eye = t64 == s64
    xr = t64 ^ s64
    lvl1 = strict & (xr == 1)
    lvl = [strict & (xr >= m) & (xr < 2 * m) for m in RW_LEVELS]
    head0 = l64 < RW_HEAD
    r128 = lax.broadcasted_iota(jnp.int32, (LANE, LANE), 0)
    l128 = lax.broadcasted_iota(jnp.int32, (LANE, LANE), 1)
    same_head = (r128 >> 6) == (l128 >> 6)

    def stack(x):
        return jnp.concatenate([jnp.where(head0, x, 0.0), jnp.where(head0, 0.0, x)], axis=0).astype(BF16)

    def chunk_step(c, carry):
        rows = pl.ds(pl.multiple_of(c * CHUNK, CHUNK), CHUNK)
        for bi in range(nbat):
            lg_ref[bi] = _exact_dot01(tri_ref[...], lw_ref[bi, rows, :])
        probs = [(bi, j) for bi in range(nbat) for j in range(ntile)]
        n = range(len(probs))

        def ld(ref, p):
            return ref[probs[p][0], rows, probs[p][1] * LANE:(probs[p][1] + 1) * LANE].astype(F32)

        def lgp(p):
            return lg_ref[probs[p][0], :, probs[p][1] * LANE:(probs[p][1] + 1) * LANE]

        a_t, r_t, a_ak, a_rk, l_ab, a_rb, x = [], [], [], [], [], [], []
        for p in n:
            lg = lgp(p)
            e_neg = jnp.exp(-lg)
            a_t.append(-ld(kk_ref, p) * jnp.exp(lg - ld(lw_ref, p)))
            r_t.append(ld(r_ref, p) * jnp.exp(lg))
            lhs = jnp.concatenate([a_t[p], r_t[p]], axis=0).astype(BF16)
            kb_neg = jnp.concatenate([stack(ld(k_ref, p) * e_neg), stack(ld(b_ref, p) * e_neg)], axis=0)
            a_kb = _dot_nt(lhs, kb_neg)
            a_ak.append(jnp.where(strict, a_kb[0:CHUNK, 0:LANE], 0.0))
            a_rk.append(jnp.where(incl, a_kb[CHUNK:2 * CHUNK, 0:LANE], 0.0))
            l_ab.append(jnp.where(strict, a_kb[0:CHUNK, LANE:2 * LANE], 0.0))
            a_rb.append(jnp.where(incl, a_kb[CHUNK:2 * CHUNK, LANE:2 * LANE], 0.0))
            x.append(jnp.where(eye, 1.0, 0.0) + jnp.where(lvl1, l_ab[p], 0.0))

        for li in range(len(RW_LEVELS)):
            z = [_dot(x[p].astype(BF16), stack(jnp.where(lvl[li], l_ab[p], 0.0))) for p in n]
            x = [x[p] + _dot(z[p].astype(BF16), stack(x[p])) for p in n]

        v_st = [stack(ld(v_ref, p)) for p in n]
        mv = [_dot(a_ak[p].astype(BF16), v_st[p]) for p in n]
        u0w = [_dot(x[p].astype(BF16), jnp.concatenate([stack(mv[p]), stack(a_t[p])], axis=1)) for p in n]
        htb = [ht_ref[p].astype(BF16) for p in n]
        u = [u0w[p][:, 0:LANE] + _dot_nt(u0w[p][:, LANE:2 * LANE].astype(BF16), htb[p]) for p in n]
        y = [_dot_nt(r_t[p].astype(BF16), htb[p]) + _dot(a_rk[p].astype(BF16), v_st[p])
             + _dot(a_rb[p].astype(BF16), stack(u[p])) for p in n]
        for p in n:
            lg = lgp(p)
            lg_end = lg[CHUNK - 1:CHUNK, :]
            e_end = jnp.exp(lg_end - lg)
            vu = jnp.concatenate([ld(v_ref, p), u[p]], axis=0)
            kb = jnp.concatenate([ld(k_ref, p) * e_end, ld(b_ref, p) * e_end], axis=0).astype(BF16)
            upd = _dot(vu.T.astype(BF16), kb)
            ht_ref[p] = ht_ref[p] * jnp.exp(lg_end) + jnp.where(same_head, upd, 0.0)

        ones = ones_ref[...]
        sl = lambda a, p: a[p * CHUNK:(p + 1) * CHUNK]
        mean = _segsum(jnp.concatenate(y, axis=0), ones) * (1.0 / RW_HEAD)
        yc = [y[p] - sl(mean, p) for p in n]
        var = _segsum(jnp.concatenate([c * c for c in yc], axis=0), ones) * (1.0 / RW_HEAD)
        rk = [ld(r_ref, p) * ld(k_ref, p) * rk_ref[:, probs[p][1] * LANE:(probs[p][1] + 1) * LANE] for p in n]
        bonus = _segsum(jnp.concatenate(rk, axis=0), ones)
        for p in n:
            bi, j = probs[p]
            ln = slice(j * LANE, (j + 1) * LANE)
            yn = yc[p] * lax.rsqrt(sl(var, p) + RW_GN_EPS) * lnw_ref[:, ln] + lnb_ref[:, ln]
            o_ref[bi, rows, ln] = ((yn + sl(bonus, p) * ld(v_ref, p)) * ld(g_ref, p)).astype(o_ref.dtype)
        return carry

    lax.fori_loop(0, tb // CHUNK, chunk_step, 0)


def _rwc_core(feats, r_k, ln_w, ln_b, batch, seq, nbat, tb):
    t = feats[0].shape[0]
    fixed = lambda b, s: (0, 0)
    vec = pl.BlockSpec((1, D), fixed)
    blk = pl.BlockSpec((nbat, tb, D), lambda b, s: (b, s, 0))
    tri = np.tril(np.ones((CHUNK, CHUNK), np.float32))
    tri3 = jnp.asarray(np.concatenate([tri, tri, tri], axis=1), BF16)
    out = pl.pallas_call(
        _rwc_body,
        grid=(batch // nbat, seq // tb),
        in_specs=[blk] * 7 + [vec, vec, vec, pl.BlockSpec((CHUNK, 3 * CHUNK), fixed),
                              pl.BlockSpec((LANE, LANE), fixed)],
        out_specs=blk,
        out_shape=jax.ShapeDtypeStruct((batch, seq, D), BF16),
        scratch_shapes=[pltpu.VMEM((nbat * (D // LANE), LANE, LANE), F32), pltpu.VMEM((nbat, CHUNK, D), F32)],
        compiler_params=_cparams("parallel", "arbitrary"),
        name="rw_core",
    )(*[f.reshape(batch, seq, D) for f in feats], r_k, ln_w, ln_b, tri3, _ones_bd())
    return out.reshape(t, D)


def _pack_w_tail(w):
    o_lora = len(IN_ALIGNED_DST) * D
    o_a, o_g = o_lora + LORA_W, o_lora + LORA_W + LORA_A
    o_s5 = o_g + LORA_G
    z = lambda n: jnp.zeros((D, n), w.dtype)
    return jnp.concatenate([w[:, o_s5:o_s5 + 4 * D], w[:, o_lora:o_a], z(LANE - LORA_W), w[:, o_a:o_g],
                            z(LANE - LORA_A), w[:, o_g:o_s5], z(2 * LANE - LORA_G)], axis=1)


def _pad_rows(w, n):
    return jnp.concatenate([w, jnp.zeros((n - w.shape[0], w.shape[1]), w.dtype)], axis=0)


def _pad_lanes(v, n):
    return jnp.concatenate([v, jnp.zeros((n - v.shape[0],), v.dtype)])


def kernel(x, mix_norm, w_in, hg_lb_logits, hg_onorm, rw_shift_mu, rw_w0, rw_w2, rw_a0, rw_a2, rw_g2, rw_v0, rw_v1, rw_v2, rw_k_k, rw_k_a, rw_r_k, rw_ln_w, rw_ln_b, s5_lambda_re, s5_lambda_im, s5_log_step, s5_b_re, s5_b_im, s5_c_re, s5_c_im, s5_d, s5_glu_w, s5_glu_b, w_branch, w_out, ffn_norm, ffn_w_gate, ffn_w_up, ffn_w_down, final_norm):
    batch, seq, _ = x.shape
    t = batch * seq
    nc = seq // CHUNK
    depth = w_in.shape[0]
    tm = min(512, t)
    tb = min(256, seq)
    h = x.reshape(t, D)
    row = lambda v: v.reshape(1, -1)

    s5_ops = _s5_prep(s5_lambda_re, s5_lambda_im, s5_log_step, s5_b_re, s5_b_im, s5_c_re, s5_c_im)
    w_in_bf = w_in.astype(BF16)
    v_first = None
    for l in range(depth):
        p16, p32, u_t = _inproj(h, row(mix_norm[l]), w_in_bf, l, _pack_w_tail(w_in_bf[l]), min(256, t))

        o_hg = _hgrn2(p16, p32, hg_lb_logits, row(hg_onorm[l]), l, batch, seq, tb)

        mu = rw_shift_mu[l]
        o1, o2, o3 = 3 * D + LORA_W, 3 * D + LORA_W + LORA_A, 3 * D + LORA_W + LORA_A + LORA_G
        first = l == 0
        rp = {
            "mu": mu[0:3 * D].reshape(3, D),
            "mu_lora": row(jnp.concatenate([_pad_lanes(mu[3 * D:o1], LANE), _pad_lanes(mu[o1:o2], LANE),
                                            _pad_lanes(mu[o2:o3], 2 * LANE)])),
            "w0": row(rw_w0[l]), "w2": _pad_rows(rw_w2[l], LANE).astype(BF16),
            "a0": row(rw_a0[l]), "a2": _pad_rows(rw_a2[l], LANE).astype(BF16),
            "g2": _pad_rows(rw_g2[l], 2 * LANE).astype(BF16),
            "v0": row(rw_v0[0 if first else l - 1]),
            "v1": jnp.concatenate([rw_v1[0 if first else l - 1],
                                   jnp.zeros((D, LANE - LORA_V), F32)], axis=1).astype(BF16),
            "v2": _pad_rows(rw_v2[0 if first else l - 1], LANE).astype(BF16),
            "k_k": row(rw_k_k[l]), "k_a": row(rw_k_a[l]),
        }
        feats = _rw_prep(p16, p32, p16 if first else v_first, rp, first, batch, seq, tb)
        if first:
            v_first = feats[3]
        o_rw = _rwc_core(feats, row(rw_r_k[l]), row(rw_ln_w[l]), row(rw_ln_b[l]), batch, seq,
                         math.gcd(batch, 4), min(128, seq))

        y_t = _s5_core(u_t.reshape(t // CHUNK, S5_GROUPS * S5_ROW), s5_ops, l, nc)
        ycore = y_t.reshape(t // CHUNK, D, CHUNK)
        eye8 = jnp.eye(LANE // S5_GROUP, dtype=F32)
        gw = s5_glu_w[l].reshape(D // LANE, LANE // S5_GROUP, S5_GROUP, S5_GROUP)
        glu_tiles = jnp.einsum("igjk,gh->igjhk", gw, eye8).reshape(D // LANE, LANE, LANE).astype(BF16)
        h = _tail(h, o_hg, o_rw, ycore, p16, row(s5_d[l]), glu_tiles, row(s5_glu_b[l]),
                  w_branch[l].reshape(3, D, D).astype(BF16), w_out[l].astype(BF16),
                  row(ffn_norm[l]), ffn_w_gate[l].astype(BF16), ffn_w_up[l].astype(BF16),
                  ffn_w_down[l].astype(BF16), row(final_norm), l == depth - 1, tm)
    return h.reshape(batch, seq, D)
```

```python
import functools
import math

import numpy as np
import jax
import jax.numpy as jnp
from jax import lax
from jax.experimental import pallas as pl
from jax.experimental.pallas import tpu as pltpu

F32 = jnp.float32
BF16 = jnp.bfloat16

D = 1024
DEPTH = 4
NORM_EPS = 1e-6
CHUNK = 64
LANE = 128
HG_HEADS = 8
RW_HEAD = 64
RW_GN_EPS = 64e-5
LORA_W, LORA_A, LORA_G, LORA_V = 64, 64, 160, 32
S5_GROUPS, S5_GROUP, S5_STATE = 64, 16, 64
FFN_HIDDEN = 2816
FFN_CHUNK = 256

C16_Q, C16_I, C16_OG, C16_R, C16_K, C16_V, C16_S5, C16_GATE = 0, 1, 2, 3, 4, 5, 6, 7
C32_F = 0
C32_LORA = 2
LOG2E = 1.4426950408889634
VMEM_LIMIT = 56 * 1024 * 1024


def _cparams(*sem):
    return pltpu.CompilerParams(dimension_semantics=sem, vmem_limit_bytes=VMEM_LIMIT)


def _split3(x):
    hi = x.astype(BF16)
    r1 = x - hi.astype(F32)
    mid = r1.astype(BF16)
    lo = (r1 - mid.astype(F32)).astype(BF16)
    return hi, mid, lo


def _dot(a, b):
    return jnp.dot(a, b, preferred_element_type=F32)


def _dot_nt(a, b):
    return lax.dot_general(a, b, (((1,), (1,)), ((), ())), preferred_element_type=F32)


def _exact_dot01(m3, x):
    hi, mid, lo = _split3(x)
    return _dot(m3, jnp.concatenate([hi, mid, lo], axis=0))


def _rms(x, g):
    ms = jnp.mean(x * x, axis=-1, keepdims=True)
    return x * lax.rsqrt(ms + NORM_EPS) * g


def _neg_abs(x):
    return -jnp.abs(x)


def _sigmoid(x):
    return 1.0 / (1.0 + jnp.exp(-x))


def _silu(x):
    return x * _sigmoid(x)


IN_ALIGNED_DST = (C16_Q, None, C16_I, C16_OG, C16_R, C16_K, C16_V)
N16 = (C16_GATE + 3) * D
N32 = D + 4 * LANE


def _inproj_body(x_ref, g_ref, *refs):
    na = len(IN_ALIGNED_DST)
    w_refs, wt_ref = refs[:na], refs[na]
    o16_ref, o32_ref, ut_ref, xn_ref = refs[na + 1:]
    xn_ref[...] = _rms(x_ref[...], g_ref[...]).astype(BF16)
    blk = lambda j: slice(j * D, (j + 1) * D)
    for w_ref, dst in zip(w_refs, IN_ALIGNED_DST):
        y = _dot(xn_ref[...], w_ref[...])
        if dst is None:
            o32_ref[:, blk(C32_F)] = y
        else:
            o16_ref[:, blk(dst)] = y.astype(o16_ref.dtype)
    y = _dot(xn_ref[...], wt_ref[:, blk(0)])
    o16_ref[:, blk(C16_S5)] = y.astype(o16_ref.dtype)
    for c in range(ut_ref.shape[0]):
        ut_ref[c] = y[c * CHUNK:(c + 1) * CHUNK, :].T.astype(ut_ref.dtype)
    for j in range(3):
        o16_ref[:, blk(C16_GATE + j)] = _dot(xn_ref[...], wt_ref[:, blk(1 + j)]).astype(o16_ref.dtype)
    o32_ref[:, D:N32] = _dot(xn_ref[...], wt_ref[:, 4 * D:4 * D + 4 * LANE])


def _inproj(h, g, w_all, layer, w_tail, tm):
    t = h.shape[0]
    row = lambda i: (i, 0)
    fixed = lambda i: (0, 0)
    once = dict(pipeline_mode=pl.Buffered(1))
    aligned = [pl.BlockSpec((None, D, D), functools.partial(lambda j, i: (layer, 0, j), j), **once)
               for j in range(len(IN_ALIGNED_DST))]
    return pl.pallas_call(
        _inproj_body,
        grid=(t // tm,),
        in_specs=[pl.BlockSpec((tm, D), row), pl.BlockSpec((1, D), fixed)] + aligned
                 + [pl.BlockSpec(w_tail.shape, fixed, **once)],
        out_specs=[pl.BlockSpec((tm, N16), row), pl.BlockSpec((tm, N32), row),
                   pl.BlockSpec((tm // CHUNK, D, CHUNK), lambda i: (i, 0, 0))],
        out_shape=[jax.ShapeDtypeStruct((t, N16), BF16), jax.ShapeDtypeStruct((t, N32), F32),
                   jax.ShapeDtypeStruct((t // CHUNK, D, CHUNK), BF16)],
        scratch_shapes=[pltpu.VMEM((tm, D), BF16)],
        compiler_params=_cparams("parallel"),
        name="inproj",
    )(h, g, *([w_all] * len(IN_ALIGNED_DST)), w_tail)


def _tail_body(final, h_ref, ohg_ref, orw_ref, y_ref, u_ref, g0_ref, g1_ref, g2_ref,
               d_ref, gw_ref, gb_ref, wb_ref, wo_ref, fn_ref, wg_ref, wu_ref, wd_ref, fg_ref,
               o_ref, yt_ref, s5_ref, xn_ref, acc_ref):
    for c in range(y_ref.shape[0]):
        yt_ref[c * CHUNK:(c + 1) * CHUNK, :] = y_ref[c].astype(F32).T
    for i in range(D // LANE):
        ln = slice(i * LANE, (i + 1) * LANE)
        y = yt_ref[:, ln] + d_ref[:, ln] * u_ref[:, ln].astype(F32)
        zg = 0.5 * y * (1.0 + jnp.tanh(math.sqrt(2.0 / math.pi) * (y + 0.044715 * (y * y * y))))
        gate = _dot(zg.astype(BF16), gw_ref[i]) + gb_ref[:, ln]
        s5_ref[:, ln] = (zg * _sigmoid(gate)).astype(BF16)

    m = _sigmoid(g0_ref[...].astype(F32)) * _dot(ohg_ref[...], wb_ref[0])
    m = m + _sigmoid(g1_ref[...].astype(F32)) * _dot(orw_ref[...], wb_ref[1])
    m = m + _sigmoid(g2_ref[...].astype(F32)) * _dot(s5_ref[...], wb_ref[2])
    x = h_ref[...] + _dot(m.astype(BF16), wo_ref[...])

    xn_ref[...] = _rms(x, fn_ref[...]).astype(BF16)
    for c in range(FFN_HIDDEN // FFN_CHUNK):
        sl = slice(c * FFN_CHUNK, (c + 1) * FFN_CHUNK)
        a = _dot(xn_ref[...], wg_ref[:, sl])
        b = _dot(xn_ref[...], wu_ref[:, sl])
        part = _dot((_silu(a) * b).astype(BF16), wd_ref[sl, :])
        if c == 0:
            acc_ref[...] = x + part
        else:
            acc_ref[...] += part
    out = acc_ref[...]
    if final:
        out = _rms(out, fg_ref[...])
    o_ref[...] = out


def _tail(h, o_hg, o_rw, ycore, p16, d, glu_tiles, glu_b, wb, wo, fn, wg, wu, wd, fg, final, tm):
    t = h.shape[0]
    row = lambda i: (i, 0)
    once = dict(pipeline_mode=pl.Buffered(1))
    vec = pl.BlockSpec((1, D), lambda i: (0, 0))
    rowblk = pl.BlockSpec((tm, D), row)
    col = lambda j: pl.BlockSpec((tm, D), lambda i: (i, j))
    return pl.pallas_call(
        functools.partial(_tail_body, final),
        grid=(t // tm,),
        in_specs=[rowblk, rowblk, rowblk,
                  pl.BlockSpec((tm // CHUNK, D, CHUNK), lambda i: (i, 0, 0)),
                  col(C16_S5), col(C16_GATE), col(C16_GATE + 1), col(C16_GATE + 2),
                  vec, pl.BlockSpec((D // LANE, LANE, LANE), lambda i: (0, 0, 0), **once), vec,
                  pl.BlockSpec((3, D, D), lambda i: (0, 0, 0), **once),
                  pl.BlockSpec((D, D), lambda i: (0, 0), **once),
                  vec,
                  pl.BlockSpec((D, FFN_HIDDEN), lambda i: (0, 0), **once),
                  pl.BlockSpec((D, FFN_HIDDEN), lambda i: (0, 0), **once),
                  pl.BlockSpec((FFN_HIDDEN, D), lambda i: (0, 0), **once),
                  vec],
        out_specs=rowblk,
        out_shape=jax.ShapeDtypeStruct((t, D), F32),
        scratch_shapes=[pltpu.VMEM((tm, D), F32), pltpu.VMEM((tm, D), BF16),
                        pltpu.VMEM((tm, D), BF16), pltpu.VMEM((tm, D), F32)],
        compiler_params=_cparams("parallel"),
        name="tail",
    )(h, o_hg, o_rw, ycore, p16, p16, p16, p16, d, glu_tiles, glu_b, wb, wo, fn, wg, wu, wd, fg)


HG_LEVELS = (32, 16, 8, 4, 2, 1)


def _hg_dmat():
    t = np.arange(CHUNK)
    tri = (t[None, :] <= t[:, None]).astype(np.float32)
    blocks = [tri]
    for m in HG_LEVELS:
        r = (t // (2 * m)) * (2 * m) + m - 1
        blocks.append(tri - tri[r])
    d = np.concatenate(blocks, axis=0)
    return jnp.asarray(np.concatenate([d, d, d], axis=1), BF16)


def _hg_body(layer, q_ref, f_ref, i_ref, og_ref, lbl_ref, on_ref, dm_ref, o_ref, st_ref, d_ref):
    tb = q_ref.shape[0]

    @pl.when(pl.program_id(1) == 0)
    def _():
        st_ref[...] = jnp.zeros_like(st_ref)

    lg = lbl_ref[...]
    e = jnp.exp(lg - jnp.max(lg, axis=0, keepdims=True))
    sm = e / jnp.sum(e, axis=0, keepdims=True)
    lb = jnp.zeros((1, D), F32)
    for l in range(1, layer + 1):
        lb = lb + sm[l:l + 1, :]
    lb = jnp.maximum(lb, 0.0)
    log_lb = jnp.log(lb)
    log_1mlb = jnp.log1p(-lb)
    one_mlb = 1.0 - lb

    ti = lax.broadcasted_iota(jnp.int32, (CHUNK, CHUNK), 0)
    si = lax.broadcasted_iota(jnp.int32, (CHUNK, CHUNK), 1)
    xr = ti ^ si
    lower = ti > si
    pair_masks = [lower & (xr >= m) & (xr < 2 * m) for m in HG_LEVELS]
    diag_mask = ti == si
    trow = lax.broadcasted_iota(jnp.int32, (CHUNK, LANE), 0)
    second = [(trow & m) != 0 for m in HG_LEVELS]

    def chunk_step(c, carry):
        r0 = pl.multiple_of(c * CHUNK, CHUNK)
        rows = pl.ds(r0, CHUNK)
        z = f_ref[rows, :]
        ez = jnp.exp(_neg_abs(z))
        lsig = jnp.minimum(z, 0.0) - jnp.log(1.0 + ez)
        cterm = log_1mlb + lsig
        mx = jnp.maximum(log_lb, cterm)
        logf = mx + jnp.log(1.0 + jnp.exp(_neg_abs(log_lb - cterm)))
        d_ref[...] = _exact_dot01(dm_ref[...], logf * LOG2E)
        kk_all = one_mlb * jnp.where(z >= 0.0, ez, 1.0) / (1.0 + ez)
        heads = range(HG_HEADS)
        lns = [slice(h * LANE, (h + 1) * LANE) for h in heads]
        q = [_silu(q_ref[rows, lns[h]].astype(F32)) for h in heads]
        k = [kk_all[:, lns[h]] for h in heads]
        s0 = [_dot_nt(q[h].astype(BF16), k[h].astype(BF16)) for h in heads]
        scores = [jnp.where(diag_mask, s0[h], 0.0) for h in heads]
        for li in range(len(HG_LEVELS)):
            x = [(jnp.where(second[li], q[h], k[h])
                  * jnp.exp2(_neg_abs(d_ref[(li + 1) * CHUNK:(li + 2) * CHUNK, lns[h]]))).astype(BF16)
                 for h in heads]
            sl = [_dot_nt(x[h], x[h]) for h in heads]
            scores = [jnp.where(pair_masks[li], sl[h], scores[h]) for h in heads]
        o = []
        for h in heads:
            b = d_ref[0:CHUNK, lns[h]]
            qe = (q[h] * jnp.exp2(b)).astype(BF16)
            o.append(_dot(scores[h].astype(BF16), i_ref[rows, lns[h]].astype(BF16))
                     + _dot_nt(qe, st_ref[h].astype(BF16)))
        for h in heads:
            b = d_ref[0:CHUNK, lns[h]]
            b_last = b[CHUNK - 1:CHUNK, :]
            kd = (k[h] * jnp.exp2(b_last - b)).astype(BF16)
            vt = i_ref[rows, lns[h]].astype(F32).T.astype(BF16)
            st_ref[h] = st_ref[h] * jnp.exp2(b_last) + _dot(vt, kd)
        for h in heads:
            oh = o[h] * lax.rsqrt(jnp.mean(o[h] * o[h], axis=-1, keepdims=True) + NORM_EPS)
            oh = oh * on_ref[:, lns[h]] * _silu(og_ref[rows, lns[h]].astype(F32))
            o_ref[rows, lns[h]] = oh.astype(o_ref.dtype)
        return carry

    lax.fori_loop(0, tb // CHUNK, chunk_step, 0)


def _hgrn2(p16, p32, lb_logits, onorm, layer, batch, seq, tb):
    t = p16.shape[0]
    nb = seq // tb
    col = lambda j: (lambda b, s: (b * nb + s, j))
    fixed = lambda b, s: (0, 0)
    return pl.pallas_call(
        functools.partial(_hg_body, layer),
        grid=(batch, nb),
        in_specs=[
            pl.BlockSpec((tb, D), col(C16_Q)),
            pl.BlockSpec((tb, D), col(C32_F)),
            pl.BlockSpec((tb, D), col(C16_I)),
            pl.BlockSpec((tb, D), col(C16_OG)),
            pl.BlockSpec(lb_logits.shape, fixed),
            pl.BlockSpec((1, D), fixed),
            pl.BlockSpec((7 * CHUNK, 3 * CHUNK), fixed),
        ],
        out_specs=pl.BlockSpec((tb, D), lambda b, s: (b * nb + s, 0)),
        out_shape=jax.ShapeDtypeStruct((t, D), BF16),
        scratch_shapes=[pltpu.VMEM((HG_HEADS, LANE, LANE), F32),
                        pltpu.VMEM((7 * CHUNK, D), F32)],
        compiler_params=_cparams("parallel", "arbitrary"),
        name="hgrn2",
    )(p16, p32, p16, p16, lb_logits, onorm, _hg_dmat())


S5_ROW = CHUNK * S5_GROUP


def _s5_consts():
    tj = np.arange(S5_ROW)
    e_tj = (tj[None, :] % CHUNK == np.arange(CHUNK)[:, None]).astype(np.float32)
    f_tj = (tj[None, :] // CHUNK == np.arange(S5_GROUP)[:, None]).astype(np.float32)
    return (jnp.asarray(e_tj, BF16), jnp.asarray(f_tj, BF16),
            jnp.asarray(e_tj.T, BF16), jnp.asarray(f_tj.T, BF16))


def _right01(x, e):
    hi, mid, lo = _split3(x)
    return _dot(hi, e) + _dot(mid, e) + _dot(lo, e)


def _left01(e, x):
    hi, mid, lo = _split3(x)
    return _dot(e, hi) + _dot(e, mid) + _dot(e, lo)


def _dot_f32(a, b):
    ah, am, al = _split3(a)
    bh, bm, bl = _split3(b)
    return (_dot(ah, bh) + _dot(ah, bm) + _dot(am, bh)
            + _dot(ah, bl) + _dot(al, bh) + _dot(am, bm))


def _s5_prep_body(lrr_ref, lir_ref, lrc_ref, lic_ref, ls_ref, brt_ref, bit_ref, crt_ref, cit_ref,
                  etj_ref, ftj_ref, esh_ref, fsh_ref,
                  wt_ref, wi_ref, wis_ref, wo_ref, a1_ref, a2_ref):
    dt = jnp.exp(ls_ref[...])
    lam_re, lam_im = lrr_ref[...], lir_ref[...]
    lr_row, li_row = lam_re * dt, lam_im * dt
    lr_col, li_col = lrc_ref[...] * dt, lic_ref[...] * dt
    re_half = lax.broadcasted_iota(jnp.int32, (1, 2 * S5_STATE), 1) < S5_STATE

    mag = jnp.exp(lr_row)
    ab_re, ab_im = mag * jnp.cos(li_row), mag * jnp.sin(li_row)
    den = lam_re * lam_re + lam_im * lam_im
    nr, ni = ab_re - 1.0, ab_im
    coef_re = (nr * lam_re + ni * lam_im) / den
    coef_im = (ni * lam_re - nr * lam_im) / den
    bb_re = coef_re * brt_ref[...] - coef_im * bit_ref[...]
    bb_im = coef_re * bit_ref[...] + coef_im * brt_ref[...]

    def powers(lr, li, n):
        m = jnp.exp(lr * n)
        return m * jnp.cos(li * n), m * jnp.sin(li * n)

    tau = lax.broadcasted_iota(jnp.int32, (1, CHUNK), 1).astype(F32)
    p0_re, p0_im = powers(lr_col, li_col, tau)
    ac_re, ac_im = powers(lr_col, li_col, 1.0)
    p1_re, p1_im = p0_re * ac_re - p0_im * ac_im, p0_re * ac_im + p0_im * ac_re
    c_re = _right01(crt_ref[...], ftj_ref[...])
    c_im = _right01(cit_ref[...], ftj_ref[...])
    p0e_re, p0e_im = _right01(p0_re, etj_ref[...]), _right01(p0_im, etj_ref[...])
    p1e_re, p1e_im = _right01(p1_re, etj_ref[...]), _right01(p1_im, etj_ref[...])

    m_stack = jnp.concatenate([c_re * p0e_re - c_im * p0e_im, c_re * p0e_im + c_im * p0e_re], axis=0)
    r0 = _dot_f32(jnp.where(re_half, bb_re, -bb_im), m_stack)
    t_lane = lax.broadcasted_iota(jnp.int32, (CHUNK, S5_ROW), 1) & (CHUNK - 1)
    s_row = lax.broadcasted_iota(jnp.int32, (CHUNK, S5_ROW), 0)
    for h in range(S5_GROUP):
        base = jnp.broadcast_to(r0[h:h + 1, :], (CHUNK, S5_ROW))
        rolled = pltpu.roll(base, 0, 1, stride=1, stride_axis=0)
        wt_ref[h * CHUNK:(h + 1) * CHUNK, :] = jnp.where(t_lane >= s_row, rolled, 0.0).astype(BF16)

    wo_ref[0:S5_STATE, :] = (c_re * p1e_re - c_im * p1e_im).astype(BF16)
    wo_ref[S5_STATE:2 * S5_STATE, :] = (-(c_re * p1e_im + c_im * p1e_re)).astype(BF16)

    back = (CHUNK - 1.0) - lax.broadcasted_iota(jnp.int32, (CHUNK, 1), 0).astype(F32)
    pt_re, pt_im = powers(lr_row, li_row, back)
    pte_re, pte_im = _left01(esh_ref[...], pt_re), _left01(esh_ref[...], pt_im)
    bbe_re, bbe_im = _left01(fsh_ref[...], bb_re), _left01(fsh_ref[...], bb_im)
    w_re = pte_re * bbe_re - pte_im * bbe_im
    w_im = pte_re * bbe_im + pte_im * bbe_re
    wi_ref[...] = jnp.where(re_half, w_re, w_im).astype(BF16)
    wis_ref[...] = jnp.where(re_half, w_im, w_re).astype(BF16)

    k = lax.broadcasted_iota(jnp.int32, (8, 1), 0)
    n = (CHUNK * (1 << k)).astype(F32)
    an_re, an_im = powers(lr_row, li_row, n)
    a1_ref[...] = an_re
    a2_ref[...] = jnp.where(re_half, -an_im, an_im)


def _s5_prep(lam_re, lam_im, log_step, b_re, b_im, c_re, c_im):
    nl = lam_re.shape[0]
    ns = 2 * S5_STATE
    g4 = lambda r, c: pl.BlockSpec((None, None, r, c), lambda l, g: (l, g, 0, 0))
    cst = lambda r, c: pl.BlockSpec((r, c), lambda l, g: (0, 0))
    sds = lambda r, c, dt: jax.ShapeDtypeStruct((nl, S5_GROUPS, r, c), dt)
    twice = lambda a: jnp.concatenate([a, a], axis=-1)
    e_tj, f_tj, e_sh, f_sh = _s5_consts()
    return pl.pallas_call(
        _s5_prep_body,
        grid=(nl, S5_GROUPS),
        in_specs=[g4(1, ns), g4(1, ns), g4(S5_STATE, 1), g4(S5_STATE, 1), g4(1, 1),
                  g4(S5_GROUP, ns), g4(S5_GROUP, ns), g4(S5_STATE, S5_GROUP), g4(S5_STATE, S5_GROUP),
                  cst(CHUNK, S5_ROW), cst(S5_GROUP, S5_ROW), cst(S5_ROW, CHUNK), cst(S5_ROW, S5_GROUP)],
        out_specs=[g4(S5_ROW, S5_ROW), g4(S5_ROW, ns), g4(S5_ROW, ns), g4(ns, S5_ROW), g4(8, ns), g4(8, ns)],
        out_shape=[sds(S5_ROW, S5_ROW, BF16), sds(S5_ROW, ns, BF16), sds(S5_ROW, ns, BF16),
                   sds(ns, S5_ROW, BF16), sds(8, ns, F32), sds(8, ns, F32)],
        compiler_params=_cparams("parallel", "parallel"),
        name="s5_prep",
    )(twice(lam_re)[:, :, None, :], twice(lam_im)[:, :, None, :], lam_re[:, :, :, None], lam_im[:, :, :, None],
      log_step[:, :, None, None],
      twice(jnp.swapaxes(b_re, 2, 3)), twice(jnp.swapaxes(b_im, 2, 3)),
      jnp.swapaxes(c_re, 2, 3), jnp.swapaxes(c_im, 2, 3),
      e_tj, f_tj, e_sh, f_sh)


def _s5_body(nc, u_ref, wt_ref, wi_ref, wis_ref, wo_ref, a1_ref, a2_ref, y_ref):
    u = u_ref[...]
    m = u.shape[0]
    x = _dot(u, wi_ref[...])
    xs = _dot(u, wis_ref[...])
    chunk = lax.broadcasted_iota(jnp.int32, (m, 2 * S5_STATE), 0) % nc

    def shifted(z, n):
        return jnp.where(chunk >= n, pltpu.roll(z, n, 0), 0.0)

    k = 0
    while (1 << k) < nc:
        a1, a2 = a1_ref[k:k + 1, :], a2_ref[k:k + 1, :]
        sx, sxs = shifted(x, 1 << k), shifted(xs, 1 << k)
        x, xs = x + a1 * sx + a2 * sxs, xs + a1 * sxs - a2 * sx
        k += 1
    y = _dot(u, wt_ref[...]) + _dot(shifted(x, 1).astype(BF16), wo_ref[...])
    y_ref[...] = y.astype(y_ref.dtype)


def _s5_core(u_t, ops, layer, nc):
    wt, wi, wis, wo, a1, a2 = ops
    m = u_t.shape[0]
    ns = 2 * S5_STATE
    grp = pl.BlockSpec((m, S5_ROW), lambda g: (0, g))
    lg = lambda r, c: pl.BlockSpec((None, None, r, c), lambda g: (layer, g, 0, 0))
    return pl.pallas_call(
        functools.partial(_s5_body, nc),
        grid=(S5_GROUPS,),
        in_specs=[grp, lg(S5_ROW, S5_ROW), lg(S5_ROW, ns), lg(S5_ROW, ns),
                  lg(ns, S5_ROW), lg(8, ns), lg(8, ns)],
        out_specs=grp,
        out_shape=jax.ShapeDtypeStruct((m, S5_GROUPS * S5_ROW), BF16),
        compiler_params=_cparams("parallel"),
        name="s5_core",
    )(u_t, wt, wi, wis, wo, a1, a2)


def _ones_bd():
    i = np.arange(LANE)
    return jnp.asarray((i[:, None] // RW_HEAD == i[None, :] // RW_HEAD).astype(np.float32), BF16)


def _segsum(x, ones_bd):
    hi = x.astype(BF16)
    lo = (x - hi.astype(F32)).astype(BF16)
    return _dot(hi, ones_bd) + _dot(lo, ones_bd)


def _rw_prep_body(first, r_ref, k_ref, v_ref, lo_ref, vf_ref, mu_ref, mul_ref, w0_ref, w2_ref, a0_ref, a2_ref,
                  g2_ref, v0_ref, v1_ref, v2_ref, kk_ref, ka_ref, ones_ref,
                  ro_ref, lw_ref, ko_ref, vo_ref, kko_ref, bo_ref, go_ref, cr_ref, cl_ref):
    tb = r_ref.shape[0]

    @pl.when(pl.program_id(1) == 0)
    def _():
        cr_ref[...] = jnp.zeros_like(cr_ref)
        cl_ref[...] = jnp.zeros_like(cl_ref)

    def mix(z, carry, mu):
        row = lax.broadcasted_iota(jnp.int32, z.shape, 0)
        zs = jnp.where(row == 0, carry, pltpu.roll(z, 1, 0))
        return z + mu * (zs - z)

    def softplus(x):
        return jnp.maximum(x, 0.0) + jnp.log(1.0 + jnp.exp(_neg_abs(x)))

    zl = lo_ref[...]
    lora = mix(zl, cl_ref[0:1, :], mul_ref[...])
    cl_ref[0:1, :] = zl[tb - 1:tb, :]
    wd = jnp.tanh(lora[:, 0:LANE]).astype(BF16)
    ad = lora[:, LANE:2 * LANE].astype(BF16)
    gd = _sigmoid(lora[:, 2 * LANE:4 * LANE]).astype(BF16)
    w_log = -softplus(-(w0_ref[...] + _dot(wd, w2_ref[...]))) - 0.5
    lw_ref[...] = -jnp.exp(w_log)
    a = _sigmoid(a0_ref[...] + _dot(ad, a2_ref[...]))
    go_ref[...] = _dot(gd, g2_ref[...]).astype(go_ref.dtype)

    zr = r_ref[...].astype(F32)
    ro_ref[...] = mix(zr, cr_ref[0:1, :], mu_ref[0:1, :]).astype(ro_ref.dtype)
    cr_ref[0:1, :] = zr[tb - 1:tb, :]
    zk = k_ref[...].astype(F32)
    k = mix(zk, cr_ref[1:2, :], mu_ref[1:2, :])
    cr_ref[1:2, :] = zk[tb - 1:tb, :]
    zv = v_ref[...].astype(F32)
    v = mix(zv, cr_ref[2:3, :], mu_ref[2:3, :])
    cr_ref[2:3, :] = zv[tb - 1:tb, :]
    if not first:
        lowr = _dot(v.astype(BF16), v1_ref[...]).astype(BF16)
        v = v + (vf_ref[...].astype(F32) - v) * _sigmoid(v0_ref[...] + _dot(lowr, v2_ref[...]))
    vo_ref[...] = v.astype(vo_ref.dtype)

    kk = k * kk_ref[...]
    ss = jnp.concatenate([_segsum(kk[:, i * LANE:(i + 1) * LANE] * kk[:, i * LANE:(i + 1) * LANE], ones_ref[...])
                          for i in range(D // LANE)], axis=1)
    kk = kk * lax.rsqrt(jnp.maximum(ss, 1e-24))
    kko_ref[...] = kk.astype(kko_ref.dtype)
    bo_ref[...] = (kk * a).astype(bo_ref.dtype)
    ko_ref[...] = (k * (1.0 + (a - 1.0) * ka_ref[...])).astype(ko_ref.dtype)


def _rw_prep(p16, p32, v_first, p, first, batch, seq, tb):
    t = p16.shape[0]
    nb = seq // tb
    col = lambda j: (lambda b, s: (b * nb + s, j))
    fixed = lambda b, s: (0, 0)
    vec = pl.BlockSpec((1, D), fixed)
    rowblk = pl.BlockSpec((tb, D), lambda b, s: (b * nb + s, 0))
    out = jax.ShapeDtypeStruct((t, D), BF16)
    return pl.pallas_call(
        functools.partial(_rw_prep_body, first),
        grid=(batch, nb),
        in_specs=[pl.BlockSpec((tb, D), col(C16_R)), pl.BlockSpec((tb, D), col(C16_K)),
                  pl.BlockSpec((tb, D), col(C16_V)),
                  pl.BlockSpec((tb, 4 * LANE), col(C32_LORA)),
                  rowblk,
                  pl.BlockSpec((3, D), fixed), pl.BlockSpec((1, 4 * LANE), fixed),
                  vec, pl.BlockSpec((LANE, D), fixed), vec, pl.BlockSpec((LANE, D), fixed),
                  pl.BlockSpec((2 * LANE, D), fixed),
                  vec, pl.BlockSpec((D, LANE), fixed), pl.BlockSpec((LANE, D), fixed),
                  vec, vec, pl.BlockSpec((LANE, LANE), fixed)],
        out_specs=[rowblk] * 7,
        out_shape=[out, jax.ShapeDtypeStruct((t, D), F32)] + [out] * 5,
        scratch_shapes=[pltpu.VMEM((8, D), F32), pltpu.VMEM((8, 4 * LANE), F32)],
        compiler_params=_cparams("parallel", "arbitrary"),
        name="rw_prep",
    )(p16, p16, p16, p32, v_first, p["mu"], p["mu_lora"], p["w0"], p["w2"], p["a0"], p["a2"], p["g2"],
      p["v0"], p["v1"], p["v2"], p["k_k"], p["k_a"], _ones_bd())


RW_LEVELS = (2, 4, 8, 16, 32)


def _rwc_body(r_ref, lw_ref, k_ref, v_ref, kk_ref, b_ref, g_ref, rk_ref, lnw_ref, lnb_ref, tri_ref, ones_ref,
              o_ref, ht_ref, lg_ref):
    nbat, tb = r_ref.shape[0], r_ref.shape[1]
    ntile = D // LANE

    @pl.when(pl.program_id(1) == 0)
    def _():
        ht_ref[...] = jnp.zeros_like(ht_ref)

    t64 = lax.broadcasted_iota(jnp.int32, (CHUNK, LANE), 0)
    l64 = lax.broadcasted_iota(jnp.int32, (CHUNK, LANE), 1)
    s64 = l64 & (RW_HEAD - 1)
    strict = t64 > s64
    incl = t64 >= s64
    eye = t64 == s64
    xr = t64 ^ s64
    lvl1 = strict & (xr == 1)
    lvl = [strict & (xr >= m) & (xr < 2 * m) for m in RW_LEVELS]
    head0 = l64 < RW_HEAD
    r128 = lax.broadcasted_iota(jnp.int32, (LANE, LANE), 0)
    l128 = lax.broadcasted_iota(jnp.int32, (LANE, LANE), 1)
    same_head = (r128 >> 6) == (l128 >> 6)

    def stack(x):
        return jnp.concatenate([jnp.where(head0, x, 0.0), jnp.where(head0, 0.0, x)], axis=0).astype(BF16)

    def chunk_step(c, carry):
        rows = pl.ds(pl.multiple_of(c * CHUNK, CHUNK), CHUNK)
        for bi in range(nbat):
            lg_ref[bi] = _exact_dot01(tri_ref[...], lw_ref[bi, rows, :])
        probs = [(bi, j) for bi in range(nbat) for j in range(ntile)]
        n = range(len(probs))

        def ld(ref, p):
            return ref[probs[p][0], rows, probs[p][1] * LANE:(probs[p][1] + 1) * LANE].astype(F32)

        def lgp(p):
            return lg_ref[probs[p][0], :, probs[p][1] * LANE:(probs[p][1] + 1) * LANE]

        a_t, r_t, a_ak, a_rk, l_ab, a_rb, x = [], [], [], [], [], [], []
        for p in n:
            lg = lgp(p)
            e_neg = jnp.exp(-lg)
            a_t.append(-ld(kk_ref, p) * jnp.exp(lg - ld(lw_ref, p)))
            r_t.append(ld(r_ref, p) * jnp.exp(lg))
            lhs = jnp.concatenate([a_t[p], r_t[p]], axis=0).astype(BF16)
            kb_neg = jnp.concatenate([stack(ld(k_ref, p) * e_neg), stack(ld(b_ref, p) * e_neg)], axis=0)
            a_kb = _dot_nt(lhs, kb_neg)
            a_ak.append(jnp.where(strict, a_kb[0:CHUNK, 0:LANE], 0.0))
            a_rk.append(jnp.where(incl, a_kb[CHUNK:2 * CHUNK, 0:LANE], 0.0))
            l_ab.append(jnp.where(strict, a_kb[0:CHUNK, LANE:2 * LANE], 0.0))
            a_rb.append(jnp.where(incl, a_kb[CHUNK:2 * CHUNK, LANE:2 * LANE], 0.0))
            x.append(jnp.where(eye, 1.0, 0.0) + jnp.where(lvl1, l_ab[p], 0.0))

        for li in range(len(RW_LEVELS)):
            z = [_dot(x[p].astype(BF16), stack(jnp.where(lvl[li], l_ab[p], 0.0))) for p in n]
            x = [x[p] + _dot(z[p].astype(BF16), stack(x[p])) for p in n]

        v_st = [stack(ld(v_ref, p)) for p in n]
        mv = [_dot(a_ak[p].astype(BF16), v_st[p]) for p in n]
        u0w = [_dot(x[p].astype(BF16), jnp.concatenate([stack(mv[p]), stack(a_t[p])], axis=1)) for p in n]
        htb = [ht_ref[p].astype(BF16) for p in n]
        u = [u0w[p][:, 0:LANE] + _dot_nt(u0w[p][:, LANE:2 * LANE].astype(BF16), htb[p]) for p in n]
        y = [_dot_nt(r_t[p].astype(BF16), htb[p]) + _dot(a_rk[p].astype(BF16), v_st[p])
             + _dot(a_rb[p].astype(BF16), stack(u[p])) for p in n]
        for p in n:
            lg = lgp(p)
            lg_end = lg[CHUNK - 1:CHUNK, :]
            e_end = jnp.exp(lg_end - lg)
            vu = jnp.concatenate([ld(v_ref, p), u[p]], axis=0)
            kb = jnp.concatenate([ld(k_ref, p) * e_end, ld(b_ref, p) * e_end], axis=0).astype(BF16)
            upd = _dot(vu.T.astype(BF16), kb)
            ht_ref[p] = ht_ref[p] * jnp.exp(lg_end) + jnp.where(same_head, upd, 0.0)

        ones = ones_ref[...]
        sl = lambda a, p: a[p * CHUNK:(p + 1) * CHUNK]
        mean = _segsum(jnp.concatenate(y, axis=0), ones) * (1.0 / RW_HEAD)
        yc = [y[p] - sl(mean, p) for p in n]
        var = _segsum(jnp.concatenate([c * c for c in yc], axis=0), ones) * (1.0 / RW_HEAD)
        rk = [ld(r_ref, p) * ld(k_ref, p) * rk_ref[:, probs[p][1] * LANE:(probs[p][1] + 1) * LANE] for p in n]
        bonus = _segsum(jnp.concatenate(rk, axis=0), ones)
        for p in n:
            bi, j = probs[p]
            ln = slice(j * LANE, (j + 1) * LANE)
            yn = yc[p] * lax.rsqrt(sl(var, p) + RW_GN_EPS) * lnw_ref[:, ln] + lnb_ref[:, ln]
            o_ref[bi, rows, ln] = ((yn + sl(bonus, p) * ld(v_ref, p)) * ld(g_ref, p)).astype(o_ref.dtype)
        return carry

    lax.fori_loop(0, tb // CHUNK, chunk_step, 0)


def _rwc_core(feats, r_k, ln_w, ln_b, batch, seq, nbat, tb):
    t = feats[0].shape[0]
    fixed = lambda b, s: (0, 0)
    vec = pl.BlockSpec((1, D), fixed)
    blk = pl.BlockSpec((nbat, tb, D), lambda b, s: (b, s, 0))
    tri = np.tril(np.ones((CHUNK, CHUNK), np.float32))
    tri3 = jnp.asarray(np.concatenate([tri, tri, tri], axis=1), BF16)
    out = pl.pallas_call(
        _rwc_body,
        grid=(batch // nbat, seq // tb),
        in_specs=[blk] * 7 + [vec, vec, vec, pl.BlockSpec((CHUNK, 3 * CHUNK), fixed),
                              pl.BlockSpec((LANE, LANE), fixed)],
        out_specs=blk,
        out_shape=jax.ShapeDtypeStruct((batch, seq, D), BF16),
        scratch_shapes=[pltpu.VMEM((nbat * (D // LANE), LANE, LANE), F32), pltpu.VMEM((nbat, CHUNK, D), F32)],
        compiler_params=_cparams("parallel", "arbitrary"),
        name="rw_core",
    )(*[f.reshape(batch, seq, D) for f in feats], r_k, ln_w, ln_b, tri3, _ones_bd())
    return out.reshape(t, D)


def _mix_body(layer, q_ref, f_ref, i_ref, og_ref, lbl_ref, on_ref, dm_ref,
              r_ref, lw_ref, k_ref, v_ref, kk_ref, b_ref, g_ref, rk_ref, lnw_ref, lnb_ref, tri_ref, ones_ref,
              ohg_ref, orw_ref, st_ref, d_ref, ht_ref, lg_ref):
    nbat, tb = r_ref.shape[0], r_ref.shape[1]
    ntile = D // LANE

    @pl.when(pl.program_id(1) == 0)
    def _():
        st_ref[...] = jnp.zeros_like(st_ref)
        ht_ref[...] = jnp.zeros_like(ht_ref)

    lgt = lbl_ref[...]
    e = jnp.exp(lgt - jnp.max(lgt, axis=0, keepdims=True))
    sm = e / jnp.sum(e, axis=0, keepdims=True)
    lb = jnp.zeros((1, D), F32)
    for l in range(1, layer + 1):
        lb = lb + sm[l:l + 1, :]
    lb = jnp.maximum(lb, 0.0)
    log_lb = jnp.log(lb)
    log_1mlb = jnp.log1p(-lb)
    one_mlb = 1.0 - lb

    ti = lax.broadcasted_iota(jnp.int32, (CHUNK, CHUNK), 0)
    si = lax.broadcasted_iota(jnp.int32, (CHUNK, CHUNK), 1)
    pair_masks = [(ti > si) & ((ti ^ si) >= m) & ((ti ^ si) < 2 * m) for m in HG_LEVELS]
    diag_mask = ti == si
    trow = lax.broadcasted_iota(jnp.int32, (CHUNK, LANE), 0)
    second = [(trow & m) != 0 for m in HG_LEVELS]

    l64 = lax.broadcasted_iota(jnp.int32, (CHUNK, LANE), 1)
    s64 = l64 & (RW_HEAD - 1)
    strict = trow > s64
    incl = trow >= s64
    eye = trow == s64
    xr = trow ^ s64
    lvl1 = strict & (xr == 1)
    lvl = [strict & (xr >= m) & (xr < 2 * m) for m in RW_LEVELS]
    head0 = l64 < RW_HEAD
    r128 = lax.broadcasted_iota(jnp.int32, (LANE, LANE), 0)
    l128 = lax.broadcasted_iota(jnp.int32, (LANE, LANE), 1)
    same_head = (r128 >> 6) == (l128 >> 6)

    def stack(x):
        return jnp.concatenate([jnp.where(head0, x, 0.0), jnp.where(head0, 0.0, x)], axis=0).astype(BF16)

    def chunk_step(c, carry):
        rows = pl.ds(pl.multiple_of(c * CHUNK, CHUNK), CHUNK)
        probs = [(bi, j) for bi in range(nbat) for j in range(ntile)]
        n = range(len(probs))
        lanes = lambda p: slice(probs[p][1] * LANE, (probs[p][1] + 1) * LANE)

        def ld(ref, p):
            return ref[probs[p][0], rows, lanes(p)].astype(F32)

        def lgp(p):
            return lg_ref[probs[p][0], :, lanes(p)]

        def dlev(p, lev):
            return d_ref[probs[p][0], lev * CHUNK:(lev + 1) * CHUNK, lanes(p)]

        hg, rw = {}, {}

        def hg_gate():
            kk_all = []
            for bi in range(nbat):
                z = f_ref[bi, rows, :]
                ez = jnp.exp(_neg_abs(z))
                lsig = jnp.minimum(z, 0.0) - jnp.log(1.0 + ez)
                cterm = log_1mlb + lsig
                mx = jnp.maximum(log_lb, cterm)
                logf = mx + jnp.log(1.0 + jnp.exp(_neg_abs(log_lb - cterm)))
                d_ref[bi] = _exact_dot01(dm_ref[...], logf * LOG2E)
                kk_all.append(one_mlb * jnp.where(z >= 0.0, ez, 1.0) / (1.0 + ez))
            hg["q"] = [_silu(ld(q_ref, p)) for p in n]
            hg["k"] = [kk_all[probs[p][0]][:, lanes(p)] for p in n]

        def hg_diag():
            s0 = [_dot_nt(hg["q"][p].astype(BF16), hg["k"][p].astype(BF16)) for p in n]
            hg["scores"] = [jnp.where(diag_mask, s0[p], 0.0) for p in n]

        def hg_level(li):
            def run():
                x = [(jnp.where(second[li], hg["q"][p], hg["k"][p])
                      * jnp.exp2(_neg_abs(dlev(p, li + 1)))).astype(BF16) for p in n]
                sl = [_dot_nt(x[p], x[p]) for p in n]
                hg["scores"] = [jnp.where(pair_masks[li], sl[p], hg["scores"][p]) for p in n]
            return run

        def hg_out():
            o = []
            for p in n:
                qe = (hg["q"][p] * jnp.exp2(dlev(p, 0))).astype(BF16)
                o.append(_dot(hg["scores"][p].astype(BF16), i_ref[probs[p][0], rows, lanes(p)].astype(BF16))
                         + _dot_nt(qe, st_ref[p].astype(BF16)))
            hg["o"] = o

        def hg_state():
            for p in n:
                b = dlev(p, 0)
                b_last = b[CHUNK - 1:CHUNK, :]
                kd = (hg["k"][p] * jnp.exp2(b_last - b)).astype(BF16)
                vt = ld(i_ref, p).T.astype(BF16)
                st_ref[p] = st_ref[p] * jnp.exp2(b_last) + _dot(vt, kd)

        def hg_store():
            for p in n:
                o = hg["o"][p]
                o = o * lax.rsqrt(jnp.mean(o * o, axis=-1, keepdims=True) + NORM_EPS)
                o = o * on_ref[:, lanes(p)] * _silu(ld(og_ref, p))
                ohg_ref[probs[p][0], rows, lanes(p)] = o.astype(ohg_ref.dtype)

        def rw_amat():
            for bi in range(nbat):
                lg_ref[bi] = _exact_dot01(tri_ref[...], lw_ref[bi, rows, :])
            a_t, r_t, a_ak, a_rk, l_ab, a_rb, x = [], [], [], [], [], [], []
            for p in n:
                lg = lgp(p)
                e_neg = jnp.exp(-lg)
                a_t.append(-ld(kk_ref, p) * jnp.exp(lg - ld(lw_ref, p)))
                r_t.append(ld(r_ref, p) * jnp.exp(lg))
                lhs = jnp.concatenate([a_t[p], r_t[p]], axis=0).astype(BF16)
                kb_neg = jnp.concatenate([stack(ld(k_ref, p) * e_neg), stack(ld(b_ref, p) * e_neg)], axis=0)
                a_kb = _dot_nt(lhs, kb_neg)
                a_ak.append(jnp.where(strict, a_kb[0:CHUNK, 0:LANE], 0.0))
                a_rk.append(jnp.where(incl, a_kb[CHUNK:2 * CHUNK, 0:LANE], 0.0))
                l_ab.append(jnp.where(strict, a_kb[0:CHUNK, LANE:2 * LANE], 0.0))
                a_rb.append(jnp.where(incl, a_kb[CHUNK:2 * CHUNK, LANE:2 * LANE], 0.0))
                x.append(jnp.where(eye, 1.0, 0.0) + jnp.where(lvl1, l_ab[p], 0.0))
            rw.update(a_t=a_t, r_t=r_t, a_ak=a_ak, a_rk=a_rk, l_ab=l_ab, a_rb=a_rb, x=x)

        def rw_inv_a(li):
            def run():
                rw["z"] = [_dot(rw["x"][p].astype(BF16), stack(jnp.where(lvl[li], rw["l_ab"][p], 0.0))) for p in n]
            return run

        def rw_inv_b():
            rw["x"] = [rw["x"][p] + _dot(rw["z"][p].astype(BF16), stack(rw["x"][p])) for p in n]

        def rw_mv():
            rw["v_st"] = [stack(ld(v_ref, p)) for p in n]
            rw["mv"] = [_dot(rw["a_ak"][p].astype(BF16), rw["v_st"][p]) for p in n]

        def rw_u0w():
            rw["u0w"] = [_dot(rw["x"][p].astype(BF16),
                              jnp.concatenate([stack(rw["mv"][p]), stack(rw["a_t"][p])], axis=1)) for p in n]

        def rw_u():
            rw["htb"] = [ht_ref[p].astype(BF16) for p in n]
            rw["u"] = [rw["u0w"][p][:, 0:LANE] + _dot_nt(rw["u0w"][p][:, LANE:2 * LANE].astype(BF16), rw["htb"][p])
                       for p in n]

        def rw_y():
            rw["y"] = [_dot_nt(rw["r_t"][p].astype(BF16), rw["htb"][p])
                       + _dot(rw["a_rk"][p].astype(BF16), rw["v_st"][p])
                       + _dot(rw["a_rb"][p].astype(BF16), stack(rw["u"][p])) for p in n]

        def rw_state():
            for p in n:
                lg = lgp(p)
                lg_end = lg[CHUNK - 1:CHUNK, :]
                e_end = jnp.exp(lg_end - lg)
                vu = jnp.concatenate([ld(v_ref, p), rw["u"][p]], axis=0)
                kb = jnp.concatenate([ld(k_ref, p) * e_end, ld(b_ref, p) * e_end], axis=0).astype(BF16)
                upd = _dot(vu.T.astype(BF16), kb)
                ht_ref[p] = ht_ref[p] * jnp.exp(lg_end) + jnp.where(same_head, upd, 0.0)

        def rw_store():
            ones = ones_ref[...]
            y = rw["y"]
            sl = lambda a, p: a[p * CHUNK:(p + 1) * CHUNK]
            mean = _segsum(jnp.concatenate(y, axis=0), ones) * (1.0 / RW_HEAD)
            yc = [y[p] - sl(mean, p) for p in n]
            var = _segsum(jnp.concatenate([c_ * c_ for c_ in yc], axis=0), ones) * (1.0 / RW_HEAD)
            rk = [ld(r_ref, p) * ld(k_ref, p) * rk_ref[:, lanes(p)] for p in n]
            bonus = _segsum(jnp.concatenate(rk, axis=0), ones)
            for p in n:
                yn = yc[p] * lax.rsqrt(sl(var, p) + RW_GN_EPS) * lnw_ref[:, lanes(p)] + lnb_ref[:, lanes(p)]
                orw_ref[probs[p][0], rows, lanes(p)] = (
                    (yn + sl(bonus, p) * ld(v_ref, p)) * ld(g_ref, p)).astype(orw_ref.dtype)

        rw_stages = [rw_amat]
        for li in range(len(RW_LEVELS)):
            rw_stages += [rw_inv_a(li), rw_inv_b]
        rw_stages += [rw_mv, rw_u0w, rw_u, rw_y, rw_state, rw_store]
        hg_stages = [hg_gate, hg_diag] + [hg_level(li) for li in range(len(HG_LEVELS))] + [hg_out, hg_state, hg_store]
        for s in range(max(len(rw_stages), len(hg_stages))):
            if s < len(rw_stages):
                rw_stages[s]()
            if s < len(hg_stages):
                hg_stages[s]()
        return carry

    lax.fori_loop(0, tb // CHUNK, chunk_step, 0)


def _mix(p16, p32, feats, lb_logits, onorm, r_k, ln_w, ln_b, layer, batch, seq, nbat, tb):
    t = p16.shape[0]
    fixed = lambda b, s: (0, 0)
    vec = pl.BlockSpec((1, D), fixed)
    col = lambda j: pl.BlockSpec((nbat, tb, D), lambda b, s: (b, s, j))
    blk = col(0)
    tri = np.tril(np.ones((CHUNK, CHUNK), np.float32))
    tri3 = jnp.asarray(np.concatenate([tri, tri, tri], axis=1), BF16)
    p16_3, p32_3 = p16.reshape(batch, seq, -1), p32.reshape(batch, seq, -1)
    out = jax.ShapeDtypeStruct((batch, seq, D), BF16)
    o_hg, o_rw = pl.pallas_call(
        functools.partial(_mix_body, layer),
        grid=(batch // nbat, seq // tb),
        in_specs=[col(C16_Q), col(C32_F), col(C16_I), col(C16_OG),
                  pl.BlockSpec(lb_logits.shape, fixed), vec, pl.BlockSpec((7 * CHUNK, 3 * CHUNK), fixed)]
                 + [blk] * 7 + [vec, vec, vec, pl.BlockSpec((CHUNK, 3 * CHUNK), fixed),
                                pl.BlockSpec((LANE, LANE), fixed)],
        out_specs=[blk, blk],
        out_shape=[out, out],
        scratch_shapes=[pltpu.VMEM((nbat * HG_HEADS, LANE, LANE), F32), pltpu.VMEM((nbat, 7 * CHUNK, D), F32),
                        pltpu.VMEM((nbat * (D // LANE), LANE, LANE), F32), pltpu.VMEM((nbat, CHUNK, D), F32)],
        compiler_params=_cparams("parallel", "arbitrary"),
        name="mix",
    )(p16_3, p32_3, p16_3, p16_3, lb_logits, onorm, _hg_dmat(),
      *[f.reshape(batch, seq, D) for f in feats], r_k, ln_w, ln_b, tri3, _ones_bd())
    return o_hg.reshape(t, D), o_rw.reshape(t, D)


def _pack_w_tail(w):
    o_lora = len(IN_ALIGNED_DST) * D
    o_a, o_g = o_lora + LORA_W, o_lora + LORA_W + LORA_A
    o_s5 = o_g + LORA_G
    z = lambda n: jnp.zeros((D, n), w.dtype)
    return jnp.concatenate([w[:, o_s5:o_s5 + 4 * D], w[:, o_lora:o_a], z(LANE - LORA_W), w[:, o_a:o_g],
                            z(LANE - LORA_A), w[:, o_g:o_s5], z(2 * LANE - LORA_G)], axis=1)


def _pad_rows(w, n):
    return jnp.concatenate([w, jnp.zeros((n - w.shape[0], w.shape[1]), w.dtype)], axis=0)


def _pad_lanes(v, n):
    return jnp.concatenate([v, jnp.zeros((n - v.shape[0],), v.dtype)])


def kernel(x, mix_norm, w_in, hg_lb_logits, hg_onorm, rw_shift_mu, rw_w0, rw_w2, rw_a0, rw_a2, rw_g2, rw_v0, rw_v1, rw_v2, rw_k_k, rw_k_a, rw_r_k, rw_ln_w, rw_ln_b, s5_lambda_re, s5_lambda_im, s5_log_step, s5_b_re, s5_b_im, s5_c_re, s5_c_im, s5_d, s5_glu_w, s5_glu_b, w_branch, w_out, ffn_norm, ffn_w_gate, ffn_w_up, ffn_w_down, final_norm):
    batch, seq, _ = x.shape
    t = batch * seq
    nc = seq // CHUNK
    depth = w_in.shape[0]
    tm = min(512, t)
    tb = min(256, seq)
    h = x.reshape(t, D)
    row = lambda v: v.reshape(1, -1)

    s5_ops = _s5_prep(s5_lambda_re, s5_lambda_im, s5_log_step, s5_b_re, s5_b_im, s5_c_re, s5_c_im)
    w_in_bf = w_in.astype(BF16)
    v_first = None
    for l in range(depth):
        p16, p32, u_t = _inproj(h, row(mix_norm[l]), w_in_bf, l, _pack_w_tail(w_in_bf[l]), min(256, t))

        mu = rw_shift_mu[l]
        o1, o2, o3 = 3 * D + LORA_W, 3 * D + LORA_W + LORA_A, 3 * D + LORA_W + LORA_A + LORA_G
        first = l == 0
        rp = {
            "mu": mu[0:3 * D].reshape(3, D),
            "mu_lora": row(jnp.concatenate([_pad_lanes(mu[3 * D:o1], LANE), _pad_lanes(mu[o1:o2], LANE),
                                            _pad_lanes(mu[o2:o3], 2 * LANE)])),
            "w0": row(rw_w0[l]), "w2": _pad_rows(rw_w2[l], LANE).astype(BF16),
            "a0": row(rw_a0[l]), "a2": _pad_rows(rw_a2[l], LANE).astype(BF16),
            "g2": _pad_rows(rw_g2[l], 2 * LANE).astype(BF16),
            "v0": row(rw_v0[0 if first else l - 1]),
            "v1": jnp.concatenate([rw_v1[0 if first else l - 1],
                                   jnp.zeros((D, LANE - LORA_V), F32)], axis=1).astype(BF16),
            "v2": _pad_rows(rw_v2[0 if first else l - 1], LANE).astype(BF16),
            "k_k": row(rw_k_k[l]), "k_a": row(rw_k_a[l]),
        }
        feats = _rw_prep(p16, p32, p16 if first else v_first, rp, first, batch, seq, tb)
        if first:
            v_first = feats[3]
        o_hg, o_rw = _mix(p16, p32, feats, hg_lb_logits, row(hg_onorm[l]), row(rw_r_k[l]), row(rw_ln_w[l]),
                          row(rw_ln_b[l]), l, batch, seq, math.gcd(batch, 4), min(128, seq))

        y_t = _s5_core(u_t.reshape(t // CHUNK, S5_GROUPS * S5_ROW), s5_ops, l, nc)
        ycore = y_t.reshape(t // CHUNK, D, CHUNK)
        eye8 = jnp.eye(LANE // S5_GROUP, dtype=F32)
        gw = s5_glu_w[l].reshape(D // LANE, LANE // S5_GROUP, S5_GROUP, S5_GROUP)
        glu_tiles = jnp.einsum("igjk,gh->igjhk", gw, eye8).reshape(D // LANE, LANE, LANE).astype(BF16)
        h = _tail(h, o_hg, o_rw, ycore, p16, row(s5_d[l]), glu_tiles, row(s5_glu_b[l]),
                  w_branch[l].reshape(3, D, D).astype(BF16), w_out[l].astype(BF16),
                  row(ffn_norm[l]), ffn_w_gate[l].astype(BF16), ffn_w_up[l].astype(BF16),
                  ffn_w_down[l].astype(BF16), row(final_norm), l == depth - 1, tm)
    return h.reshape(batch, seq, D)
```

```python
import functools
import math

import numpy as np
import jax
import jax.numpy as jnp
from jax import lax
from jax.experimental import pallas as pl
from jax.experimental.pallas import tpu as pltpu

F32 = jnp.float32
BF16 = jnp.bfloat16

D = 1024
DEPTH = 4
NORM_EPS = 1e-6
CHUNK = 64
LANE = 128
HG_HEADS = 8
RW_HEAD = 64
RW_GN_EPS = 64e-5
LORA_W, LORA_A, LORA_G, LORA_V = 64, 64, 160, 32
S5_GROUPS, S5_GROUP, S5_STATE = 64, 16, 64
FFN_HIDDEN = 2816
FFN_CHUNK = 256

C16_Q, C16_I, C16_OG, C16_S5, C16_GATE = 0, 1, 2, 3, 4
C32_F = 0
LOG2E = 1.4426950408889634
VMEM_LIMIT = 56 * 1024 * 1024


def _cparams(*sem):
    return pltpu.CompilerParams(dimension_semantics=sem, vmem_limit_bytes=VMEM_LIMIT)


def _split3(x):
    hi = x.astype(BF16)
    r1 = x - hi.astype(F32)
    mid = r1.astype(BF16)
    lo = (r1 - mid.astype(F32)).astype(BF16)
    return hi, mid, lo


def _dot(a, b):
    return jnp.dot(a, b, preferred_element_type=F32)


def _dot_nt(a, b):
    return lax.dot_general(a, b, (((1,), (1,)), ((), ())), preferred_element_type=F32)


def _exact_dot01(m3, x):
    hi, mid, lo = _split3(x)
    return _dot(m3, jnp.concatenate([hi, mid, lo], axis=0))


def _rms(x, g):
    ms = jnp.mean(x * x, axis=-1, keepdims=True)
    return x * lax.rsqrt(ms + NORM_EPS) * g


def _neg_abs(x):
    return -jnp.abs(x)


def _sigmoid(x):
    return 1.0 / (1.0 + jnp.exp(-x))


def _silu(x):
    return x * _sigmoid(x)


N_ALIGNED = 7
N16 = (C16_GATE + 3) * D
RW_PARAMS = ("mu", "mu_lora", "w0", "w2", "a0", "a2", "g2", "v0", "v1", "v2", "k_k", "k_a")


def _inproj_body(first, steps_per_seq, x_ref, g_ref, *refs):
    w_refs, wt_ref = refs[:N_ALIGNED], refs[N_ALIGNED]
    refs = refs[N_ALIGNED + 1:]
    if not first:
        vf_ref, refs = refs[0], refs[1:]
    p = dict(zip(RW_PARAMS, refs[:len(RW_PARAMS)]))
    ones_ref = refs[len(RW_PARAMS)]
    (o16_ref, o32_ref, ut_ref, ro_ref, lw_ref, ko_ref, vo_ref, kko_ref, bo_ref, go_ref,
     xn_ref, cr_ref, cl_ref) = refs[len(RW_PARAMS) + 1:]
    tm = x_ref.shape[0]

    @pl.when(pl.program_id(0) % steps_per_seq == 0)
    def _():
        cr_ref[...] = jnp.zeros_like(cr_ref)
        cl_ref[...] = jnp.zeros_like(cl_ref)

    xn_ref[...] = _rms(x_ref[...], g_ref[...]).astype(BF16)
    blk = lambda j: slice(j * D, (j + 1) * D)
    proj = lambda w: _dot(xn_ref[...], w)
    o16_ref[:, blk(C16_Q)] = proj(w_refs[0][...]).astype(BF16)
    o32_ref[...] = proj(w_refs[1][...])
    o16_ref[:, blk(C16_I)] = proj(w_refs[2][...]).astype(BF16)
    o16_ref[:, blk(C16_OG)] = proj(w_refs[3][...]).astype(BF16)
    y = proj(wt_ref[:, blk(0)])
    o16_ref[:, blk(C16_S5)] = y.astype(BF16)
    for c in range(ut_ref.shape[0]):
        ut_ref[c] = y[c * CHUNK:(c + 1) * CHUNK, :].T.astype(ut_ref.dtype)
    for j in range(3):
        o16_ref[:, blk(C16_GATE + j)] = proj(wt_ref[:, blk(1 + j)]).astype(BF16)

    def mix(z, carry, mu):
        row = lax.broadcasted_iota(jnp.int32, z.shape, 0)
        zs = jnp.where(row == 0, carry, pltpu.roll(z, 1, 0))
        return z + mu * (zs - z)

    def softplus(x):
        return jnp.maximum(x, 0.0) + jnp.log(1.0 + jnp.exp(_neg_abs(x)))

    zl = proj(wt_ref[:, 4 * D:4 * D + 4 * LANE])
    lora = mix(zl, cl_ref[0:1, :], p["mu_lora"][...])
    cl_ref[0:1, :] = zl[tm - 1:tm, :]
    wd = jnp.tanh(lora[:, 0:LANE]).astype(BF16)
    ad = lora[:, LANE:2 * LANE].astype(BF16)
    gd = _sigmoid(lora[:, 2 * LANE:4 * LANE]).astype(BF16)
    w_log = -softplus(-(p["w0"][...] + _dot(wd, p["w2"][...]))) - 0.5
    lw_ref[...] = -jnp.exp(w_log)
    a = _sigmoid(p["a0"][...] + _dot(ad, p["a2"][...]))
    go_ref[...] = _dot(gd, p["g2"][...]).astype(go_ref.dtype)

    zr = proj(w_refs[4][...])
    ro_ref[...] = mix(zr, cr_ref[0:1, :], p["mu"][0:1, :]).astype(ro_ref.dtype)
    cr_ref[0:1, :] = zr[tm - 1:tm, :]
    zk = proj(w_refs[5][...])
    k = mix(zk, cr_ref[1:2, :], p["mu"][1:2, :])
    cr_ref[1:2, :] = zk[tm - 1:tm, :]
    zv = proj(w_refs[6][...])
    v = mix(zv, cr_ref[2:3, :], p["mu"][2:3, :])
    cr_ref[2:3, :] = zv[tm - 1:tm, :]
    if not first:
        lowr = _dot(v.astype(BF16), p["v1"][...]).astype(BF16)
        v = v + (vf_ref[...].astype(F32) - v) * _sigmoid(p["v0"][...] + _dot(lowr, p["v2"][...]))
    vo_ref[...] = v.astype(vo_ref.dtype)

    kk = k * p["k_k"][...]
    ss = jnp.concatenate([_segsum(kk[:, i * LANE:(i + 1) * LANE] * kk[:, i * LANE:(i + 1) * LANE], ones_ref[...])
                          for i in range(D // LANE)], axis=1)
    kk = kk * lax.rsqrt(jnp.maximum(ss, 1e-24))
    kko_ref[...] = kk.astype(kko_ref.dtype)
    bo_ref[...] = (kk * a).astype(bo_ref.dtype)
    ko_ref[...] = (k * (1.0 + (a - 1.0) * p["k_a"][...])).astype(ko_ref.dtype)


def _inproj(h, g, w_all, layer, w_tail, v_first, rp, seq, tm):
    t = h.shape[0]
    first = v_first is None
    row = lambda i: (i, 0)
    fixed = lambda i: (0, 0)
    once = dict(pipeline_mode=pl.Buffered(1))
    rowblk = pl.BlockSpec((tm, D), row)
    aligned = [pl.BlockSpec((None, D, D), functools.partial(lambda j, i: (layer, 0, j), j), **once)
               for j in range(N_ALIGNED)]
    params = [rp[k] for k in RW_PARAMS] + [_ones_bd()]
    bf = jax.ShapeDtypeStruct((t, D), BF16)
    f32 = jax.ShapeDtypeStruct((t, D), F32)
    outs = pl.pallas_call(
        functools.partial(_inproj_body, first, seq // tm),
        grid=(t // tm,),
        in_specs=[rowblk, pl.BlockSpec((1, D), fixed)] + aligned + [pl.BlockSpec(w_tail.shape, fixed, **once)]
                 + ([] if first else [rowblk]) + [pl.BlockSpec(a.shape, fixed) for a in params],
        out_specs=[pl.BlockSpec((tm, N16), row), rowblk,
                   pl.BlockSpec((tm // CHUNK, D, CHUNK), lambda i: (i, 0, 0))] + [rowblk] * 7,
        out_shape=[jax.ShapeDtypeStruct((t, N16), BF16), f32,
                   jax.ShapeDtypeStruct((t // CHUNK, D, CHUNK), BF16), bf, f32, bf, bf, bf, bf, bf],
        scratch_shapes=[pltpu.VMEM((tm, D), BF16), pltpu.VMEM((8, D), F32), pltpu.VMEM((8, 4 * LANE), F32)],
        compiler_params=_cparams("arbitrary"),
        name="inproj",
    )(h, g, *([w_all] * N_ALIGNED), w_tail, *([] if first else [v_first]), *params)
    return outs[0], outs[1], outs[2], outs[3:]


def _tail_body(final, h_ref, ohg_ref, orw_ref, y_ref, u_ref, g0_ref, g1_ref, g2_ref,
               d_ref, gw_ref, gb_ref, wb_ref, wo_ref, fn_ref, wg_ref, wu_ref, wd_ref, fg_ref,
               o_ref, yt_ref, s5_ref, xn_ref, acc_ref):
    for c in range(y_ref.shape[0]):
        yt_ref[c * CHUNK:(c + 1) * CHUNK, :] = y_ref[c].astype(F32).T
    for i in range(D // LANE):
        ln = slice(i * LANE, (i + 1) * LANE)
        y = yt_ref[:, ln] + d_ref[:, ln] * u_ref[:, ln].astype(F32)
        zg = 0.5 * y * (1.0 + jnp.tanh(math.sqrt(2.0 / math.pi) * (y + 0.044715 * (y * y * y))))
        gate = _dot(zg.astype(BF16), gw_ref[i]) + gb_ref[:, ln]
        s5_ref[:, ln] = (zg * _sigmoid(gate)).astype(BF16)

    m = _sigmoid(g0_ref[...].astype(F32)) * _dot(ohg_ref[...], wb_ref[0])
    m = m + _sigmoid(g1_ref[...].astype(F32)) * _dot(orw_ref[...], wb_ref[1])
    m = m + _sigmoid(g2_ref[...].astype(F32)) * _dot(s5_ref[...], wb_ref[2])
    x = h_ref[...] + _dot(m.astype(BF16), wo_ref[...])

    xn_ref[...] = _rms(x, fn_ref[...]).astype(BF16)
    for c in range(FFN_HIDDEN // FFN_CHUNK):
        sl = slice(c * FFN_CHUNK, (c + 1) * FFN_CHUNK)
        a = _dot(xn_ref[...], wg_ref[:, sl])
        b = _dot(xn_ref[...], wu_ref[:, sl])
        part = _dot((_silu(a) * b).astype(BF16), wd_ref[sl, :])
        if c == 0:
            acc_ref[...] = x + part
        else:
            acc_ref[...] += part
    out = acc_ref[...]
    if final:
        out = _rms(out, fg_ref[...])
    o_ref[...] = out


def _tail(h, o_hg, o_rw, ycore, p16, d, glu_tiles, glu_b, wb, wo, fn, wg, wu, wd, fg, final, tm):
    t = h.shape[0]
    row = lambda i: (i, 0)
    once = dict(pipeline_mode=pl.Buffered(1))
    vec = pl.BlockSpec((1, D), lambda i: (0, 0))
    rowblk = pl.BlockSpec((tm, D), row)
    col = lambda j: pl.BlockSpec((tm, D), lambda i: (i, j))
    return pl.pallas_call(
        functools.partial(_tail_body, final),
        grid=(t // tm,),
        in_specs=[rowblk, rowblk, rowblk,
                  pl.BlockSpec((tm // CHUNK, D, CHUNK), lambda i: (i, 0, 0)),
                  col(C16_S5), col(C16_GATE), col(C16_GATE + 1), col(C16_GATE + 2),
                  vec, pl.BlockSpec((D // LANE, LANE, LANE), lambda i: (0, 0, 0), **once), vec,
                  pl.BlockSpec((3, D, D), lambda i: (0, 0, 0), **once),
                  pl.BlockSpec((D, D), lambda i: (0, 0), **once),
                  vec,
                  pl.BlockSpec((D, FFN_HIDDEN), lambda i: (0, 0), **once),
                  pl.BlockSpec((D, FFN_HIDDEN), lambda i: (0, 0), **once),
                  pl.BlockSpec((FFN_HIDDEN, D), lambda i: (0, 0), **once),
                  vec],
        out_specs=rowblk,
        out_shape=jax.ShapeDtypeStruct((t, D), F32),
        scratch_shapes=[pltpu.VMEM((tm, D), F32), pltpu.VMEM((tm, D), BF16),
                        pltpu.VMEM((tm, D), BF16), pltpu.VMEM((tm, D), F32)],
        compiler_params=_cparams("parallel"),
        name="tail",
    )(h, o_hg, o_rw, ycore, p16, p16, p16, p16, d, glu_tiles, glu_b, wb, wo, fn, wg, wu, wd, fg)


HG_LEVELS = (32, 16, 8, 4, 2, 1)


def _hg_dmat():
    t = np.arange(CHUNK)
    tri = (t[None, :] <= t[:, None]).astype(np.float32)
    blocks = [tri]
    for m in HG_LEVELS:
        r = (t // (2 * m)) * (2 * m) + m - 1
        blocks.append(tri - tri[r])
    d = np.concatenate(blocks, axis=0)
    return jnp.asarray(np.concatenate([d, d, d], axis=1), BF16)


S5_ROW = CHUNK * S5_GROUP


def _s5_consts():
    tj = np.arange(S5_ROW)
    e_tj = (tj[None, :] % CHUNK == np.arange(CHUNK)[:, None]).astype(np.float32)
    f_tj = (tj[None, :] // CHUNK == np.arange(S5_GROUP)[:, None]).astype(np.float32)
    return (jnp.asarray(e_tj, BF16), jnp.asarray(f_tj, BF16),
            jnp.asarray(e_tj.T, BF16), jnp.asarray(f_tj.T, BF16))


def _right01(x, e):
    hi, mid, lo = _split3(x)
    return _dot(hi, e) + _dot(mid, e) + _dot(lo, e)


def _left01(e, x):
    hi, mid, lo = _split3(x)
    return _dot(e, hi) + _dot(e, mid) + _dot(e, lo)


def _dot_f32(a, b):
    ah, am, al = _split3(a)
    bh, bm, bl = _split3(b)
    return (_dot(ah, bh) + _dot(ah, bm) + _dot(am, bh)
            + _dot(ah, bl) + _dot(al, bh) + _dot(am, bm))


def _s5_prep_body(lrr_ref, lir_ref, lrc_ref, lic_ref, ls_ref, brt_ref, bit_ref, crt_ref, cit_ref,
                  etj_ref, ftj_ref, esh_ref, fsh_ref,
                  wt_ref, wi_ref, wis_ref, wo_ref, a1_ref, a2_ref):
    dt = jnp.exp(ls_ref[...])
    lam_re, lam_im = lrr_ref[...], lir_ref[...]
    lr_row, li_row = lam_re * dt, lam_im * dt
    lr_col, li_col = lrc_ref[...] * dt, lic_ref[...] * dt
    re_half = lax.broadcasted_iota(jnp.int32, (1, 2 * S5_STATE), 1) < S5_STATE

    mag = jnp.exp(lr_row)
    ab_re, ab_im = mag * jnp.cos(li_row), mag * jnp.sin(li_row)
    den = lam_re * lam_re + lam_im * lam_im
    nr, ni = ab_re - 1.0, ab_im
    coef_re = (nr * lam_re + ni * lam_im) / den
    coef_im = (ni * lam_re - nr * lam_im) / den
    bb_re = coef_re * brt_ref[...] - coef_im * bit_ref[...]
    bb_im = coef_re * bit_ref[...] + coef_im * brt_ref[...]

    def powers(lr, li, n):
        m = jnp.exp(lr * n)
        return m * jnp.cos(li * n), m * jnp.sin(li * n)

    tau = lax.broadcasted_iota(jnp.int32, (1, CHUNK), 1).astype(F32)
    p0_re, p0_im = powers(lr_col, li_col, tau)
    ac_re, ac_im = powers(lr_col, li_col, 1.0)
    p1_re, p1_im = p0_re * ac_re - p0_im * ac_im, p0_re * ac_im + p0_im * ac_re
    c_re = _right01(crt_ref[...], ftj_ref[...])
    c_im = _right01(cit_ref[...], ftj_ref[...])
    p0e_re, p0e_im = _right01(p0_re, etj_ref[...]), _right01(p0_im, etj_ref[...])
    p1e_re, p1e_im = _right01(p1_re, etj_ref[...]), _right01(p1_im, etj_ref[...])

    m_stack = jnp.concatenate([c_re * p0e_re - c_im * p0e_im, c_re * p0e_im + c_im * p0e_re], axis=0)
    r0 = _dot_f32(jnp.where(re_half, bb_re, -bb_im), m_stack)
    t_lane = lax.broadcasted_iota(jnp.int32, (CHUNK, S5_ROW), 1) & (CHUNK - 1)
    s_row = lax.broadcasted_iota(jnp.int32, (CHUNK, S5_ROW), 0)
    for h in range(S5_GROUP):
        base = jnp.broadcast_to(r0[h:h + 1, :], (CHUNK, S5_ROW))
        rolled = pltpu.roll(base, 0, 1, stride=1, stride_axis=0)
        wt_ref[h * CHUNK:(h + 1) * CHUNK, :] = jnp.where(t_lane >= s_row, rolled, 0.0).astype(BF16)

    wo_ref[0:S5_STATE, :] = (c_re * p1e_re - c_im * p1e_im).astype(BF16)
    wo_ref[S5_STATE:2 * S5_STATE, :] = (-(c_re * p1e_im + c_im * p1e_re)).astype(BF16)

    back = (CHUNK - 1.0) - lax.broadcasted_iota(jnp.int32, (CHUNK, 1), 0).astype(F32)
    pt_re, pt_im = powers(lr_row, li_row, back)
    pte_re, pte_im = _left01(esh_ref[...], pt_re), _left01(esh_ref[...], pt_im)
    bbe_re, bbe_im = _left01(fsh_ref[...], bb_re), _left01(fsh_ref[...], bb_im)
    w_re = pte_re * bbe_re - pte_im * bbe_im
    w_im = pte_re * bbe_im + pte_im * bbe_re
    wi_ref[...] = jnp.where(re_half, w_re, w_im).astype(BF16)
    wis_ref[...] = jnp.where(re_half, w_im, w_re).astype(BF16)

    k = lax.broadcasted_iota(jnp.int32, (8, 1), 0)
    n = (CHUNK * (1 << k)).astype(F32)
    an_re, an_im = powers(lr_row, li_row, n)
    a1_ref[...] = an_re
    a2_ref[...] = jnp.where(re_half, -an_im, an_im)


def _s5_prep(lam_re, lam_im, log_step, b_re, b_im, c_re, c_im):
    nl = lam_re.shape[0]
    ns = 2 * S5_STATE
    g4 = lambda r, c: pl.BlockSpec((None, None, r, c), lambda l, g: (l, g, 0, 0))
    cst = lambda r, c: pl.BlockSpec((r, c), lambda l, g: (0, 0))
    sds = lambda r, c, dt: jax.ShapeDtypeStruct((nl, S5_GROUPS, r, c), dt)
    twice = lambda a: jnp.concatenate([a, a], axis=-1)
    e_tj, f_tj, e_sh, f_sh = _s5_consts()
    return pl.pallas_call(
        _s5_prep_body,
        grid=(nl, S5_GROUPS),
        in_specs=[g4(1, ns), g4(1, ns), g4(S5_STATE, 1), g4(S5_STATE, 1), g4(1, 1),
                  g4(S5_GROUP, ns), g4(S5_GROUP, ns), g4(S5_STATE, S5_GROUP), g4(S5_STATE, S5_GROUP),
                  cst(CHUNK, S5_ROW), cst(S5_GROUP, S5_ROW), cst(S5_ROW, CHUNK), cst(S5_ROW, S5_GROUP)],
        out_specs=[g4(S5_ROW, S5_ROW), g4(S5_ROW, ns), g4(S5_ROW, ns), g4(ns, S5_ROW), g4(8, ns), g4(8, ns)],
        out_shape=[sds(S5_ROW, S5_ROW, BF16), sds(S5_ROW, ns, BF16), sds(S5_ROW, ns, BF16),
                   sds(ns, S5_ROW, BF16), sds(8, ns, F32), sds(8, ns, F32)],
        compiler_params=_cparams("parallel", "parallel"),
        name="s5_prep",
    )(twice(lam_re)[:, :, None, :], twice(lam_im)[:, :, None, :], lam_re[:, :, :, None], lam_im[:, :, :, None],
      log_step[:, :, None, None],
      twice(jnp.swapaxes(b_re, 2, 3)), twice(jnp.swapaxes(b_im, 2, 3)),
      jnp.swapaxes(c_re, 2, 3), jnp.swapaxes(c_im, 2, 3),
      e_tj, f_tj, e_sh, f_sh)


def _s5_body(nc, u_ref, wt_ref, wi_ref, wis_ref, wo_ref, a1_ref, a2_ref, y_ref):
    u = u_ref[...]
    m = u.shape[0]
    x = _dot(u, wi_ref[...])
    xs = _dot(u, wis_ref[...])
    chunk = lax.broadcasted_iota(jnp.int32, (m, 2 * S5_STATE), 0) % nc

    def shifted(z, n):
        return jnp.where(chunk >= n, pltpu.roll(z, n, 0), 0.0)

    k = 0
    while (1 << k) < nc:
        a1, a2 = a1_ref[k:k + 1, :], a2_ref[k:k + 1, :]
        sx, sxs = shifted(x, 1 << k), shifted(xs, 1 << k)
        x, xs = x + a1 * sx + a2 * sxs, xs + a1 * sxs - a2 * sx
        k += 1
    y = _dot(u, wt_ref[...]) + _dot(shifted(x, 1).astype(BF16), wo_ref[...])
    y_ref[...] = y.astype(y_ref.dtype)


def _s5_core(u_t, ops, layer, nc):
    wt, wi, wis, wo, a1, a2 = ops
    m = u_t.shape[0]
    ns = 2 * S5_STATE
    grp = pl.BlockSpec((m, S5_ROW), lambda g: (0, g))
    lg = lambda r, c: pl.BlockSpec((None, None, r, c), lambda g: (layer, g, 0, 0))
    return pl.pallas_call(
        functools.partial(_s5_body, nc),
        grid=(S5_GROUPS,),
        in_specs=[grp, lg(S5_ROW, S5_ROW), lg(S5_ROW, ns), lg(S5_ROW, ns),
                  lg(ns, S5_ROW), lg(8, ns), lg(8, ns)],
        out_specs=grp,
        out_shape=jax.ShapeDtypeStruct((m, S5_GROUPS * S5_ROW), BF16),
        compiler_params=_cparams("parallel"),
        name="s5_core",
    )(u_t, wt, wi, wis, wo, a1, a2)


def _ones_bd():
    i = np.arange(LANE)
    return jnp.asarray((i[:, None] // RW_HEAD == i[None, :] // RW_HEAD).astype(np.float32), BF16)


def _segsum(x, ones_bd):
    hi = x.astype(BF16)
    lo = (x - hi.astype(F32)).astype(BF16)
    return _dot(hi, ones_bd) + _dot(lo, ones_bd)


RW_LEVELS = (2, 4, 8, 16, 32)


def _mix_body(layer, q_ref, f_ref, i_ref, og_ref, lbl_ref, on_ref, dm_ref,
              r_ref, lw_ref, k_ref, v_ref, kk_ref, b_ref, g_ref, rk_ref, lnw_ref, lnb_ref, tri_ref, ones_ref,
              ohg_ref, orw_ref, st_ref, d_ref, ht_ref, lg_ref):
    nbat, tb = r_ref.shape[0], r_ref.shape[1]
    ntile = D // LANE

    @pl.when(pl.program_id(1) == 0)
    def _():
        st_ref[...] = jnp.zeros_like(st_ref)
        ht_ref[...] = jnp.zeros_like(ht_ref)

    lgt = lbl_ref[...]
    e = jnp.exp(lgt - jnp.max(lgt, axis=0, keepdims=True))
    sm = e / jnp.sum(e, axis=0, keepdims=True)
    lb = jnp.zeros((1, D), F32)
    for l in range(1, layer + 1):
        lb = lb + sm[l:l + 1, :]
    lb = jnp.maximum(lb, 0.0)
    log_lb = jnp.log(lb)
    log_1mlb = jnp.log1p(-lb)
    one_mlb = 1.0 - lb

    ti = lax.broadcasted_iota(jnp.int32, (CHUNK, CHUNK), 0)
    si = lax.broadcasted_iota(jnp.int32, (CHUNK, CHUNK), 1)
    pair_masks = [(ti > si) & ((ti ^ si) >= m) & ((ti ^ si) < 2 * m) for m in HG_LEVELS]
    diag_mask = ti == si
    trow = lax.broadcasted_iota(jnp.int32, (CHUNK, LANE), 0)
    second = [(trow & m) != 0 for m in HG_LEVELS]

    l64 = lax.broadcasted_iota(jnp.int32, (CHUNK, LANE), 1)
    s64 = l64 & (RW_HEAD - 1)
    strict = trow > s64
    incl = trow >= s64
    eye = trow == s64
    xr = trow ^ s64
    lvl1 = strict & (xr == 1)
    lvl = [strict & (xr >= m) & (xr < 2 * m) for m in RW_LEVELS]
    head0 = l64 < RW_HEAD
    r128 = lax.broadcasted_iota(jnp.int32, (LANE, LANE), 0)
    l128 = lax.broadcasted_iota(jnp.int32, (LANE, LANE), 1)
    same_head = (r128 >> 6) == (l128 >> 6)

    def stack(x):
        return jnp.concatenate([jnp.where(head0, x, 0.0), jnp.where(head0, 0.0, x)], axis=0).astype(BF16)

    def chunk_step(c, carry):
        rows = pl.ds(pl.multiple_of(c * CHUNK, CHUNK), CHUNK)
        probs = [(bi, j) for bi in range(nbat) for j in range(ntile)]
        n = range(len(probs))
        lanes = lambda p: slice(probs[p][1] * LANE, (probs[p][1] + 1) * LANE)

        def ld(ref, p):
            return ref[probs[p][0], rows, lanes(p)].astype(F32)

        def lgp(p):
            return lg_ref[probs[p][0], :, lanes(p)]

        def dlev(p, lev):
            return d_ref[probs[p][0], lev * CHUNK:(lev + 1) * CHUNK, lanes(p)]

        hg, rw = {}, {}

        def hg_gate():
            kk_all = []
            for bi in range(nbat):
                z = f_ref[bi, rows, :]
                ez = jnp.exp(_neg_abs(z))
                lsig = jnp.minimum(z, 0.0) - jnp.log(1.0 + ez)
                cterm = log_1mlb + lsig
                mx = jnp.maximum(log_lb, cterm)
                logf = mx + jnp.log(1.0 + jnp.exp(_neg_abs(log_lb - cterm)))
                d_ref[bi] = _exact_dot01(dm_ref[...], logf * LOG2E)
                kk_all.append(one_mlb * jnp.where(z >= 0.0, ez, 1.0) / (1.0 + ez))
            hg["q"] = [_silu(ld(q_ref, p)) for p in n]
            hg["k"] = [kk_all[probs[p][0]][:, lanes(p)] for p in n]

        def hg_diag():
            s0 = [_dot_nt(hg["q"][p].astype(BF16), hg["k"][p].astype(BF16)) for p in n]
            hg["scores"] = [jnp.where(diag_mask, s0[p], 0.0) for p in n]

        def hg_level(li):
            def run():
                x = [(jnp.where(second[li], hg["q"][p], hg["k"][p])
                      * jnp.exp2(_neg_abs(dlev(p, li + 1)))).astype(BF16) for p in n]
                sl = [_dot_nt(x[p], x[p]) for p in n]
                hg["scores"] = [jnp.where(pair_masks[li], sl[p], hg["scores"][p]) for p in n]
            return run

        def hg_out():
            o = []
            for p in n:
                qe = (hg["q"][p] * jnp.exp2(dlev(p, 0))).astype(BF16)
                o.append(_dot(hg["scores"][p].astype(BF16), i_ref[probs[p][0], rows, lanes(p)].astype(BF16))
                         + _dot_nt(qe, st_ref[p].astype(BF16)))
            hg["o"] = o

        def hg_state():
            for p in n:
                b = dlev(p, 0)
                b_last = b[CHUNK - 1:CHUNK, :]
                kd = (hg["k"][p] * jnp.exp2(b_last - b)).astype(BF16)
                vt = ld(i_ref, p).T.astype(BF16)
                st_ref[p] = st_ref[p] * jnp.exp2(b_last) + _dot(vt, kd)

        def hg_store():
            for p in n:
                o = hg["o"][p]
                o = o * lax.rsqrt(jnp.mean(o * o, axis=-1, keepdims=True) + NORM_EPS)
                o = o * on_ref[:, lanes(p)] * _silu(ld(og_ref, p))
                ohg_ref[probs[p][0], rows, lanes(p)] = o.astype(ohg_ref.dtype)

        def rw_amat():
            for bi in range(nbat):
                lg_ref[bi] = _exact_dot01(tri_ref[...], lw_ref[bi, rows, :])
            a_t, r_t, a_ak, a_rk, l_ab, a_rb, x = [], [], [], [], [], [], []
            for p in n:
                lg = lgp(p)
                e_neg = jnp.exp(-lg)
                a_t.append(-ld(kk_ref, p) * jnp.exp(lg - ld(lw_ref, p)))
                r_t.append(ld(r_ref, p) * jnp.exp(lg))
                lhs = jnp.concatenate([a_t[p], r_t[p]], axis=0).astype(BF16)
                kb_neg = jnp.concatenate([stack(ld(k_ref, p) * e_neg), stack(ld(b_ref, p) * e_neg)], axis=0)
                a_kb = _dot_nt(lhs, kb_neg)
                a_ak.append(jnp.where(strict, a_kb[0:CHUNK, 0:LANE], 0.0))
                a_rk.append(jnp.where(incl, a_kb[CHUNK:2 * CHUNK, 0:LANE], 0.0))
                l_ab.append(jnp.where(strict, a_kb[0:CHUNK, LANE:2 * LANE], 0.0))
                a_rb.append(jnp.where(incl, a_kb[CHUNK:2 * CHUNK, LANE:2 * LANE], 0.0))
                x.append(jnp.where(eye, 1.0, 0.0) + jnp.where(lvl1, l_ab[p], 0.0))
            rw.update(a_t=a_t, r_t=r_t, a_ak=a_ak, a_rk=a_rk, l_ab=l_ab, a_rb=a_rb, x=x)

        def rw_inv_a(li):
            def run():
                rw["z"] = [_dot(rw["x"][p].astype(BF16), stack(jnp.where(lvl[li], rw["l_ab"][p], 0.0))) for p in n]
            return run

        def rw_inv_b():
            rw["x"] = [rw["x"][p] + _dot(rw["z"][p].astype(BF16), stack(rw["x"][p])) for p in n]

        def rw_mv():
            rw["v_st"] = [stack(ld(v_ref, p)) for p in n]
            rw["mv"] = [_dot(rw["a_ak"][p].astype(BF16), rw["v_st"][p]) for p in n]

        def rw_u0w():
            rw["u0w"] = [_dot(rw["x"][p].astype(BF16),
                              jnp.concatenate([stack(rw["mv"][p]), stack(rw["a_t"][p])], axis=1)) for p in n]

        def rw_u():
            rw["htb"] = [ht_ref[p].astype(BF16) for p in n]
            rw["u"] = [rw["u0w"][p][:, 0:LANE] + _dot_nt(rw["u0w"][p][:, LANE:2 * LANE].astype(BF16), rw["htb"][p])
                       for p in n]

        def rw_y():
            rw["y"] = [_dot_nt(rw["r_t"][p].astype(BF16), rw["htb"][p])
                       + _dot(rw["a_rk"][p].astype(BF16), rw["v_st"][p])
                       + _dot(rw["a_rb"][p].astype(BF16), stack(rw["u"][p])) for p in n]

        def rw_state():
            for p in n:
                lg = lgp(p)
                lg_end = lg[CHUNK - 1:CHUNK, :]
                e_end = jnp.exp(lg_end - lg)
                vu = jnp.concatenate([ld(v_ref, p), rw["u"][p]], axis=0)
                kb = jnp.concatenate([ld(k_ref, p) * e_end, ld(b_ref, p) * e_end], axis=0).astype(BF16)
                upd = _dot(vu.T.astype(BF16), kb)
                ht_ref[p] = ht_ref[p] * jnp.exp(lg_end) + jnp.where(same_head, upd, 0.0)

        def rw_store():
            ones = ones_ref[...]
            y = rw["y"]
            sl = lambda a, p: a[p * CHUNK:(p + 1) * CHUNK]
            mean = _segsum(jnp.concatenate(y, axis=0), ones) * (1.0 / RW_HEAD)
            yc = [y[p] - sl(mean, p) for p in n]
            var = _segsum(jnp.concatenate([c_ * c_ for c_ in yc], axis=0), ones) * (1.0 / RW_HEAD)
            rk = [ld(r_ref, p) * ld(k_ref, p) * rk_ref[:, lanes(p)] for p in n]
            bonus = _segsum(jnp.concatenate(rk, axis=0), ones)
            for p in n:
                yn = yc[p] * lax.rsqrt(sl(var, p) + RW_GN_EPS) * lnw_ref[:, lanes(p)] + lnb_ref[:, lanes(p)]
                orw_ref[probs[p][0], rows, lanes(p)] = (
                    (yn + sl(bonus, p) * ld(v_ref, p)) * ld(g_ref, p)).astype(orw_ref.dtype)

        rw_stages = [rw_amat]
        for li in range(len(RW_LEVELS)):
            rw_stages += [rw_inv_a(li), rw_inv_b]
        rw_stages += [rw_mv, rw_u0w, rw_u, rw_y, rw_state, rw_store]
        hg_stages = [hg_gate, hg_diag] + [hg_level(li) for li in range(len(HG_LEVELS))] + [hg_out, hg_state, hg_store]
        for s in range(max(len(rw_stages), len(hg_stages))):
            if s < len(rw_stages):
                rw_stages[s]()
            if s < len(hg_stages):
                hg_stages[s]()
        return carry

    lax.fori_loop(0, tb // CHUNK, chunk_step, 0)


def _mix(p16, p32, feats, lb_logits, onorm, r_k, ln_w, ln_b, layer, batch, seq, nbat, tb):
    t = p16.shape[0]
    fixed = lambda b, s: (0, 0)
    vec = pl.BlockSpec((1, D), fixed)
    col = lambda j: pl.BlockSpec((nbat, tb, D), lambda b, s: (b, s, j))
    blk = col(0)
    tri = np.tril(np.ones((CHUNK, CHUNK), np.float32))
    tri3 = jnp.asarray(np.concatenate([tri, tri, tri], axis=1), BF16)
    p16_3, p32_3 = p16.reshape(batch, seq, -1), p32.reshape(batch, seq, -1)
    out = jax.ShapeDtypeStruct((batch, seq, D), BF16)
    o_hg, o_rw = pl.pallas_call(
        functools.partial(_mix_body, layer),
        grid=(batch // nbat, seq // tb),
        in_specs=[col(C16_Q), col(C32_F), col(C16_I), col(C16_OG),
                  pl.BlockSpec(lb_logits.shape, fixed), vec, pl.BlockSpec((7 * CHUNK, 3 * CHUNK), fixed)]
                 + [blk] * 7 + [vec, vec, vec, pl.BlockSpec((CHUNK, 3 * CHUNK), fixed),
                                pl.BlockSpec((LANE, LANE), fixed)],
        out_specs=[blk, blk],
        out_shape=[out, out],
        scratch_shapes=[pltpu.VMEM((nbat * HG_HEADS, LANE, LANE), F32), pltpu.VMEM((nbat, 7 * CHUNK, D), F32),
                        pltpu.VMEM((nbat * (D // LANE), LANE, LANE), F32), pltpu.VMEM((nbat, CHUNK, D), F32)],
        compiler_params=_cparams("parallel", "arbitrary"),
        name="mix",
    )(p16_3, p32_3, p16_3, p16_3, lb_logits, onorm, _hg_dmat(),
      *[f.reshape(batch, seq, D) for f in feats], r_k, ln_w, ln_b, tri3, _ones_bd())
    return o_hg.reshape(t, D), o_rw.reshape(t, D)


def _pack_w_tail(w):
    o_lora = N_ALIGNED * D
    o_a, o_g = o_lora + LORA_W, o_lora + LORA_W + LORA_A
    o_s5 = o_g + LORA_G
    z = lambda n: jnp.zeros((D, n), w.dtype)
    return jnp.concatenate([w[:, o_s5:o_s5 + 4 * D], w[:, o_lora:o_a], z(LANE - LORA_W), w[:, o_a:o_g],
                            z(LANE - LORA_A), w[:, o_g:o_s5], z(2 * LANE - LORA_G)], axis=1)


def _pad_rows(w, n):
    return jnp.concatenate([w, jnp.zeros((n - w.shape[0], w.shape[1]), w.dtype)], axis=0)


def _pad_lanes(v, n):
    return jnp.concatenate([v, jnp.zeros((n - v.shape[0],), v.dtype)])


def kernel(x, mix_norm, w_in, hg_lb_logits, hg_onorm, rw_shift_mu, rw_w0, rw_w2, rw_a0, rw_a2, rw_g2, rw_v0, rw_v1, rw_v2, rw_k_k, rw_k_a, rw_r_k, rw_ln_w, rw_ln_b, s5_lambda_re, s5_lambda_im, s5_log_step, s5_b_re, s5_b_im, s5_c_re, s5_c_im, s5_d, s5_glu_w, s5_glu_b, w_branch, w_out, ffn_norm, ffn_w_gate, ffn_w_up, ffn_w_down, final_norm):
    batch, seq, _ = x.shape
    t = batch * seq
    nc = seq // CHUNK
    depth = w_in.shape[0]
    tm = min(512, t)
    tb = min(256, seq)
    h = x.reshape(t, D)
    row = lambda v: v.reshape(1, -1)

    s5_ops = _s5_prep(s5_lambda_re, s5_lambda_im, s5_log_step, s5_b_re, s5_b_im, s5_c_re, s5_c_im)
    w_in_bf = w_in.astype(BF16)
    v_first = None
    for l in range(depth):
        mu = rw_shift_mu[l]
        o1, o2, o3 = 3 * D + LORA_W, 3 * D + LORA_W + LORA_A, 3 * D + LORA_W + LORA_A + LORA_G
        first = l == 0
        rp = {
            "mu": mu[0:3 * D].reshape(3, D),
            "mu_lora": row(jnp.concatenate([_pad_lanes(mu[3 * D:o1], LANE), _pad_lanes(mu[o1:o2], LANE),
                                            _pad_lanes(mu[o2:o3], 2 * LANE)])),
            "w0": row(rw_w0[l]), "w2": _pad_rows(rw_w2[l], LANE).astype(BF16),
            "a0": row(rw_a0[l]), "a2": _pad_rows(rw_a2[l], LANE).astype(BF16),
            "g2": _pad_rows(rw_g2[l], 2 * LANE).astype(BF16),
            "v0": row(rw_v0[0 if first else l - 1]),
            "v1": jnp.concatenate([rw_v1[0 if first else l - 1],
                                   jnp.zeros((D, LANE - LORA_V), F32)], axis=1).astype(BF16),
            "v2": _pad_rows(rw_v2[0 if first else l - 1], LANE).astype(BF16),
            "k_k": row(rw_k_k[l]), "k_a": row(rw_k_a[l]),
        }
        p16, p32, u_t, feats = _inproj(h, row(mix_norm[l]), w_in_bf, l, _pack_w_tail(w_in_bf[l]),
                                       v_first, rp, seq, min(256, seq))
        if first:
            v_first = feats[3]
        o_hg, o_rw = _mix(p16, p32, feats, hg_lb_logits, row(hg_onorm[l]), row(rw_r_k[l]), row(rw_ln_w[l]),
                          row(rw_ln_b[l]), l, batch, seq, math.gcd(batch, 4), min(128, seq))

        y_t = _s5_core(u_t.reshape(t // CHUNK, S5_GROUPS * S5_ROW), s5_ops, l, nc)
        ycore = y_t.reshape(t // CHUNK, D, CHUNK)
        eye8 = jnp.eye(LANE // S5_GROUP, dtype=F32)
        gw = s5_glu_w[l].reshape(D // LANE, LANE // S5_GROUP, S5_GROUP, S5_GROUP)
        glu_tiles = jnp.einsum("igjk,gh->igjhk", gw, eye8).reshape(D // LANE, LANE, LANE).astype(BF16)
        h = _tail(h, o_hg, o_rw, ycore, p16, row(s5_d[l]), glu_tiles, row(s5_glu_b[l]),
                  w_branch[l].reshape(3, D, D).astype(BF16), w_out[l].astype(BF16),
                  row(ffn_norm[l]), ffn_w_gate[l].astype(BF16), ffn_w_up[l].astype(BF16),
                  ffn_w_down[l].astype(BF16), row(final_norm), l == depth - 1, tm)
    return h.reshape(batch, seq, D)
```

```python
import functools
import math

import numpy as np
import jax
import jax.numpy as jnp
from jax import lax
from jax.experimental import pallas as pl
from jax.experimental.pallas import tpu as pltpu

F32 = jnp.float32
BF16 = jnp.bfloat16

D = 1024
DEPTH = 4
NORM_EPS = 1e-6
CHUNK = 64
LANE = 128
HG_HEADS = 8
RW_HEAD = 64
RW_GN_EPS = 64e-5
LORA_W, LORA_A, LORA_G, LORA_V = 64, 64, 160, 32
S5_GROUPS, S5_GROUP, S5_STATE = 64, 16, 64
FFN_HIDDEN = 2816
FFN_CHUNK = 256

C16_Q, C16_I, C16_OG, C16_S5, C16_GATE = 0, 1, 2, 3, 4
C32_F = 0
LOG2E = 1.4426950408889634
VMEM_LIMIT = 56 * 1024 * 1024


def _cparams(*sem):
    return pltpu.CompilerParams(dimension_semantics=sem, vmem_limit_bytes=VMEM_LIMIT)


def _split3(x):
    hi = x.astype(BF16)
    r1 = x - hi.astype(F32)
    mid = r1.astype(BF16)
    lo = (r1 - mid.astype(F32)).astype(BF16)
    return hi, mid, lo


def _dot(a, b):
    return jnp.dot(a, b, preferred_element_type=F32)


def _dot_nt(a, b):
    return lax.dot_general(a, b, (((1,), (1,)), ((), ())), preferred_element_type=F32)


def _exact_dot01(m3, x):
    hi, mid, lo = _split3(x)
    return _dot(m3, jnp.concatenate([hi, mid, lo], axis=0))


def _rms(x, g):
    ms = jnp.mean(x * x, axis=-1, keepdims=True)
    return x * lax.rsqrt(ms + NORM_EPS) * g


def _neg_abs(x):
    return -jnp.abs(x)


def _sigmoid(x):
    return 1.0 / (1.0 + jnp.exp(-x))


def _silu(x):
    return x * _sigmoid(x)


N_ALIGNED = 7
N16 = (C16_GATE + 3) * D
RW_PARAMS = ("mu", "mu_lora", "w0", "w2", "a0", "a2", "g2", "v0", "v1", "v2", "k_k", "k_a")


def _inproj_body(first, steps_per_seq, x_ref, g_ref, *refs):
    w_refs, wt_ref = refs[:N_ALIGNED], refs[N_ALIGNED]
    refs = refs[N_ALIGNED + 1:]
    if not first:
        vf_ref, refs = refs[0], refs[1:]
    p = dict(zip(RW_PARAMS, refs[:len(RW_PARAMS)]))
    ones_ref = refs[len(RW_PARAMS)]
    (o16_ref, o32_ref, ut_ref, ro_ref, lw_ref, ko_ref, vo_ref, kko_ref, bo_ref, go_ref,
     xn_ref, cr_ref, cl_ref) = refs[len(RW_PARAMS) + 1:]
    tm = x_ref.shape[0]

    @pl.when(pl.program_id(0) % steps_per_seq == 0)
    def _():
        cr_ref[...] = jnp.zeros_like(cr_ref)
        cl_ref[...] = jnp.zeros_like(cl_ref)

    xn_ref[...] = _rms(x_ref[...], g_ref[...]).astype(BF16)
    blk = lambda j: slice(j * D, (j + 1) * D)
    proj = lambda w: _dot(xn_ref[...], w)
    zl = proj(wt_ref[:, 4 * D:4 * D + 4 * LANE])
    zr = proj(w_refs[4][...])
    zk = proj(w_refs[5][...])
    zv = proj(w_refs[6][...])
    o16_ref[:, blk(C16_Q)] = proj(w_refs[0][...]).astype(BF16)
    o32_ref[...] = proj(w_refs[1][...])
    o16_ref[:, blk(C16_I)] = proj(w_refs[2][...]).astype(BF16)
    o16_ref[:, blk(C16_OG)] = proj(w_refs[3][...]).astype(BF16)

    def mix(z, carry, mu):
        row = lax.broadcasted_iota(jnp.int32, z.shape, 0)
        zs = jnp.where(row == 0, carry, pltpu.roll(z, 1, 0))
        return z + mu * (zs - z)

    def softplus(x):
        return jnp.maximum(x, 0.0) + jnp.log(1.0 + jnp.exp(_neg_abs(x)))

    lora = mix(zl, cl_ref[0:1, :], p["mu_lora"][...])
    cl_ref[0:1, :] = zl[tm - 1:tm, :]
    wd = jnp.tanh(lora[:, 0:LANE]).astype(BF16)
    ad = lora[:, LANE:2 * LANE].astype(BF16)
    gd = _sigmoid(lora[:, 2 * LANE:4 * LANE]).astype(BF16)
    w_log = -softplus(-(p["w0"][...] + _dot(wd, p["w2"][...]))) - 0.5
    lw_ref[...] = -jnp.exp(w_log)
    a = _sigmoid(p["a0"][...] + _dot(ad, p["a2"][...]))
    go_ref[...] = _dot(gd, p["g2"][...]).astype(go_ref.dtype)

    ro_ref[...] = mix(zr, cr_ref[0:1, :], p["mu"][0:1, :]).astype(ro_ref.dtype)
    cr_ref[0:1, :] = zr[tm - 1:tm, :]
    k = mix(zk, cr_ref[1:2, :], p["mu"][1:2, :])
    cr_ref[1:2, :] = zk[tm - 1:tm, :]
    v = mix(zv, cr_ref[2:3, :], p["mu"][2:3, :])
    cr_ref[2:3, :] = zv[tm - 1:tm, :]
    if not first:
        lowr = _dot(v.astype(BF16), p["v1"][...]).astype(BF16)
        v = v + (vf_ref[...].astype(F32) - v) * _sigmoid(p["v0"][...] + _dot(lowr, p["v2"][...]))
    vo_ref[...] = v.astype(vo_ref.dtype)

    kk = k * p["k_k"][...]
    ss = jnp.concatenate([_segsum(kk[:, i * LANE:(i + 1) * LANE] * kk[:, i * LANE:(i + 1) * LANE], ones_ref[...])
                          for i in range(D // LANE)], axis=1)
    kk = kk * lax.rsqrt(jnp.maximum(ss, 1e-24))
    kko_ref[...] = kk.astype(kko_ref.dtype)
    bo_ref[...] = (kk * a).astype(bo_ref.dtype)
    ko_ref[...] = (k * (1.0 + (a - 1.0) * p["k_a"][...])).astype(ko_ref.dtype)

    y = proj(wt_ref[:, blk(0)])
    o16_ref[:, blk(C16_S5)] = y.astype(BF16)
    for c in range(ut_ref.shape[0]):
        ut_ref[c] = y[c * CHUNK:(c + 1) * CHUNK, :].T.astype(ut_ref.dtype)
    for j in range(3):
        o16_ref[:, blk(C16_GATE + j)] = proj(wt_ref[:, blk(1 + j)]).astype(BF16)


def _inproj(h, g, w_all, layer, w_tail, v_first, rp, seq, tm):
    t = h.shape[0]
    first = v_first is None
    row = lambda i: (i, 0)
    fixed = lambda i: (0, 0)
    once = dict(pipeline_mode=pl.Buffered(1))
    rowblk = pl.BlockSpec((tm, D), row)
    aligned = [pl.BlockSpec((None, D, D), functools.partial(lambda j, i: (layer, 0, j), j), **once)
               for j in range(N_ALIGNED)]
    params = [rp[k] for k in RW_PARAMS] + [_ones_bd()]
    bf = jax.ShapeDtypeStruct((t, D), BF16)
    f32 = jax.ShapeDtypeStruct((t, D), F32)
    outs = pl.pallas_call(
        functools.partial(_inproj_body, first, seq // tm),
        grid=(t // tm,),
        in_specs=[rowblk, pl.BlockSpec((1, D), fixed)] + aligned + [pl.BlockSpec(w_tail.shape, fixed, **once)]
                 + ([] if first else [rowblk]) + [pl.BlockSpec(a.shape, fixed) for a in params],
        out_specs=[pl.BlockSpec((tm, N16), row), rowblk,
                   pl.BlockSpec((tm // CHUNK, D, CHUNK), lambda i: (i, 0, 0))] + [rowblk] * 7,
        out_shape=[jax.ShapeDtypeStruct((t, N16), BF16), f32,
                   jax.ShapeDtypeStruct((t // CHUNK, D, CHUNK), BF16), bf, f32, bf, bf, bf, bf, bf],
        scratch_shapes=[pltpu.VMEM((tm, D), BF16), pltpu.VMEM((8, D), F32), pltpu.VMEM((8, 4 * LANE), F32)],
        compiler_params=_cparams("arbitrary"),
        name="inproj",
    )(h, g, *([w_all] * N_ALIGNED), w_tail, *([] if first else [v_first]), *params)
    return outs[0], outs[1], outs[2], outs[3:]


def _tail_body(final, h_ref, ohg_ref, orw_ref, y_ref, u_ref, g0_ref, g1_ref, g2_ref,
               d_ref, gw_ref, gb_ref, wb_ref, wo_ref, fn_ref, wg_ref, wu_ref, wd_ref, fg_ref,
               o_ref, yt_ref, s5_ref, xn_ref, acc_ref):
    for c in range(y_ref.shape[0]):
        yt_ref[c * CHUNK:(c + 1) * CHUNK, :] = y_ref[c].astype(F32).T
    for i in range(D // LANE):
        ln = slice(i * LANE, (i + 1) * LANE)
        y = yt_ref[:, ln] + d_ref[:, ln] * u_ref[:, ln].astype(F32)
        zg = 0.5 * y * (1.0 + jnp.tanh(math.sqrt(2.0 / math.pi) * (y + 0.044715 * (y * y * y))))
        gate = _dot(zg.astype(BF16), gw_ref[i]) + gb_ref[:, ln]
        s5_ref[:, ln] = (zg * _sigmoid(gate)).astype(BF16)

    m = _sigmoid(g0_ref[...].astype(F32)) * _dot(ohg_ref[...], wb_ref[0])
    m = m + _sigmoid(g1_ref[...].astype(F32)) * _dot(orw_ref[...], wb_ref[1])
    m = m + _sigmoid(g2_ref[...].astype(F32)) * _dot(s5_ref[...], wb_ref[2])
    x = h_ref[...] + _dot(m.astype(BF16), wo_ref[...])

    xn_ref[...] = _rms(x, fn_ref[...]).astype(BF16)
    for c in range(FFN_HIDDEN // FFN_CHUNK):
        sl = slice(c * FFN_CHUNK, (c + 1) * FFN_CHUNK)
        a = _dot(xn_ref[...], wg_ref[:, sl])
        b = _dot(xn_ref[...], wu_ref[:, sl])
        part = _dot((_silu(a) * b).astype(BF16), wd_ref[sl, :])
        if c == 0:
            acc_ref[...] = x + part
        else:
            acc_ref[...] += part
    out = acc_ref[...]
    if final:
        out = _rms(out, fg_ref[...])
    o_ref[...] = out


def _tail(h, o_hg, o_rw, ycore, p16, d, glu_tiles, glu_b, wb, wo, fn, wg, wu, wd, fg, final, tm):
    t = h.shape[0]
    row = lambda i: (i, 0)
    once = dict(pipeline_mode=pl.Buffered(1))
    vec = pl.BlockSpec((1, D), lambda i: (0, 0))
    rowblk = pl.BlockSpec((tm, D), row)
    col = lambda j: pl.BlockSpec((tm, D), lambda i: (i, j))
    return pl.pallas_call(
        functools.partial(_tail_body, final),
        grid=(t // tm,),
        in_specs=[rowblk, rowblk, rowblk,
                  pl.BlockSpec((tm // CHUNK, D, CHUNK), lambda i: (i, 0, 0)),
                  col(C16_S5), col(C16_GATE), col(C16_GATE + 1), col(C16_GATE + 2),
                  vec, pl.BlockSpec((D // LANE, LANE, LANE), lambda i: (0, 0, 0), **once), vec,
                  pl.BlockSpec((3, D, D), lambda i: (0, 0, 0), **once),
                  pl.BlockSpec((D, D), lambda i: (0, 0), **once),
                  vec,
                  pl.BlockSpec((D, FFN_HIDDEN), lambda i: (0, 0), **once),
                  pl.BlockSpec((D, FFN_HIDDEN), lambda i: (0, 0), **once),
                  pl.BlockSpec((FFN_HIDDEN, D), lambda i: (0, 0), **once),
                  vec],
        out_specs=rowblk,
        out_shape=jax.ShapeDtypeStruct((t, D), F32),
        scratch_shapes=[pltpu.VMEM((tm, D), F32), pltpu.VMEM((tm, D), BF16),
                        pltpu.VMEM((tm, D), BF16), pltpu.VMEM((tm, D), F32)],
        compiler_params=_cparams("parallel"),
        name="tail",
    )(h, o_hg, o_rw, ycore, p16, p16, p16, p16, d, glu_tiles, glu_b, wb, wo, fn, wg, wu, wd, fg)


HG_LEVELS = (32, 16, 8, 4, 2, 1)


def _hg_dmat():
    t = np.arange(CHUNK)
    tri = (t[None, :] <= t[:, None]).astype(np.float32)
    blocks = [tri]
    for m in HG_LEVELS:
        r = (t // (2 * m)) * (2 * m) + m - 1
        blocks.append(tri - tri[r])
    d = np.concatenate(blocks, axis=0)
    return jnp.asarray(np.concatenate([d, d, d], axis=1), BF16)


S5_ROW = CHUNK * S5_GROUP


def _s5_consts():
    tj = np.arange(S5_ROW)
    e_tj = (tj[None, :] % CHUNK == np.arange(CHUNK)[:, None]).astype(np.float32)
    f_tj = (tj[None, :] // CHUNK == np.arange(S5_GROUP)[:, None]).astype(np.float32)
    return (jnp.asarray(e_tj, BF16), jnp.asarray(f_tj, BF16),
            jnp.asarray(e_tj.T, BF16), jnp.asarray(f_tj.T, BF16))


def _right01(x, e):
    hi, mid, lo = _split3(x)
    return _dot(hi, e) + _dot(mid, e) + _dot(lo, e)


def _left01(e, x):
    hi, mid, lo = _split3(x)
    return _dot(e, hi) + _dot(e, mid) + _dot(e, lo)


def _dot_f32(a, b):
    ah, am, al = _split3(a)
    bh, bm, bl = _split3(b)
    return (_dot(ah, bh) + _dot(ah, bm) + _dot(am, bh)
            + _dot(ah, bl) + _dot(al, bh) + _dot(am, bm))


def _s5_prep_body(lrr_ref, lir_ref, lrc_ref, lic_ref, ls_ref, brt_ref, bit_ref, crt_ref, cit_ref,
                  etj_ref, ftj_ref, esh_ref, fsh_ref,
                  wt_ref, wi_ref, wis_ref, wo_ref, a1_ref, a2_ref):
    dt = jnp.exp(ls_ref[...])
    lam_re, lam_im = lrr_ref[...], lir_ref[...]
    lr_row, li_row = lam_re * dt, lam_im * dt
    lr_col, li_col = lrc_ref[...] * dt, lic_ref[...] * dt
    re_half = lax.broadcasted_iota(jnp.int32, (1, 2 * S5_STATE), 1) < S5_STATE

    mag = jnp.exp(lr_row)
    ab_re, ab_im = mag * jnp.cos(li_row), mag * jnp.sin(li_row)
    den = lam_re * lam_re + lam_im * lam_im
    nr, ni = ab_re - 1.0, ab_im
    coef_re = (nr * lam_re + ni * lam_im) / den
    coef_im = (ni * lam_re - nr * lam_im) / den
    bb_re = coef_re * brt_ref[...] - coef_im * bit_ref[...]
    bb_im = coef_re * bit_ref[...] + coef_im * brt_ref[...]

    def powers(lr, li, n):
        m = jnp.exp(lr * n)
        return m * jnp.cos(li * n), m * jnp.sin(li * n)

    tau = lax.broadcasted_iota(jnp.int32, (1, CHUNK), 1).astype(F32)
    p0_re, p0_im = powers(lr_col, li_col, tau)
    ac_re, ac_im = powers(lr_col, li_col, 1.0)
    p1_re, p1_im = p0_re * ac_re - p0_im * ac_im, p0_re * ac_im + p0_im * ac_re
    c_re = _right01(crt_ref[...], ftj_ref[...])
    c_im = _right01(cit_ref[...], ftj_ref[...])
    p0e_re, p0e_im = _right01(p0_re, etj_ref[...]), _right01(p0_im, etj_ref[...])
    p1e_re, p1e_im = _right01(p1_re, etj_ref[...]), _right01(p1_im, etj_ref[...])

    m_stack = jnp.concatenate([c_re * p0e_re - c_im * p0e_im, c_re * p0e_im + c_im * p0e_re], axis=0)
    r0 = _dot_f32(jnp.where(re_half, bb_re, -bb_im), m_stack)
    t_lane = lax.broadcasted_iota(jnp.int32, (CHUNK, S5_ROW), 1) & (CHUNK - 1)
    s_row = lax.broadcasted_iota(jnp.int32, (CHUNK, S5_ROW), 0)
    for h in range(S5_GROUP):
        base = jnp.broadcast_to(r0[h:h + 1, :], (CHUNK, S5_ROW))
        rolled = pltpu.roll(base, 0, 1, stride=1, stride_axis=0)
        wt_ref[h * CHUNK:(h + 1) * CHUNK, :] = jnp.where(t_lane >= s_row, rolled, 0.0).astype(BF16)

    wo_ref[0:S5_STATE, :] = (c_re * p1e_re - c_im * p1e_im).astype(BF16)
    wo_ref[S5_STATE:2 * S5_STATE, :] = (-(c_re * p1e_im + c_im * p1e_re)).astype(BF16)

    back = (CHUNK - 1.0) - lax.broadcasted_iota(jnp.int32, (CHUNK, 1), 0).astype(F32)
    pt_re, pt_im = powers(lr_row, li_row, back)
    pte_re, pte_im = _left01(esh_ref[...], pt_re), _left01(esh_ref[...], pt_im)
    bbe_re, bbe_im = _left01(fsh_ref[...], bb_re), _left01(fsh_ref[...], bb_im)
    w_re = pte_re * bbe_re - pte_im * bbe_im
    w_im = pte_re * bbe_im + pte_im * bbe_re
    wi_ref[...] = jnp.where(re_half, w_re, w_im).astype(BF16)
    wis_ref[...] = jnp.where(re_half, w_im, w_re).astype(BF16)

    k = lax.broadcasted_iota(jnp.int32, (8, 1), 0)
    n = (CHUNK * (1 << k)).astype(F32)
    an_re, an_im = powers(lr_row, li_row, n)
    a1_ref[...] = an_re
    a2_ref[...] = jnp.where(re_half, -an_im, an_im)


def _s5_prep(lam_re, lam_im, log_step, b_re, b_im, c_re, c_im):
    nl = lam_re.shape[0]
    ns = 2 * S5_STATE
    g4 = lambda r, c: pl.BlockSpec((None, None, r, c), lambda l, g: (l, g, 0, 0))
    cst = lambda r, c: pl.BlockSpec((r, c), lambda l, g: (0, 0))
    sds = lambda r, c, dt: jax.ShapeDtypeStruct((nl, S5_GROUPS, r, c), dt)
    twice = lambda a: jnp.concatenate([a, a], axis=-1)
    e_tj, f_tj, e_sh, f_sh = _s5_consts()
    return pl.pallas_call(
        _s5_prep_body,
        grid=(nl, S5_GROUPS),
        in_specs=[g4(1, ns), g4(1, ns), g4(S5_STATE, 1), g4(S5_STATE, 1), g4(1, 1),
                  g4(S5_GROUP, ns), g4(S5_GROUP, ns), g4(S5_STATE, S5_GROUP), g4(S5_STATE, S5_GROUP),
                  cst(CHUNK, S5_ROW), cst(S5_GROUP, S5_ROW), cst(S5_ROW, CHUNK), cst(S5_ROW, S5_GROUP)],
        out_specs=[g4(S5_ROW, S5_ROW), g4(S5_ROW, ns), g4(S5_ROW, ns), g4(ns, S5_ROW), g4(8, ns), g4(8, ns)],
        out_shape=[sds(S5_ROW, S5_ROW, BF16), sds(S5_ROW, ns, BF16), sds(S5_ROW, ns, BF16),
                   sds(ns, S5_ROW, BF16), sds(8, ns, F32), sds(8, ns, F32)],
        compiler_params=_cparams("parallel", "parallel"),
        name="s5_prep",
    )(twice(lam_re)[:, :, None, :], twice(lam_im)[:, :, None, :], lam_re[:, :, :, None], lam_im[:, :, :, None],
      log_step[:, :, None, None],
      twice(jnp.swapaxes(b_re, 2, 3)), twice(jnp.swapaxes(b_im, 2, 3)),
      jnp.swapaxes(c_re, 2, 3), jnp.swapaxes(c_im, 2, 3),
      e_tj, f_tj, e_sh, f_sh)


def _s5_body(nc, u_ref, wt_ref, wi_ref, wis_ref, wo_ref, a1_ref, a2_ref, y_ref):
    u = u_ref[...]
    m = u.shape[0]
    x = _dot(u, wi_ref[...])
    xs = _dot(u, wis_ref[...])
    chunk = lax.broadcasted_iota(jnp.int32, (m, 2 * S5_STATE), 0) % nc

    def shifted(z, n):
        return jnp.where(chunk >= n, pltpu.roll(z, n, 0), 0.0)

    k = 0
    while (1 << k) < nc:
        a1, a2 = a1_ref[k:k + 1, :], a2_ref[k:k + 1, :]
        sx, sxs = shifted(x, 1 << k), shifted(xs, 1 << k)
        x, xs = x + a1 * sx + a2 * sxs, xs + a1 * sxs - a2 * sx
        k += 1
    y = _dot(u, wt_ref[...]) + _dot(shifted(x, 1).astype(BF16), wo_ref[...])
    y_ref[...] = y.astype(y_ref.dtype)


def _s5_core(u_t, ops, layer, nc):
    wt, wi, wis, wo, a1, a2 = ops
    m = u_t.shape[0]
    ns = 2 * S5_STATE
    grp = pl.BlockSpec((m, S5_ROW), lambda g: (0, g))
    lg = lambda r, c: pl.BlockSpec((None, None, r, c), lambda g: (layer, g, 0, 0))
    return pl.pallas_call(
        functools.partial(_s5_body, nc),
        grid=(S5_GROUPS,),
        in_specs=[grp, lg(S5_ROW, S5_ROW), lg(S5_ROW, ns), lg(S5_ROW, ns),
                  lg(ns, S5_ROW), lg(8, ns), lg(8, ns)],
        out_specs=grp,
        out_shape=jax.ShapeDtypeStruct((m, S5_GROUPS * S5_ROW), BF16),
        compiler_params=_cparams("parallel"),
        name="s5_core",
    )(u_t, wt, wi, wis, wo, a1, a2)


def _ones_bd():
    i = np.arange(LANE)
    return jnp.asarray((i[:, None] // RW_HEAD == i[None, :] // RW_HEAD).astype(np.float32), BF16)


def _segsum(x, ones_bd):
    hi = x.astype(BF16)
    lo = (x - hi.astype(F32)).astype(BF16)
    return _dot(hi, ones_bd) + _dot(lo, ones_bd)


RW_LEVELS = (2, 4, 8, 16, 32)


def _mix_body(layer, q_ref, f_ref, i_ref, og_ref, lbl_ref, on_ref, dm_ref,
              r_ref, lw_ref, k_ref, v_ref, kk_ref, b_ref, g_ref, rk_ref, lnw_ref, lnb_ref, tri_ref, ones_ref,
              ohg_ref, orw_ref, st_ref, d_ref, ht_ref, lg_ref):
    nbat, tb = r_ref.shape[0], r_ref.shape[1]
    ntile = D // LANE

    @pl.when(pl.program_id(1) == 0)
    def _():
        st_ref[...] = jnp.zeros_like(st_ref)
        ht_ref[...] = jnp.zeros_like(ht_ref)

    lgt = lbl_ref[...]
    e = jnp.exp(lgt - jnp.max(lgt, axis=0, keepdims=True))
    sm = e / jnp.sum(e, axis=0, keepdims=True)
    lb = jnp.zeros((1, D), F32)
    for l in range(1, layer + 1):
        lb = lb + sm[l:l + 1, :]
    lb = jnp.maximum(lb, 0.0)
    log_lb = jnp.log(lb)
    log_1mlb = jnp.log1p(-lb)
    one_mlb = 1.0 - lb

    ti = lax.broadcasted_iota(jnp.int32, (CHUNK, CHUNK), 0)
    si = lax.broadcasted_iota(jnp.int32, (CHUNK, CHUNK), 1)
    pair_masks = [(ti > si) & ((ti ^ si) >= m) & ((ti ^ si) < 2 * m) for m in HG_LEVELS]
    diag_mask = ti == si
    trow = lax.broadcasted_iota(jnp.int32, (CHUNK, LANE), 0)
    second = [(trow & m) != 0 for m in HG_LEVELS]

    l64 = lax.broadcasted_iota(jnp.int32, (CHUNK, LANE), 1)
    s64 = l64 & (RW_HEAD - 1)
    strict = trow > s64
    incl = trow >= s64
    eye = trow == s64
    xr = trow ^ s64
    lvl1 = strict & (xr == 1)
    lvl = [strict & (xr >= m) & (xr < 2 * m) for m in RW_LEVELS]
    head0 = l64 < RW_HEAD
    r128 = lax.broadcasted_iota(jnp.int32, (LANE, LANE), 0)
    l128 = lax.broadcasted_iota(jnp.int32, (LANE, LANE), 1)
    same_head = (r128 >> 6) == (l128 >> 6)

    def stack(x):
        return jnp.concatenate([jnp.where(head0, x, 0.0), jnp.where(head0, 0.0, x)], axis=0).astype(BF16)

    def chunk_step(c, carry):
        rows = pl.ds(pl.multiple_of(c * CHUNK, CHUNK), CHUNK)
        probs = [(bi, j) for bi in range(nbat) for j in range(ntile)]
        n = range(len(probs))
        lanes = lambda p: slice(probs[p][1] * LANE, (probs[p][1] + 1) * LANE)

        def ld(ref, p):
            return ref[probs[p][0], rows, lanes(p)].astype(F32)

        def lgp(p):
            return lg_ref[probs[p][0], :, lanes(p)]

        def dlev(p, lev):
            return d_ref[probs[p][0], lev * CHUNK:(lev + 1) * CHUNK, lanes(p)]

        hg, rw = {}, {}

        def hg_gate():
            kk_all = []
            for bi in range(nbat):
                z = f_ref[bi, rows, :]
                ez = jnp.exp(_neg_abs(z))
                lsig = jnp.minimum(z, 0.0) - jnp.log(1.0 + ez)
                cterm = log_1mlb + lsig
                mx = jnp.maximum(log_lb, cterm)
                logf = mx + jnp.log(1.0 + jnp.exp(_neg_abs(log_lb - cterm)))
                d_ref[bi] = _exact_dot01(dm_ref[...], logf * LOG2E)
                kk_all.append(one_mlb * jnp.where(z >= 0.0, ez, 1.0) / (1.0 + ez))
            hg["q"] = [_silu(ld(q_ref, p)) for p in n]
            hg["k"] = [kk_all[probs[p][0]][:, lanes(p)] for p in n]

        def hg_diag():
            s0 = [_dot_nt(hg["q"][p].astype(BF16), hg["k"][p].astype(BF16)) for p in n]
            hg["scores"] = [jnp.where(diag_mask, s0[p], 0.0) for p in n]

        def hg_level(li):
            def run():
                x = [(jnp.where(second[li], hg["q"][p], hg["k"][p])
                      * jnp.exp2(_neg_abs(dlev(p, li + 1)))).astype(BF16) for p in n]
                sl = [_dot_nt(x[p], x[p]) for p in n]
                hg["scores"] = [jnp.where(pair_masks[li], sl[p], hg["scores"][p]) for p in n]
            return run

        def hg_out():
            o = []
            for p in n:
                qe = (hg["q"][p] * jnp.exp2(dlev(p, 0))).astype(BF16)
                o.append(_dot(hg["scores"][p].astype(BF16), i_ref[probs[p][0], rows, lanes(p)].astype(BF16))
                         + _dot_nt(qe, st_ref[p].astype(BF16)))
            hg["o"] = o

        def hg_state():
            for p in n:
                b = dlev(p, 0)
                b_last = b[CHUNK - 1:CHUNK, :]
                kd = (hg["k"][p] * jnp.exp2(b_last - b)).astype(BF16)
                vt = ld(i_ref, p).T.astype(BF16)
                st_ref[p] = st_ref[p] * jnp.exp2(b_last) + _dot(vt, kd)

        def hg_store():
            for p in n:
                o = hg["o"][p]
                o = o * lax.rsqrt(jnp.mean(o * o, axis=-1, keepdims=True) + NORM_EPS)
                o = o * on_ref[:, lanes(p)] * _silu(ld(og_ref, p))
                ohg_ref[probs[p][0], rows, lanes(p)] = o.astype(ohg_ref.dtype)

        def rw_amat():
            for bi in range(nbat):
                lg_ref[bi] = _exact_dot01(tri_ref[...], lw_ref[bi, rows, :])
            a_t, r_t, a_ak, a_rk, l_ab, a_rb, x = [], [], [], [], [], [], []
            for p in n:
                lg = lgp(p)
                e_neg = jnp.exp(-lg)
                a_t.append(-ld(kk_ref, p) * jnp.exp(lg - ld(lw_ref, p)))
                r_t.append(ld(r_ref, p) * jnp.exp(lg))
                lhs = jnp.concatenate([a_t[p], r_t[p]], axis=0).astype(BF16)
                kb_neg = jnp.concatenate([stack(ld(k_ref, p) * e_neg), stack(ld(b_ref, p) * e_neg)], axis=0)
                a_kb = _dot_nt(lhs, kb_neg)
                a_ak.append(jnp.where(strict, a_kb[0:CHUNK, 0:LANE], 0.0))
                a_rk.append(jnp.where(incl, a_kb[CHUNK:2 * CHUNK, 0:LANE], 0.0))
                l_ab.append(jnp.where(strict, a_kb[0:CHUNK, LANE:2 * LANE], 0.0))
                a_rb.append(jnp.where(incl, a_kb[CHUNK:2 * CHUNK, LANE:2 * LANE], 0.0))
                x.append(jnp.where(eye, 1.0, 0.0) + jnp.where(lvl1, l_ab[p], 0.0))
            rw.update(a_t=a_t, r_t=r_t, a_ak=a_ak, a_rk=a_rk, l_ab=l_ab, a_rb=a_rb, x=x)

        def rw_inv_a(li):
            def run():
                rw["z"] = [_dot(rw["x"][p].astype(BF16), stack(jnp.where(lvl[li], rw["l_ab"][p], 0.0))) for p in n]
            return run

        def rw_inv_b():
            rw["x"] = [rw["x"][p] + _dot(rw["z"][p].astype(BF16), stack(rw["x"][p])) for p in n]

        def rw_mv():
            rw["v_st"] = [stack(ld(v_ref, p)) for p in n]
            rw["mv"] = [_dot(rw["a_ak"][p].astype(BF16), rw["v_st"][p]) for p in n]

        def rw_u0w():
            rw["u0w"] = [_dot(rw["x"][p].astype(BF16),
                              jnp.concatenate([stack(rw["mv"][p]), stack(rw["a_t"][p])], axis=1)) for p in n]

        def rw_u():
            rw["htb"] = [ht_ref[p].astype(BF16) for p in n]
            rw["u"] = [rw["u0w"][p][:, 0:LANE] + _dot_nt(rw["u0w"][p][:, LANE:2 * LANE].astype(BF16), rw["htb"][p])
                       for p in n]

        def rw_y():
            rw["y"] = [_dot_nt(rw["r_t"][p].astype(BF16), rw["htb"][p])
                       + _dot(rw["a_rk"][p].astype(BF16), rw["v_st"][p])
                       + _dot(rw["a_rb"][p].astype(BF16), stack(rw["u"][p])) for p in n]

        def rw_state():
            for p in n:
                lg = lgp(p)
                lg_end = lg[CHUNK - 1:CHUNK, :]
                e_end = jnp.exp(lg_end - lg)
                vu = jnp.concatenate([ld(v_ref, p), rw["u"][p]], axis=0)
                kb = jnp.concatenate([ld(k_ref, p) * e_end, ld(b_ref, p) * e_end], axis=0).astype(BF16)
                upd = _dot(vu.T.astype(BF16), kb)
                ht_ref[p] = ht_ref[p] * jnp.exp(lg_end) + jnp.where(same_head, upd, 0.0)

        def rw_store():
            ones = ones_ref[...]
            y = rw["y"]
            sl = lambda a, p: a[p * CHUNK:(p + 1) * CHUNK]
            mean = _segsum(jnp.concatenate(y, axis=0), ones) * (1.0 / RW_HEAD)
            yc = [y[p] - sl(mean, p) for p in n]
            var = _segsum(jnp.concatenate([c_ * c_ for c_ in yc], axis=0), ones) * (1.0 / RW_HEAD)
            rk = [ld(r_ref, p) * ld(k_ref, p) * rk_ref[:, lanes(p)] for p in n]
            bonus = _segsum(jnp.concatenate(rk, axis=0), ones)
            for p in n:
                yn = yc[p] * lax.rsqrt(sl(var, p) + RW_GN_EPS) * lnw_ref[:, lanes(p)] + lnb_ref[:, lanes(p)]
                orw_ref[probs[p][0], rows, lanes(p)] = (
                    (yn + sl(bonus, p) * ld(v_ref, p)) * ld(g_ref, p)).astype(orw_ref.dtype)

        rw_stages = [rw_amat]
        for li in range(len(RW_LEVELS)):
            rw_stages += [rw_inv_a(li), rw_inv_b]
        rw_stages += [rw_mv, rw_u0w, rw_u, rw_y, rw_state, rw_store]
        hg_stages = [hg_gate, hg_diag] + [hg_level(li) for li in range(len(HG_LEVELS))] + [hg_out, hg_state, hg_store]
        for s in range(max(len(rw_stages), len(hg_stages))):
            if s < len(rw_stages):
                rw_stages[s]()
            if s < len(hg_stages):
                hg_stages[s]()
        return carry

    lax.fori_loop(0, tb // CHUNK, chunk_step, 0)


def _mix(p16, p32, feats, lb_logits, onorm, r_k, ln_w, ln_b, layer, batch, seq, nbat, tb):
    t = p16.shape[0]
    fixed = lambda b, s: (0, 0)
    vec = pl.BlockSpec((1, D), fixed)
    col = lambda j: pl.BlockSpec((nbat, tb, D), lambda b, s: (b, s, j))
    blk = col(0)
    tri = np.tril(np.ones((CHUNK, CHUNK), np.float32))
    tri3 = jnp.asarray(np.concatenate([tri, tri, tri], axis=1), BF16)
    p16_3, p32_3 = p16.reshape(batch, seq, -1), p32.reshape(batch, seq, -1)
    out = jax.ShapeDtypeStruct((batch, seq, D), BF16)
    o_hg, o_rw = pl.pallas_call(
        functools.partial(_mix_body, layer),
        grid=(batch // nbat, seq // tb),
        in_specs=[col(C16_Q), col(C32_F), col(C16_I), col(C16_OG),
                  pl.BlockSpec(lb_logits.shape, fixed), vec, pl.BlockSpec((7 * CHUNK, 3 * CHUNK), fixed)]
                 + [blk] * 7 + [vec, vec, vec, pl.BlockSpec((CHUNK, 3 * CHUNK), fixed),
                                pl.BlockSpec((LANE, LANE), fixed)],
        out_specs=[blk, blk],
        out_shape=[out, out],
        scratch_shapes=[pltpu.VMEM((nbat * HG_HEADS, LANE, LANE), F32), pltpu.VMEM((nbat, 7 * CHUNK, D), F32),
                        pltpu.VMEM((nbat * (D // LANE), LANE, LANE), F32), pltpu.VMEM((nbat, CHUNK, D), F32)],
        compiler_params=_cparams("parallel", "arbitrary"),
        name="mix",
    )(p16_3, p32_3, p16_3, p16_3, lb_logits, onorm, _hg_dmat(),
      *[f.reshape(batch, seq, D) for f in feats], r_k, ln_w, ln_b, tri3, _ones_bd())
    return o_hg.reshape(t, D), o_rw.reshape(t, D)


def _pack_w_tail(w):
    o_lora = N_ALIGNED * D
    o_a, o_g = o_lora + LORA_W, o_lora + LORA_W + LORA_A
    o_s5 = o_g + LORA_G
    z = lambda n: jnp.zeros((D, n), w.dtype)
    return jnp.concatenate([w[:, o_s5:o_s5 + 4 * D], w[:, o_lora:o_a], z(LANE - LORA_W), w[:, o_a:o_g],
                            z(LANE - LORA_A), w[:, o_g:o_s5], z(2 * LANE - LORA_G)], axis=1)


def _pad_rows(w, n):
    return jnp.concatenate([w, jnp.zeros((n - w.shape[0], w.shape[1]), w.dtype)], axis=0)


def _pad_lanes(v, n):
    return jnp.concatenate([v, jnp.zeros((n - v.shape[0],), v.dtype)])


def kernel(x, mix_norm, w_in, hg_lb_logits, hg_onorm, rw_shift_mu, rw_w0, rw_w2, rw_a0, rw_a2, rw_g2, rw_v0, rw_v1, rw_v2, rw_k_k, rw_k_a, rw_r_k, rw_ln_w, rw_ln_b, s5_lambda_re, s5_lambda_im, s5_log_step, s5_b_re, s5_b_im, s5_c_re, s5_c_im, s5_d, s5_glu_w, s5_glu_b, w_branch, w_out, ffn_norm, ffn_w_gate, ffn_w_up, ffn_w_down, final_norm):
    batch, seq, _ = x.shape
    t = batch * seq
    nc = seq // CHUNK
    depth = w_in.shape[0]
    tm = min(512, t)
    tb = min(256, seq)
    h = x.reshape(t, D)
    row = lambda v: v.reshape(1, -1)

    s5_ops = _s5_prep(s5_lambda_re, s5_lambda_im, s5_log_step, s5_b_re, s5_b_im, s5_c_re, s5_c_im)
    w_in_bf = w_in.astype(BF16)
    v_first = None
    for l in range(depth):
        mu = rw_shift_mu[l]
        o1, o2, o3 = 3 * D + LORA_W, 3 * D + LORA_W + LORA_A, 3 * D + LORA_W + LORA_A + LORA_G
        first = l == 0
        rp = {
            "mu": mu[0:3 * D].reshape(3, D),
            "mu_lora": row(jnp.concatenate([_pad_lanes(mu[3 * D:o1], LANE), _pad_lanes(mu[o1:o2], LANE),
                                            _pad_lanes(mu[o2:o3], 2 * LANE)])),
            "w0": row(rw_w0[l]), "w2": _pad_rows(rw_w2[l], LANE).astype(BF16),
            "a0": row(rw_a0[l]), "a2": _pad_rows(rw_a2[l], LANE).astype(BF16),
            "g2": _pad_rows(rw_g2[l], 2 * LANE).astype(BF16),
            "v0": row(rw_v0[0 if first else l - 1]),
            "v1": jnp.concatenate([rw_v1[0 if first else l - 1],
                                   jnp.zeros((D, LANE - LORA_V), F32)], axis=1).astype(BF16),
            "v2": _pad_rows(rw_v2[0 if first else l - 1], LANE).astype(BF16),
            "k_k": row(rw_k_k[l]), "k_a": row(rw_k_a[l]),
        }
        p16, p32, u_t, feats = _inproj(h, row(mix_norm[l]), w_in_bf, l, _pack_w_tail(w_in_bf[l]),
                                       v_first, rp, seq, min(256, seq))
        if first:
            v_first = feats[3]
        o_hg, o_rw = _mix(p16, p32, feats, hg_lb_logits, row(hg_onorm[l]), row(rw_r_k[l]), row(rw_ln_w[l]),
                          row(rw_ln_b[l]), l, batch, seq, math.gcd(batch, 4), min(128, seq))

        y_t = _s5_core(u_t.reshape(t // CHUNK, S5_GROUPS * S5_ROW), s5_ops, l, nc)
        ycore = y_t.reshape(t // CHUNK, D, CHUNK)
        eye8 = jnp.eye(LANE // S5_GROUP, dtype=F32)
        gw = s5_glu_w[l].reshape(D // LANE, LANE // S5_GROUP, S5_GROUP, S5_GROUP)
        glu_tiles = jnp.einsum("igjk,gh->igjhk", gw, eye8).reshape(D // LANE, LANE, LANE).astype(BF16)
        h = _tail(h, o_hg, o_rw, ycore, p16, row(s5_d[l]), glu_tiles, row(s5_glu_b[l]),
                  w_branch[l].reshape(3, D, D).astype(BF16), w_out[l].astype(BF16),
                  row(ffn_norm[l]), ffn_w_gate[l].astype(BF16), ffn_w_up[l].astype(BF16),
                  ffn_w_down[l].astype(BF16), row(final_norm), l == depth - 1, tm)
    return h.reshape(batch, seq, D)
```

```python
import functools
import math

import numpy as np
import jax
import jax.numpy as jnp
from jax import lax
from jax.experimental import pallas as pl
from jax.experimental.pallas import tpu as pltpu

F32 = jnp.float32
BF16 = jnp.bfloat16

D = 1024
DEPTH = 4
NORM_EPS = 1e-6
CHUNK = 64
LANE = 128
HG_HEADS = 8
RW_HEAD = 64
RW_GN_EPS = 64e-5
LORA_W, LORA_A, LORA_G, LORA_V = 64, 64, 160, 32
S5_GROUPS, S5_GROUP, S5_STATE = 64, 16, 64
FFN_HIDDEN = 2816
FFN_CHUNK = 256

C16_Q, C16_I, C16_OG, C16_S5, C16_GATE = 0, 1, 2, 3, 4
C32_F = 0
LOG2E = 1.4426950408889634
VMEM_LIMIT = 56 * 1024 * 1024


def _cparams(*sem):
    return pltpu.CompilerParams(dimension_semantics=sem, vmem_limit_bytes=VMEM_LIMIT)


def _split3(x):
    hi = x.astype(BF16)
    r1 = x - hi.astype(F32)
    mid = r1.astype(BF16)
    lo = (r1 - mid.astype(F32)).astype(BF16)
    return hi, mid, lo


def _dot(a, b):
    return jnp.dot(a, b, preferred_element_type=F32)


def _dot_nt(a, b):
    return lax.dot_general(a, b, (((1,), (1,)), ((), ())), preferred_element_type=F32)


def _exact_dot01(m3, x):
    hi, mid, lo = _split3(x)
    return _dot(m3, jnp.concatenate([hi, mid, lo], axis=0))


def _rms(x, g):
    ms = jnp.mean(x * x, axis=-1, keepdims=True)
    return x * lax.rsqrt(ms + NORM_EPS) * g


def _neg_abs(x):
    return -jnp.abs(x)


def _sigmoid(x):
    return 1.0 / (1.0 + jnp.exp(-x))


def _silu(x):
    return x * _sigmoid(x)


N_ALIGNED = 7
N16 = (C16_GATE + 3) * D
RW_PARAMS = ("mu", "mu_lora", "w0", "w2", "a0", "a2", "g2", "v0", "v1", "v2", "k_k", "k_a")


def _inproj_body(first, steps_per_seq, x_ref, g_ref, *refs):
    w_refs, wt_ref = refs[:N_ALIGNED], refs[N_ALIGNED]
    refs = refs[N_ALIGNED + 1:]
    if not first:
        vf_ref, refs = refs[0], refs[1:]
    p = dict(zip(RW_PARAMS, refs[:len(RW_PARAMS)]))
    ones_ref = refs[len(RW_PARAMS)]
    (o16_ref, o32_ref, ut_ref, ro_ref, lw_ref, ko_ref, vo_ref, kko_ref, bo_ref, go_ref,
     xn_ref, cr_ref, cl_ref) = refs[len(RW_PARAMS) + 1:]
    tm = x_ref.shape[0]

    @pl.when(pl.program_id(0) % steps_per_seq == 0)
    def _():
        cr_ref[...] = jnp.zeros_like(cr_ref)
        cl_ref[...] = jnp.zeros_like(cl_ref)

    xn_ref[...] = _rms(x_ref[...], g_ref[...]).astype(BF16)
    blk = lambda j: slice(j * D, (j + 1) * D)
    proj = lambda w: _dot(xn_ref[...], w)
    zl = proj(wt_ref[:, 4 * D:4 * D + 4 * LANE])
    zr = proj(w_refs[4][...])
    zk = proj(w_refs[5][...])
    zv = proj(w_refs[6][...])
    o16_ref[:, blk(C16_Q)] = proj(w_refs[0][...]).astype(BF16)
    o32_ref[...] = proj(w_refs[1][...])
    o16_ref[:, blk(C16_I)] = proj(w_refs[2][...]).astype(BF16)
    o16_ref[:, blk(C16_OG)] = proj(w_refs[3][...]).astype(BF16)

    def mix(z, carry, mu):
        row = lax.broadcasted_iota(jnp.int32, z.shape, 0)
        zs = jnp.where(row == 0, carry, pltpu.roll(z, 1, 0))
        return z + mu * (zs - z)

    def softplus(x):
        return jnp.maximum(x, 0.0) + jnp.log(1.0 + jnp.exp(_neg_abs(x)))

    lora = mix(zl, cl_ref[0:1, :], p["mu_lora"][...])
    cl_ref[0:1, :] = zl[tm - 1:tm, :]
    wd = jnp.tanh(lora[:, 0:LANE]).astype(BF16)
    ad = lora[:, LANE:2 * LANE].astype(BF16)
    gd = _sigmoid(lora[:, 2 * LANE:4 * LANE]).astype(BF16)
    w_log = -softplus(-(p["w0"][...] + _dot(wd, p["w2"][...]))) - 0.5
    lw_ref[...] = -jnp.exp(w_log)
    a = _sigmoid(p["a0"][...] + _dot(ad, p["a2"][...]))
    go_ref[...] = _dot(gd, p["g2"][...]).astype(go_ref.dtype)

    ro_ref[...] = mix(zr, cr_ref[0:1, :], p["mu"][0:1, :]).astype(ro_ref.dtype)
    cr_ref[0:1, :] = zr[tm - 1:tm, :]
    k = mix(zk, cr_ref[1:2, :], p["mu"][1:2, :])
    cr_ref[1:2, :] = zk[tm - 1:tm, :]
    v = mix(zv, cr_ref[2:3, :], p["mu"][2:3, :])
    cr_ref[2:3, :] = zv[tm - 1:tm, :]
    if not first:
        lowr = _dot(v.astype(BF16), p["v1"][...]).astype(BF16)
        v = v + (vf_ref[...].astype(F32) - v) * _sigmoid(p["v0"][...] + _dot(lowr, p["v2"][...]))
    vo_ref[...] = v.astype(vo_ref.dtype)

    kk = k * p["k_k"][...]
    ss = jnp.concatenate([_segsum(kk[:, i * LANE:(i + 1) * LANE] * kk[:, i * LANE:(i + 1) * LANE], ones_ref[...])
                          for i in range(D // LANE)], axis=1)
    kk = kk * lax.rsqrt(jnp.maximum(ss, 1e-24))
    kko_ref[...] = kk.astype(kko_ref.dtype)
    bo_ref[...] = (kk * a).astype(bo_ref.dtype)
    ko_ref[...] = (k * (1.0 + (a - 1.0) * p["k_a"][...])).astype(ko_ref.dtype)

    y = proj(wt_ref[:, blk(0)])
    o16_ref[:, blk(C16_S5)] = y.astype(BF16)
    for c in range(ut_ref.shape[0]):
        ut_ref[c] = y[c * CHUNK:(c + 1) * CHUNK, :].T.astype(ut_ref.dtype)
    for j in range(3):
        o16_ref[:, blk(C16_GATE + j)] = proj(wt_ref[:, blk(1 + j)]).astype(BF16)


def _inproj(h, g, w_all, layer, w_tail, v_first, rp, seq, tm):
    t = h.shape[0]
    first = v_first is None
    row = lambda i: (i, 0)
    fixed = lambda i: (0, 0)
    once = dict(pipeline_mode=pl.Buffered(1))
    rowblk = pl.BlockSpec((tm, D), row)
    aligned = [pl.BlockSpec((None, D, D), functools.partial(lambda j, i: (layer, 0, j), j), **once)
               for j in range(N_ALIGNED)]
    params = [rp[k] for k in RW_PARAMS] + [_ones_bd()]
    bf = jax.ShapeDtypeStruct((t, D), BF16)
    f32 = jax.ShapeDtypeStruct((t, D), F32)
    outs = pl.pallas_call(
        functools.partial(_inproj_body, first, seq // tm),
        grid=(t // tm,),
        in_specs=[rowblk, pl.BlockSpec((1, D), fixed)] + aligned + [pl.BlockSpec(w_tail.shape, fixed, **once)]
                 + ([] if first else [rowblk]) + [pl.BlockSpec(a.shape, fixed) for a in params],
        out_specs=[pl.BlockSpec((tm, N16), row), rowblk,
                   pl.BlockSpec((tm // CHUNK, D, CHUNK), lambda i: (i, 0, 0))] + [rowblk] * 7,
        out_shape=[jax.ShapeDtypeStruct((t, N16), BF16), f32,
                   jax.ShapeDtypeStruct((t // CHUNK, D, CHUNK), BF16), bf, f32, bf, bf, bf, bf, bf],
        scratch_shapes=[pltpu.VMEM((tm, D), BF16), pltpu.VMEM((8, D), F32), pltpu.VMEM((8, 4 * LANE), F32)],
        compiler_params=_cparams("arbitrary"),
        name="inproj",
    )(h, g, *([w_all] * N_ALIGNED), w_tail, *([] if first else [v_first]), *params)
    return outs[0], outs[1], outs[2], outs[3:]


def _tail_body(final, h_ref, ohg_ref, orw_ref, y_ref, u_ref, g0_ref, g1_ref, g2_ref,
               d_ref, gw_ref, gb_ref, wb_ref, wo_ref, fn_ref, wg_ref, wu_ref, wd_ref, fg_ref,
               o_ref, yt_ref, s5_ref, xn_ref, acc_ref):
    for c in range(y_ref.shape[0]):
        yt_ref[c * CHUNK:(c + 1) * CHUNK, :] = y_ref[c].astype(F32).T
    for i in range(D // LANE):
        ln = slice(i * LANE, (i + 1) * LANE)
        y = yt_ref[:, ln] + d_ref[:, ln] * u_ref[:, ln].astype(F32)
        zg = 0.5 * y * (1.0 + jnp.tanh(math.sqrt(2.0 / math.pi) * (y + 0.044715 * (y * y * y))))
        gate = _dot(zg.astype(BF16), gw_ref[i]) + gb_ref[:, ln]
        s5_ref[:, ln] = (zg * _sigmoid(gate)).astype(BF16)

    m = _sigmoid(g0_ref[...].astype(F32)) * _dot(ohg_ref[...], wb_ref[0])
    m = m + _sigmoid(g1_ref[...].astype(F32)) * _dot(orw_ref[...], wb_ref[1])
    m = m + _sigmoid(g2_ref[...].astype(F32)) * _dot(s5_ref[...], wb_ref[2])
    x = h_ref[...] + _dot(m.astype(BF16), wo_ref[...])

    xn_ref[...] = _rms(x, fn_ref[...]).astype(BF16)
    for c in range(FFN_HIDDEN // FFN_CHUNK):
        sl = slice(c * FFN_CHUNK, (c + 1) * FFN_CHUNK)
        a = _dot(xn_ref[...], wg_ref[:, sl])
        b = _dot(xn_ref[...], wu_ref[:, sl])
        part = _dot((_silu(a) * b).astype(BF16), wd_ref[sl, :])
        if c == 0:
            acc_ref[...] = x + part
        else:
            acc_ref[...] += part
    out = acc_ref[...]
    if final:
        out = _rms(out, fg_ref[...])
    o_ref[...] = out


def _tail(h, o_hg, o_rw, ycore, p16, d, glu_tiles, glu_b, wb, wo, fn, wg, wu, wd, fg, final, tm):
    t = h.shape[0]
    row = lambda i: (i, 0)
    once = dict(pipeline_mode=pl.Buffered(1))
    vec = pl.BlockSpec((1, D), lambda i: (0, 0))
    rowblk = pl.BlockSpec((tm, D), row)
    col = lambda j: pl.BlockSpec((tm, D), lambda i: (i, j))
    return pl.pallas_call(
        functools.partial(_tail_body, final),
        grid=(t // tm,),
        in_specs=[rowblk, rowblk, rowblk,
                  pl.BlockSpec((tm // CHUNK, D, CHUNK), lambda i: (i, 0, 0)),
                  col(C16_S5), col(C16_GATE), col(C16_GATE + 1), col(C16_GATE + 2),
                  vec, pl.BlockSpec((D // LANE, LANE, LANE), lambda i: (0, 0, 0), **once), vec,
                  pl.BlockSpec((3, D, D), lambda i: (0, 0, 0), **once),
                  pl.BlockSpec((D, D), lambda i: (0, 0), **once),
                  vec,
                  pl.BlockSpec((D, FFN_HIDDEN), lambda i: (0, 0), **once),
                  pl.BlockSpec((D, FFN_HIDDEN), lambda i: (0, 0), **once),
                  pl.BlockSpec((FFN_HIDDEN, D), lambda i: (0, 0), **once),
                  vec],
        out_specs=rowblk,
        out_shape=jax.ShapeDtypeStruct((t, D), F32),
        scratch_shapes=[pltpu.VMEM((tm, D), F32), pltpu.VMEM((tm, D), BF16),
                        pltpu.VMEM((tm, D), BF16), pltpu.VMEM((tm, D), F32)],
        compiler_params=_cparams("parallel"),
        name="tail",
    )(h, o_hg, o_rw, ycore, p16, p16, p16, p16, d, glu_tiles, glu_b, wb, wo, fn, wg, wu, wd, fg)


HG_LEVELS = (32, 16, 8, 4, 2, 1)


def _hg_dmat():
    t = np.arange(CHUNK)
    tri = (t[None, :] <= t[:, None]).astype(np.float32)
    blocks = [tri]
    for m in HG_LEVELS:
        r = (t // (2 * m)) * (2 * m) + m - 1
        blocks.append(tri - tri[r])
    d = np.concatenate(blocks, axis=0)
    return jnp.asarray(np.concatenate([d, d, d], axis=1), BF16)


S5_ROW = CHUNK * S5_GROUP


def _s5_consts():
    tj = np.arange(S5_ROW)
    e_tj = (tj[None, :] % CHUNK == np.arange(CHUNK)[:, None]).astype(np.float32)
    f_tj = (tj[None, :] // CHUNK == np.arange(S5_GROUP)[:, None]).astype(np.float32)
    return (jnp.asarray(e_tj, BF16), jnp.asarray(f_tj, BF16),
            jnp.asarray(e_tj.T, BF16), jnp.asarray(f_tj.T, BF16))


def _right01(x, e):
    hi, mid, lo = _split3(x)
    return _dot(hi, e) + _dot(mid, e) + _dot(lo, e)


def _left01(e, x):
    hi, mid, lo = _split3(x)
    return _dot(e, hi) + _dot(e, mid) + _dot(e, lo)


def _dot_f32(a, b):
    ah, am, al = _split3(a)
    bh, bm, bl = _split3(b)
    return (_dot(ah, bh) + _dot(ah, bm) + _dot(am, bh)
            + _dot(ah, bl) + _dot(al, bh) + _dot(am, bm))


def _s5_prep_body(lrr_ref, lir_ref, lrc_ref, lic_ref, ls_ref, brt_ref, bit_ref, crt_ref, cit_ref,
                  etj_ref, ftj_ref, esh_ref, fsh_ref,
                  wt_ref, wi_ref, wis_ref, wo_ref, a1_ref, a2_ref):
    dt = jnp.exp(ls_ref[...])
    lam_re, lam_im = lrr_ref[...], lir_ref[...]
    lr_row, li_row = lam_re * dt, lam_im * dt
    lr_col, li_col = lrc_ref[...] * dt, lic_ref[...] * dt
    re_half = lax.broadcasted_iota(jnp.int32, (1, 2 * S5_STATE), 1) < S5_STATE

    mag = jnp.exp(lr_row)
    ab_re, ab_im = mag * jnp.cos(li_row), mag * jnp.sin(li_row)
    den = lam_re * lam_re + lam_im * lam_im
    nr, ni = ab_re - 1.0, ab_im
    coef_re = (nr * lam_re + ni * lam_im) / den
    coef_im = (ni * lam_re - nr * lam_im) / den
    bb_re = coef_re * brt_ref[...] - coef_im * bit_ref[...]
    bb_im = coef_re * bit_ref[...] + coef_im * brt_ref[...]

    def powers(lr, li, n):
        m = jnp.exp(lr * n)
        return m * jnp.cos(li * n), m * jnp.sin(li * n)

    tau = lax.broadcasted_iota(jnp.int32, (1, CHUNK), 1).astype(F32)
    p0_re, p0_im = powers(lr_col, li_col, tau)
    ac_re, ac_im = powers(lr_col, li_col, 1.0)
    p1_re, p1_im = p0_re * ac_re - p0_im * ac_im, p0_re * ac_im + p0_im * ac_re
    c_re = _right01(crt_ref[...], ftj_ref[...])
    c_im = _right01(cit_ref[...], ftj_ref[...])
    p0e_re, p0e_im = _right01(p0_re, etj_ref[...]), _right01(p0_im, etj_ref[...])
    p1e_re, p1e_im = _right01(p1_re, etj_ref[...]), _right01(p1_im, etj_ref[...])

    m_stack = jnp.concatenate([c_re * p0e_re - c_im * p0e_im, c_re * p0e_im + c_im * p0e_re], axis=0)
    r0 = _dot_f32(jnp.where(re_half, bb_re, -bb_im), m_stack)
    t_lane = lax.broadcasted_iota(jnp.int32, (CHUNK, S5_ROW), 1) & (CHUNK - 1)
    s_row = lax.broadcasted_iota(jnp.int32, (CHUNK, S5_ROW), 0)
    for h in range(S5_GROUP):
        base = jnp.broadcast_to(r0[h:h + 1, :], (CHUNK, S5_ROW))
        rolled = pltpu.roll(base, 0, 1, stride=1, stride_axis=0)
        wt_ref[h * CHUNK:(h + 1) * CHUNK, :] = jnp.where(t_lane >= s_row, rolled, 0.0).astype(BF16)

    wo_ref[0:S5_STATE, :] = (c_re * p1e_re - c_im * p1e_im).astype(BF16)
    wo_ref[S5_STATE:2 * S5_STATE, :] = (-(c_re * p1e_im + c_im * p1e_re)).astype(BF16)

    back = (CHUNK - 1.0) - lax.broadcasted_iota(jnp.int32, (CHUNK, 1), 0).astype(F32)
    pt_re, pt_im = powers(lr_row, li_row, back)
    pte_re, pte_im = _left01(esh_ref[...], pt_re), _left01(esh_ref[...], pt_im)
    bbe_re, bbe_im = _left01(fsh_ref[...], bb_re), _left01(fsh_ref[...], bb_im)
    w_re = pte_re * bbe_re - pte_im * bbe_im
    w_im = pte_re * bbe_im + pte_im * bbe_re
    wi_ref[...] = jnp.where(re_half, w_re, w_im).astype(BF16)
    wis_ref[...] = jnp.where(re_half, w_im, w_re).astype(BF16)

    k = lax.broadcasted_iota(jnp.int32, (8, 1), 0)
    n = (CHUNK * (1 << k)).astype(F32)
    an_re, an_im = powers(lr_row, li_row, n)
    a1_ref[...] = an_re
    a2_ref[...] = jnp.where(re_half, -an_im, an_im)


def _s5_prep(lam_re, lam_im, log_step, b_re, b_im, c_re, c_im):
    nl = lam_re.shape[0]
    ns = 2 * S5_STATE
    g4 = lambda r, c: pl.BlockSpec((None, None, r, c), lambda l, g: (l, g, 0, 0))
    cst = lambda r, c: pl.BlockSpec((r, c), lambda l, g: (0, 0))
    sds = lambda r, c, dt: jax.ShapeDtypeStruct((nl, S5_GROUPS, r, c), dt)
    twice = lambda a: jnp.concatenate([a, a], axis=-1)
    e_tj, f_tj, e_sh, f_sh = _s5_consts()
    return pl.pallas_call(
        _s5_prep_body,
        grid=(nl, S5_GROUPS),
        in_specs=[g4(1, ns), g4(1, ns), g4(S5_STATE, 1), g4(S5_STATE, 1), g4(1, 1),
                  g4(S5_GROUP, ns), g4(S5_GROUP, ns), g4(S5_STATE, S5_GROUP), g4(S5_STATE, S5_GROUP),
                  cst(CHUNK, S5_ROW), cst(S5_GROUP, S5_ROW), cst(S5_ROW, CHUNK), cst(S5_ROW, S5_GROUP)],
        out_specs=[g4(S5_ROW, S5_ROW), g4(S5_ROW, ns), g4(S5_ROW, ns), g4(ns, S5_ROW), g4(8, ns), g4(8, ns)],
        out_shape=[sds(S5_ROW, S5_ROW, BF16), sds(S5_ROW, ns, BF16), sds(S5_ROW, ns, BF16),
                   sds(ns, S5_ROW, BF16), sds(8, ns, F32), sds(8, ns, F32)],
        compiler_params=_cparams("parallel", "parallel"),
        name="s5_prep",
    )(twice(lam_re)[:, :, None, :], twice(lam_im)[:, :, None, :], lam_re[:, :, :, None], lam_im[:, :, :, None],
      log_step[:, :, None, None],
      twice(jnp.swapaxes(b_re, 2, 3)), twice(jnp.swapaxes(b_im, 2, 3)),
      jnp.swapaxes(c_re, 2, 3), jnp.swapaxes(c_im, 2, 3),
      e_tj, f_tj, e_sh, f_sh)


def _s5_body(nc, u_ref, wt_ref, wi_ref, wis_ref, wo_ref, a1_ref, a2_ref, y_ref):
    u = u_ref[...]
    m = u.shape[0]
    x = _dot(u, wi_ref[...])
    xs = _dot(u, wis_ref[...])
    chunk = lax.broadcasted_iota(jnp.int32, (m, 2 * S5_STATE), 0) % nc

    def shifted(z, n):
        return jnp.where(chunk >= n, pltpu.roll(z, n, 0), 0.0)

    k = 0
    while (1 << k) < nc:
        a1, a2 = a1_ref[k:k + 1, :], a2_ref[k:k + 1, :]
        sx, sxs = shifted(x, 1 << k), shifted(xs, 1 << k)
        x, xs = x + a1 * sx + a2 * sxs, xs + a1 * sxs - a2 * sx
        k += 1
    y = _dot(u, wt_ref[...]) + _dot(shifted(x, 1).astype(BF16), wo_ref[...])
    y_ref[...] = y.astype(y_ref.dtype)


def _s5_core(u_t, ops, layer, nc):
    wt, wi, wis, wo, a1, a2 = ops
    m = u_t.shape[0]
    ns = 2 * S5_STATE
    grp = pl.BlockSpec((m, S5_ROW), lambda g: (0, g))
    lg = lambda r, c: pl.BlockSpec((None, None, r, c), lambda g: (layer, g, 0, 0))
    return pl.pallas_call(
        functools.partial(_s5_body, nc),
        grid=(S5_GROUPS,),
        in_specs=[grp, lg(S5_ROW, S5_ROW), lg(S5_ROW, ns), lg(S5_ROW, ns),
                  lg(ns, S5_ROW), lg(8, ns), lg(8, ns)],
        out_specs=grp,
        out_shape=jax.ShapeDtypeStruct((m, S5_GROUPS * S5_ROW), BF16),
        compiler_params=_cparams("parallel"),
        name="s5_core",
    )(u_t, wt, wi, wis, wo, a1, a2)


def _ones_bd():
    i = np.arange(LANE)
    return jnp.asarray((i[:, None] // RW_HEAD == i[None, :] // RW_HEAD).astype(np.float32), BF16)


def _segsum(x, ones_bd):
    hi = x.astype(BF16)
    lo = (x - hi.astype(F32)).astype(BF16)
    return _dot(hi, ones_bd) + _dot(lo, ones_bd)


RW_LEVELS = (2, 4, 8, 16, 32)


def _mix_body(layer, q_ref, f_ref, i_ref, og_ref, lbl_ref, on_ref, dm_ref,
              r_ref, lw_ref, k_ref, v_ref, kk_ref, b_ref, g_ref, rk_ref, lnw_ref, lnb_ref, tri_ref, ones_ref,
              ohg_ref, orw_ref, st_ref, d_ref, ht_ref, lg_ref):
    nbat, tb = r_ref.shape[0], r_ref.shape[1]
    ntile = D // LANE

    @pl.when(pl.program_id(1) == 0)
    def _():
        st_ref[...] = jnp.zeros_like(st_ref)
        ht_ref[...] = jnp.zeros_like(ht_ref)

    lgt = lbl_ref[...]
    e = jnp.exp(lgt - jnp.max(lgt, axis=0, keepdims=True))
    sm = e / jnp.sum(e, axis=0, keepdims=True)
    lb = jnp.zeros((1, D), F32)
    for l in range(1, layer + 1):
        lb = lb + sm[l:l + 1, :]
    lb = jnp.maximum(lb, 0.0)
    log_lb = jnp.log(lb)
    log_1mlb = jnp.log1p(-lb)
    one_mlb = 1.0 - lb

    ti = lax.broadcasted_iota(jnp.int32, (CHUNK, CHUNK), 0)
    si = lax.broadcasted_iota(jnp.int32, (CHUNK, CHUNK), 1)
    pair_masks = [(ti > si) & ((ti ^ si) >= m) & ((ti ^ si) < 2 * m) for m in HG_LEVELS]
    diag_mask = ti == si
    trow = lax.broadcasted_iota(jnp.int32, (CHUNK, LANE), 0)
    second = [(trow & m) != 0 for m in HG_LEVELS]

    l64 = lax.broadcasted_iota(jnp.int32, (CHUNK, LANE), 1)
    s64 = l64 & (RW_HEAD - 1)
    strict = trow > s64
    incl = trow >= s64
    eye = trow == s64
    xr = trow ^ s64
    lvl1 = strict & (xr == 1)
    lvl = [strict & (xr >= m) & (xr < 2 * m) for m in RW_LEVELS]
    head0 = l64 < RW_HEAD
    r128 = lax.broadcasted_iota(jnp.int32, (LANE, LANE), 0)
    l128 = lax.broadcasted_iota(jnp.int32, (LANE, LANE), 1)
    same_head = (r128 >> 6) == (l128 >> 6)

    def stack(x):
        return jnp.concatenate([jnp.where(head0, x, 0.0), jnp.where(head0, 0.0, x)], axis=0).astype(BF16)

    def chunk_step(c, carry):
        rows = pl.ds(pl.multiple_of(c * CHUNK, CHUNK), CHUNK)
        probs = [(bi, j) for bi in range(nbat) for j in range(ntile)]
        n = range(len(probs))
        lanes = lambda p: slice(probs[p][1] * LANE, (probs[p][1] + 1) * LANE)

        def ld(ref, p):
            return ref[probs[p][0], rows, lanes(p)].astype(F32)

        def lgp(p):
            return lg_ref[probs[p][0], :, lanes(p)]

        def dlev(p, lev):
            return d_ref[probs[p][0], lev * CHUNK:(lev + 1) * CHUNK, lanes(p)]

        hg, rw = {}, {}

        def hg_gate():
            kk_all = []
            for bi in range(nbat):
                z = f_ref[bi, rows, :]
                ez = jnp.exp(_neg_abs(z))
                lsig = jnp.minimum(z, 0.0) - jnp.log(1.0 + ez)
                cterm = log_1mlb + lsig
                mx = jnp.maximum(log_lb, cterm)
                logf = mx + jnp.log(1.0 + jnp.exp(_neg_abs(log_lb - cterm)))
                d_ref[bi] = _exact_dot01(dm_ref[...], logf * LOG2E)
                kk_all.append(one_mlb * jnp.where(z >= 0.0, ez, 1.0) / (1.0 + ez))
            hg["q"] = [_silu(ld(q_ref, p)) for p in n]
            hg["k"] = [kk_all[probs[p][0]][:, lanes(p)] for p in n]

        def hg_diag():
            s0 = [_dot_nt(hg["q"][p].astype(BF16), hg["k"][p].astype(BF16)) for p in n]
            hg["scores"] = [jnp.where(diag_mask, s0[p], 0.0) for p in n]

        def hg_level(li):
            def run():
                x = [(jnp.where(second[li], hg["q"][p], hg["k"][p])
                      * jnp.exp2(_neg_abs(dlev(p, li + 1)))).astype(BF16) for p in n]
                sl = [_dot_nt(x[p], x[p]) for p in n]
                hg["scores"] = [jnp.where(pair_masks[li], sl[p], hg["scores"][p]) for p in n]
            return run

        def hg_out():
            o = []
            for p in n:
                qe = (hg["q"][p] * jnp.exp2(dlev(p, 0))).astype(BF16)
                o.append(_dot(hg["scores"][p].astype(BF16), i_ref[probs[p][0], rows, lanes(p)].astype(BF16))
                         + _dot_nt(qe, st_ref[p].astype(BF16)))
            hg["o"] = o

        def hg_state():
            for p in n:
                b = dlev(p, 0)
                b_last = b[CHUNK - 1:CHUNK, :]
                kd = (hg["k"][p] * jnp.exp2(b_last - b)).astype(BF16)
                vt = ld(i_ref, p).T.astype(BF16)
                st_ref[p] = st_ref[p] * jnp.exp2(b_last) + _dot(vt, kd)

        def hg_store():
            for p in n:
                o = hg["o"][p]
                o = o * lax.rsqrt(jnp.mean(o * o, axis=-1, keepdims=True) + NORM_EPS)
                o = o * on_ref[:, lanes(p)] * _silu(ld(og_ref, p))
                ohg_ref[probs[p][0], rows, lanes(p)] = o.astype(ohg_ref.dtype)

        def rw_amat():
            for bi in range(nbat):
                lg_ref[bi] = _exact_dot01(tri_ref[...], lw_ref[bi, rows, :])
            a_t, r_t, a_ak, a_rk, l_ab, a_rb, x = [], [], [], [], [], [], []
            for p in n:
                lg = lgp(p)
                e_neg = jnp.exp(-lg)
                a_t.append(-ld(kk_ref, p) * jnp.exp(lg - ld(lw_ref, p)))
                r_t.append(ld(r_ref, p) * jnp.exp(lg))
                lhs = jnp.concatenate([a_t[p], r_t[p]], axis=0).astype(BF16)
                kb_neg = jnp.concatenate([stack(ld(k_ref, p) * e_neg), stack(ld(b_ref, p) * e_neg)], axis=0)
                a_kb = _dot_nt(lhs, kb_neg)
                a_ak.append(jnp.where(strict, a_kb[0:CHUNK, 0:LANE], 0.0))
                a_rk.append(jnp.where(incl, a_kb[CHUNK:2 * CHUNK, 0:LANE], 0.0))
                l_ab.append(jnp.where(strict, a_kb[0:CHUNK, LANE:2 * LANE], 0.0))
                a_rb.append(jnp.where(incl, a_kb[CHUNK:2 * CHUNK, LANE:2 * LANE], 0.0))
                x.append(jnp.where(eye, 1.0, 0.0) + jnp.where(lvl1, l_ab[p], 0.0))
            rw.update(a_t=a_t, r_t=r_t, a_ak=a_ak, a_rk=a_rk, l_ab=l_ab, a_rb=a_rb, x=x)

        def rw_inv_a(li):
            def run():
                rw["z"] = [_dot(rw["x"][p].astype(BF16), stack(jnp.where(lvl[li], rw["l_ab"][p], 0.0))) for p in n]
            return run

        def rw_inv_b():
            rw["x"] = [rw["x"][p] + _dot(rw["z"][p].astype(BF16), stack(rw["x"][p])) for p in n]

        def rw_mv():
            rw["v_st"] = [stack(ld(v_ref, p)) for p in n]
            rw["mv"] = [_dot(rw["a_ak"][p].astype(BF16), rw["v_st"][p]) for p in n]

        def rw_u0w():
            rw["u0w"] = [_dot(rw["x"][p].astype(BF16),
                              jnp.concatenate([stack(rw["mv"][p]), stack(rw["a_t"][p])], axis=1)) for p in n]

        def rw_u():
            rw["htb"] = [ht_ref[p].astype(BF16) for p in n]
            rw["u"] = [rw["u0w"][p][:, 0:LANE] + _dot_nt(rw["u0w"][p][:, LANE:2 * LANE].astype(BF16), rw["htb"][p])
                       for p in n]

        def rw_y():
            rw["y"] = [_dot_nt(rw["r_t"][p].astype(BF16), rw["htb"][p])
                       + _dot(rw["a_rk"][p].astype(BF16), rw["v_st"][p])
                       + _dot(rw["a_rb"][p].astype(BF16), stack(rw["u"][p])) for p in n]

        def rw_state():
            for p in n:
                lg = lgp(p)
                lg_end = lg[CHUNK - 1:CHUNK, :]
                e_end = jnp.exp(lg_end - lg)
                vu = jnp.concatenate([ld(v_ref, p), rw["u"][p]], axis=0)
                kb = jnp.concatenate([ld(k_ref, p) * e_end, ld(b_ref, p) * e_end], axis=0).astype(BF16)
                upd = _dot(vu.T.astype(BF16), kb)
                ht_ref[p] = ht_ref[p] * jnp.exp(lg_end) + jnp.where(same_head, upd, 0.0)

        def rw_store():
            ones = ones_ref[...]
            y = rw["y"]
            sl = lambda a, p: a[p * CHUNK:(p + 1) * CHUNK]
            mean = _segsum(jnp.concatenate(y, axis=0), ones) * (1.0 / RW_HEAD)
            yc = [y[p] - sl(mean, p) for p in n]
            var = _dot(jnp.concatenate([(c_ * c_).astype(BF16) for c_ in yc], axis=0), ones) * (1.0 / RW_HEAD)
            rk = [(ld(r_ref, p) * ld(k_ref, p) * rk_ref[:, lanes(p)]).astype(BF16) for p in n]
            bonus = _dot(jnp.concatenate(rk, axis=0), ones)
            for p in n:
                yn = yc[p] * lax.rsqrt(sl(var, p) + RW_GN_EPS) * lnw_ref[:, lanes(p)] + lnb_ref[:, lanes(p)]
                orw_ref[probs[p][0], rows, lanes(p)] = (
                    (yn + sl(bonus, p) * ld(v_ref, p)) * ld(g_ref, p)).astype(orw_ref.dtype)

        rw_stages = [rw_amat]
        for li in range(len(RW_LEVELS)):
            rw_stages += [rw_inv_a(li), rw_inv_b]
        rw_stages += [rw_mv, rw_u0w, rw_u, rw_y, rw_state, rw_store]
        hg_stages = [hg_gate, hg_diag] + [hg_level(li) for li in range(len(HG_LEVELS))] + [hg_out, hg_state, hg_store]
        for s in range(max(len(rw_stages), len(hg_stages))):
            if s < len(rw_stages):
                rw_stages[s]()
            if s < len(hg_stages):
                hg_stages[s]()
        return carry

    lax.fori_loop(0, tb // CHUNK, chunk_step, 0)


def _mix(p16, p32, feats, lb_logits, onorm, r_k, ln_w, ln_b, layer, batch, seq, nbat, tb):
    t = p16.shape[0]
    fixed = lambda b, s: (0, 0)
    vec = pl.BlockSpec((1, D), fixed)
    col = lambda j: pl.BlockSpec((nbat, tb, D), lambda b, s: (b, s, j))
    blk = col(0)
    tri = np.tril(np.ones((CHUNK, CHUNK), np.float32))
    tri3 = jnp.asarray(np.concatenate([tri, tri, tri], axis=1), BF16)
    p16_3, p32_3 = p16.reshape(batch, seq, -1), p32.reshape(batch, seq, -1)
    out = jax.ShapeDtypeStruct((batch, seq, D), BF16)
    o_hg, o_rw = pl.pallas_call(
        functools.partial(_mix_body, layer),
        grid=(batch // nbat, seq // tb),
        in_specs=[col(C16_Q), col(C32_F), col(C16_I), col(C16_OG),
                  pl.BlockSpec(lb_logits.shape, fixed), vec, pl.BlockSpec((7 * CHUNK, 3 * CHUNK), fixed)]
                 + [blk] * 7 + [vec, vec, vec, pl.BlockSpec((CHUNK, 3 * CHUNK), fixed),
                                pl.BlockSpec((LANE, LANE), fixed)],
        out_specs=[blk, blk],
        out_shape=[out, out],
        scratch_shapes=[pltpu.VMEM((nbat * HG_HEADS, LANE, LANE), F32), pltpu.VMEM((nbat, 7 * CHUNK, D), F32),
                        pltpu.VMEM((nbat * (D // LANE), LANE, LANE), F32), pltpu.VMEM((nbat, CHUNK, D), F32)],
        compiler_params=_cparams("parallel", "arbitrary"),
        name="mix",
    )(p16_3, p32_3, p16_3, p16_3, lb_logits, onorm, _hg_dmat(),
      *[f.reshape(batch, seq, D) for f in feats], r_k, ln_w, ln_b, tri3, _ones_bd())
    return o_hg.reshape(t, D), o_rw.reshape(t, D)


def _pack_w_tail(w):
    o_lora = N_ALIGNED * D
    o_a, o_g = o_lora + LORA_W, o_lora + LORA_W + LORA_A
    o_s5 = o_g + LORA_G
    z = lambda n: jnp.zeros((D, n), w.dtype)
    return jnp.concatenate([w[:, o_s5:o_s5 + 4 * D], w[:, o_lora:o_a], z(LANE - LORA_W), w[:, o_a:o_g],
                            z(LANE - LORA_A), w[:, o_g:o_s5], z(2 * LANE - LORA_G)], axis=1)


def _pad_rows(w, n):
    return jnp.concatenate([w, jnp.zeros((n - w.shape[0], w.shape[1]), w.dtype)], axis=0)


def _pad_lanes(v, n):
    return jnp.concatenate([v, jnp.zeros((n - v.shape[0],), v.dtype)])


def kernel(x, mix_norm, w_in, hg_lb_logits, hg_onorm, rw_shift_mu, rw_w0, rw_w2, rw_a0, rw_a2, rw_g2, rw_v0, rw_v1, rw_v2, rw_k_k, rw_k_a, rw_r_k, rw_ln_w, rw_ln_b, s5_lambda_re, s5_lambda_im, s5_log_step, s5_b_re, s5_b_im, s5_c_re, s5_c_im, s5_d, s5_glu_w, s5_glu_b, w_branch, w_out, ffn_norm, ffn_w_gate, ffn_w_up, ffn_w_down, final_norm):
    batch, seq, _ = x.shape
    t = batch * seq
    nc = seq // CHUNK
    depth = w_in.shape[0]
    tm = min(512, t)
    tb = min(256, seq)
    h = x.reshape(t, D)
    row = lambda v: v.reshape(1, -1)

    s5_ops = _s5_prep(s5_lambda_re, s5_lambda_im, s5_log_step, s5_b_re, s5_b_im, s5_c_re, s5_c_im)
    w_in_bf = w_in.astype(BF16)
    v_first = None
    for l in range(depth):
        mu = rw_shift_mu[l]
        o1, o2, o3 = 3 * D + LORA_W, 3 * D + LORA_W + LORA_A, 3 * D + LORA_W + LORA_A + LORA_G
        first = l == 0
        rp = {
            "mu": mu[0:3 * D].reshape(3, D),
            "mu_lora": row(jnp.concatenate([_pad_lanes(mu[3 * D:o1], LANE), _pad_lanes(mu[o1:o2], LANE),
                                            _pad_lanes(mu[o2:o3], 2 * LANE)])),
            "w0": row(rw_w0[l]), "w2": _pad_rows(rw_w2[l], LANE).astype(BF16),
            "a0": row(rw_a0[l]), "a2": _pad_rows(rw_a2[l], LANE).astype(BF16),
            "g2": _pad_rows(rw_g2[l], 2 * LANE).astype(BF16),
            "v0": row(rw_v0[0 if first else l - 1]),
            "v1": jnp.concatenate([rw_v1[0 if first else l - 1],
                                   jnp.zeros((D, LANE - LORA_V), F32)], axis=1).astype(BF16),
            "v2": _pad_rows(rw_v2[0 if first else l - 1], LANE).astype(BF16),
            "k_k": row(rw_k_k[l]), "k_a": row(rw_k_a[l]),
        }
        p16, p32, u_t, feats = _inproj(h, row(mix_norm[l]), w_in_bf, l, _pack_w_tail(w_in_bf[l]),
                                       v_first, rp, seq, min(256, seq))
        if first:
            v_first = feats[3]
        o_hg, o_rw = _mix(p16, p32, feats, hg_lb_logits, row(hg_onorm[l]), row(rw_r_k[l]), row(rw_ln_w[l]),
                          row(rw_ln_b[l]), l, batch, seq, math.gcd(batch, 4), min(128, seq))

        y_t = _s5_core(u_t.reshape(t // CHUNK, S5_GROUPS * S5_ROW), s5_ops, l, nc)
        ycore = y_t.reshape(t // CHUNK, D, CHUNK)
        eye8 = jnp.eye(LANE // S5_GROUP, dtype=F32)
        gw = s5_glu_w[l].reshape(D // LANE, LANE // S5_GROUP, S5_GROUP, S5_GROUP)
        glu_tiles = jnp.einsum("igjk,gh->igjhk", gw, eye8).reshape(D // LANE, LANE, LANE).astype(BF16)
        h = _tail(h, o_hg, o_rw, ycore, p16, row(s5_d[l]), glu_tiles, row(s5_glu_b[l]),
                  w_branch[l].reshape(3, D, D).astype(BF16), w_out[l].astype(BF16),
                  row(ffn_norm[l]), ffn_w_gate[l].astype(BF16), ffn_w_up[l].astype(BF16),
                  ffn_w_down[l].astype(BF16), row(final_norm), l == depth - 1, tm)
    return h.reshape(batch, seq, D)
```
